```python
import math
import jax, jax.numpy as jnp
from jax import lax
import numpy as np

D_MODEL = 1024
BATCH = 8
SEQ = 2048
DEPTH = 2

GRID_W = 64
CTX_LEN = 256
MIX_W = D_MODEL
GROUP_W = MIX_W // 4
ATTN_HEADS = 4
ATTN_D = GROUP_W // (2 * ATTN_HEADS)
ATTN_VD = 2 * ATTN_D
ROPE_THETA = 10000.0
Q_BLOCK = 128
POOL_WINDOWS = (2, 4, 8, 16)
POOL_GROUPS = 4
POOL_GW = GROUP_W // POOL_GROUPS
CONV_K = 31
SGU_CHUNK = 128
SGU_GROUPS = 4
SGU_GW = GROUP_W // SGU_GROUPS
N_EXPERTS = 16
EC_CAPACITY = 2
D_EXPERT = 2 * D_MODEL
N_MOD = 6
IN_W = 3 * GROUP_W + GROUP_W + 2 * GROUP_W + 2 * GROUP_W
SPLITS = (GROUP_W, 2 * GROUP_W, 3 * GROUP_W, 4 * GROUP_W, 6 * GROUP_W)
EPS = 1e-6

kernel_name = "hybrid_diffusion_parallel_heads_ec_moe"


def rms_norm(x, g):
    xf = x.astype(jnp.float32)
    y = xf * lax.rsqrt(jnp.mean(xf * xf, axis=-1, keepdims=True) + EPS)
    return (y * g.astype(jnp.float32)).astype(x.dtype)


def layer_norm(x, g, b):
    xf = x.astype(jnp.float32)
    mu = jnp.mean(xf, axis=-1, keepdims=True)
    var = jnp.mean(jnp.square(xf - mu), axis=-1, keepdims=True)
    y = (xf - mu) * lax.rsqrt(var + EPS) * g.astype(jnp.float32) + b.astype(jnp.float32)
    return y.astype(x.dtype)


def modulation(cond, w_mod, b_mod):
    m = jax.nn.silu(cond) @ w_mod + b_mod
    return jnp.split(m, N_MOD, axis=-1)


def axial_rope_tables(n_tokens, dtype):
    rows_n = n_tokens // GRID_W
    rows = jnp.repeat(jnp.arange(rows_n), GRID_W).astype(jnp.float32)
    cols = jnp.tile(jnp.arange(GRID_W), rows_n).astype(jnp.float32)
    n_freq = ATTN_D // 4
    inv = ROPE_THETA ** (-jnp.arange(n_freq, dtype=jnp.float32) / n_freq)
    ang_r = rows[:, None] * inv
    ang_c = cols[:, None] * inv
    shp = (n_tokens, 1, 1, n_freq)
    return tuple(t.reshape(shp).astype(dtype) for t in
                 (jnp.cos(ang_r), jnp.sin(ang_r), jnp.cos(ang_c), jnp.sin(ang_c)))


def rotate_axis(x, cos, sin):
    n = x.shape[-1] // 2
    x1, x2 = x[..., :n], x[..., n:]
    return jnp.concatenate([x1 * cos - x2 * sin, x1 * sin + x2 * cos], axis=-1)


def apply_rope2d(x, rope):
    cos_r, sin_r, cos_c, sin_c = rope
    h = x.shape[-1] // 2
    return jnp.concatenate([rotate_axis(x[..., :h], cos_r, sin_r),
                            rotate_axis(x[..., h:], cos_c, sin_c)], axis=-1)


def qk_heads(p, g, rope):
    B, T, _ = p.shape
    a = rms_norm(p.reshape(B, T, ATTN_HEADS, 2, ATTN_D), g)
    if rope is not None:
        a = apply_rope2d(a, rope)
    return a[..., 0, :], a[..., 1, :]


def diff_softmax_attention(q1, q2, k1, k2, v, lam):
    scale = ATTN_D ** -0.5
    s1 = jnp.einsum('bqhd,bkhd->bhqk', q1, k1).astype(jnp.float32) * scale
    s2 = jnp.einsum('bqhd,bkhd->bhqk', q2, k2).astype(jnp.float32) * scale
    w = jax.nn.softmax(s1, axis=-1) - lam * jax.nn.softmax(s2, axis=-1)
    return jnp.einsum('bhqk,bkhe->bqhe', w.astype(v.dtype), v)


def latent_diff_attention(q1, q2, k1, k2, v, lam):
    B, T, H, _ = q1.shape
    nb = T // Q_BLOCK

    def blocks(q):
        return q.reshape(B, nb, Q_BLOCK, H, q.shape[-1]).swapaxes(0, 1)

    out = lax.map(lambda qs: diff_softmax_attention(qs[0], qs[1], k1, k2, v, lam),
                  (blocks(q1), blocks(q2)))
    return out.swapaxes(0, 1).reshape(B, T, H, ATTN_VD)


def attn_head_out(o, g, lam_init):
    B, T = o.shape[:2]
    return (rms_norm(o, g) * (1.0 - lam_init)).reshape(B, T, GROUP_W)


def pool_mixer(z, w_pool, b_pool, pool_scale):
    B, T, _ = z.shape
    zg = z.reshape(B, T, POOL_GROUPS, POOL_GW)
    csum = jnp.concatenate([jnp.zeros((B, 1, POOL_GROUPS, POOL_GW), jnp.float32),
                            jnp.cumsum(zg.astype(jnp.float32), axis=1)], axis=1)
    t = jnp.arange(T)
    outs = []
    for g, w in enumerate(POOL_WINDOWS):
        lo = jnp.clip(t - w // 2, 0, T)
        hi = jnp.clip(t + w // 2, 0, T)
        cnt = (hi - lo).astype(jnp.float32)[None, :, None]
        mean = (csum[:, hi, g] - csum[:, lo, g]) / cnt
        d = mean.astype(z.dtype) - zg[:, :, g]
        outs.append(d @ w_pool[g] + b_pool[g])
    return jnp.concatenate(outs, axis=-1) * pool_scale


def conv_module(z, conv_w, conv_b, ln_g, ln_b, w_pw2):
    val, gate = jnp.split(z, 2, axis=-1)
    y = val * jax.nn.sigmoid(gate)
    y = lax.conv_general_dilated(y, conv_w[:, None, :], window_strides=(1,),
                                 padding=[(CONV_K // 2, CONV_K // 2)],
                                 dimension_numbers=('NWC', 'WIO', 'NWC'),
                                 feature_group_count=GROUP_W) + conv_b
    y = jax.nn.silu(layer_norm(y, ln_g, ln_b))
    return y @ w_pw2


def spatial_gating(z, ln_g, ln_b, w_spatial, b_spatial):
    B, T, _ = z.shape
    u, v = jnp.split(jax.nn.gelu(z), 2, axis=-1)
    v = layer_norm(v, ln_g, ln_b).reshape(B, T // SGU_CHUNK, SGU_CHUNK, SGU_GROUPS, SGU_GW)
    s = jnp.einsum('gpq,bnqgc->bnpgc', w_spatial, v) + b_spatial.T[:, :, None]
    return u * s.reshape(B, T, GROUP_W)


def expert_choice_moe(h, w_router, w_gate, w_up, w_down):
    B, T, D = h.shape
    cap = EC_CAPACITY * T // N_EXPERTS
    aff = jax.nn.softmax((h @ w_router).astype(jnp.float32), axis=-1)
    gates, idx = lax.top_k(aff.swapaxes(1, 2), cap)
    xs = jax.vmap(lambda hb, ib: hb[ib])(h, idx)
    a = jnp.einsum('becd,edf->becf', xs, w_gate)
    u = jnp.einsum('becd,edf->becf', xs, w_up)
    y = jnp.einsum('becf,efd->becd', jax.nn.silu(a) * u, w_down)
    y = y * gates[..., None].astype(y.dtype)
    return jax.vmap(lambda ib, yb: jnp.zeros((T, D), yb.dtype).at[ib.reshape(-1)].add(
        yb.reshape(-1, D)))(idx, y)


def setup_inputs(seed: int = 0) -> dict:
    key = jax.random.key(seed)
    ks = iter(jax.random.split(key, 40))

    def nrm(shape, scale):
        return jax.random.normal(next(ks), shape, jnp.float32) * scale

    def gain(shape):
        return 1.0 + nrm(shape, 0.01)

    L, D = DEPTH, D_MODEL
    return {
        "x": nrm((BATCH, SEQ, D), 1.0),
        "c": nrm((BATCH, D), 1.0),
        "ctx": nrm((BATCH, CTX_LEN, D), 1.0),
        "c_ctx": nrm((D,), 1.0),
        "w_mod": nrm((L, D, N_MOD * D), 0.5 * D ** -0.5),
        "b_mod": nrm((L, N_MOD * D), 0.01),
        "g_norm1": gain((L, D)),
        "g_norm2": gain((L, D)),
        "w_in": nrm((L, D, IN_W), D ** -0.5),
        "w_out": nrm((L, MIX_W, D), MIX_W ** -0.5),
        "g_q": gain((L, ATTN_D)),
        "g_k": gain((L, ATTN_D)),
        "lam_q1": nrm((L, ATTN_D), 0.1),
        "lam_k1": nrm((L, ATTN_D), 0.1),
        "lam_q2": nrm((L, ATTN_D), 0.1),
        "lam_k2": nrm((L, ATTN_D), 0.1),
        "g_attn_out": gain((L, ATTN_VD)),
        "w_pool": nrm((L, POOL_GROUPS, POOL_GW, POOL_GW), POOL_GW ** -0.5),
        "b_pool": nrm((L, POOL_GROUPS, POOL_GW), 0.01),
        "pool_scale": gain((L, GROUP_W)),
        "conv_w": nrm((L, CONV_K, GROUP_W), CONV_K ** -0.5),
        "conv_b": nrm((L, GROUP_W), 0.01),
        "conv_ln_g": gain((L, GROUP_W)),
        "conv_ln_b": nrm((L, GROUP_W), 0.01),
        "w_pw2": nrm((L, GROUP_W, GROUP_W), GROUP_W ** -0.5),
        "sgu_ln_g": gain((L, GROUP_W)),
        "sgu_ln_b": nrm((L, GROUP_W), 0.01),
        "w_spatial": nrm((L, SGU_GROUPS, SGU_CHUNK, SGU_CHUNK), SGU_CHUNK ** -0.5),
        "b_spatial": gain((L, SGU_GROUPS, SGU_CHUNK)),
        "w_router": nrm((L, D, N_EXPERTS), D ** -0.5),
        "w_gate": nrm((L, N_EXPERTS, D, D_EXPERT), D ** -0.5),
        "w_up": nrm((L, N_EXPERTS, D, D_EXPERT), D ** -0.5),
        "w_down": nrm((L, N_EXPERTS, D_EXPERT, D), D_EXPERT ** -0.5),
    }


def reference(x, c, ctx, c_ctx, w_mod, b_mod, g_norm1, g_norm2, w_in, w_out, g_q, g_k,
              lam_q1, lam_k1, lam_q2, lam_k2, g_attn_out, w_pool, b_pool, pool_scale,
              conv_w, conv_b, conv_ln_g, conv_ln_b, w_pw2, sgu_ln_g, sgu_ln_b,
              w_spatial, b_spatial, w_router, w_gate, w_up, w_down):
    B, T, _ = x.shape
    rope = axial_rope_tables(T, x.dtype)
    for l in range(DEPTH):
        last = l == DEPTH - 1
        lam_init = 0.8 - 0.6 * math.exp(-0.3 * l)
        lam = (jnp.exp(jnp.sum(lam_q1[l].astype(jnp.float32) * lam_k1[l].astype(jnp.float32)))
               - jnp.exp(jnp.sum(lam_q2[l].astype(jnp.float32) * lam_k2[l].astype(jnp.float32)))
               + lam_init)
        sh1, sc1, gt1, sh2, sc2, gt2 = modulation(c[:, None, :], w_mod[l], b_mod[l])
        csh1, csc1, cgt1, csh2, csc2, cgt2 = modulation(c_ctx[None, None, :], w_mod[l], b_mod[l])
        w_in_l = w_in[l]

        hx = rms_norm(x, g_norm1[l]) * (1.0 + sc1) + sh1
        hc = rms_norm(ctx, g_norm1[l]) * (1.0 + csc1) + csh1
        xq_in, xk_in, xv_in, xpool_in, xconv_in, xsgu_in = jnp.split(hx @ w_in_l, SPLITS, axis=-1)
        if last:
            ck_in, cv_in = jnp.split(hc @ w_in_l[:, SPLITS[0]:SPLITS[2]], 2, axis=-1)
        else:
            cq_in, ck_in, cv_in, cpool_in, cconv_in, csgu_in = jnp.split(hc @ w_in_l, SPLITS, axis=-1)

        ck1, ck2 = qk_heads(ck_in, g_k[l], None)
        cv = cv_in.reshape(B, -1, ATTN_HEADS, ATTN_VD)
        xq1, xq2 = qk_heads(xq_in, g_q[l], rope)
        xk1, xk2 = qk_heads(xk_in, g_k[l], rope)
        xv = xv_in.reshape(B, T, ATTN_HEADS, ATTN_VD)
        k1_all = jnp.concatenate([ck1, xk1], axis=1)
        k2_all = jnp.concatenate([ck2, xk2], axis=1)
        v_all = jnp.concatenate([cv, xv], axis=1)
        attn_x = attn_head_out(latent_diff_attention(xq1, xq2, k1_all, k2_all, v_all, lam),
                               g_attn_out[l], lam_init)
        mix_x = jnp.concatenate([
            attn_x,
            pool_mixer(xpool_in, w_pool[l], b_pool[l], pool_scale[l]),
            conv_module(xconv_in, conv_w[l], conv_b[l], conv_ln_g[l], conv_ln_b[l], w_pw2[l]),
            spatial_gating(xsgu_in, sgu_ln_g[l], sgu_ln_b[l], w_spatial[l], b_spatial[l]),
        ], axis=-1) @ w_out[l]
        x = x + gt1 * mix_x

        if not last:
            cq1, cq2 = qk_heads(cq_in, g_q[l], None)
            attn_c = attn_head_out(diff_softmax_attention(cq1, cq2, ck1, ck2, cv, lam),
                                   g_attn_out[l], lam_init)
            mix_c = jnp.concatenate([
                attn_c,
                pool_mixer(cpool_in, w_pool[l], b_pool[l], pool_scale[l]),
                conv_module(cconv_in, conv_w[l], conv_b[l], conv_ln_g[l], conv_ln_b[l], w_pw2[l]),
                spatial_gating(csgu_in, sgu_ln_g[l], sgu_ln_b[l], w_spatial[l], b_spatial[l]),
            ], axis=-1) @ w_out[l]
            ctx = ctx + cgt1 * mix_c
            hc2 = rms_norm(ctx, g_norm2[l]) * (1.0 + csc2) + csh2
            ctx = ctx + cgt2 * expert_choice_moe(hc2, w_router[l], w_gate[l], w_up[l], w_down[l])

        hx2 = rms_norm(x, g_norm2[l]) * (1.0 + sc2) + sh2
        x = x + gt2 * expert_choice_moe(hx2, w_router[l], w_gate[l], w_up[l], w_down[l])
    return x
```

```python
import functools
import math

import jax
import jax.numpy as jnp
from jax import lax
from jax.experimental import pallas as pl
from jax.experimental.pallas import tpu as pltpu

F32 = jnp.float32
BF16 = jnp.bfloat16

D_MODEL = 1024
GRID_W = 64
GROUP_W = 256
ATTN_HEADS = 4
ATTN_D = 32
ATTN_VD = 64
ROPE_THETA = 10000.0
POOL_WINDOWS = (2, 4, 8, 16)
POOL_GW = 64
CONV_K = 31
SGU_CHUNK = 128
SGU_GROUPS = 4
SGU_GW = 64
N_EXPERTS = 16
EC_CAPACITY = 2
D_EXPERT = 2048
N_MOD = 6
IN_W = 2048
EPS = 1e-6

MOD_ROWS = 16
LANES = 128
HALO = 16
TIME_CHUNK = 256
VMEM_LIMIT = 56 * 1024 * 1024


def _params(sem):
    return pltpu.CompilerParams(dimension_semantics=sem, vmem_limit_bytes=VMEM_LIMIT)


def _dot(a, b):
    return jnp.dot(a, b, preferred_element_type=F32)


def _dot_t(a, b):
    return lax.dot_general(a, b, (((1,), (1,)), ((), ())), preferred_element_type=F32)


def _split_dot(a, w):
    hi = a.astype(BF16)
    lo = (a - hi.astype(F32)).astype(BF16)
    return _dot(hi, w) + _dot(lo, w)


def _silu(x):
    return x * jax.nn.sigmoid(x)


def _layer_norm(x, g, b):
    mu = jnp.mean(x, axis=-1, keepdims=True)
    xc = x - mu
    var = jnp.mean(xc * xc, axis=-1, keepdims=True)
    return xc * lax.rsqrt(var + EPS) * g + b


def _mod_kernel(c_ref, w_ref, b_ref, o_ref):
    s = _silu(c_ref[...])
    o_ref[...] = _dot(s.astype(BF16), w_ref[...].astype(BF16)) + b_ref[...]


def _modulation(cond, w_mod, b_mod):
    L, D, N = w_mod.shape
    tn = 1024
    return pl.pallas_call(
        _mod_kernel,
        grid=(L, N // tn),
        in_specs=[
            pl.BlockSpec((MOD_ROWS, D), lambda l, j: (0, 0)),
            pl.BlockSpec((None, D, tn), lambda l, j: (l, 0, j)),
            pl.BlockSpec((None, 1, tn), lambda l, j: (l, 0, j)),
        ],
        out_specs=pl.BlockSpec((None, MOD_ROWS, tn), lambda l, j: (l, 0, j)),
        out_shape=jax.ShapeDtypeStruct((L, MOD_ROWS, N), F32),
        compiler_params=_params(("arbitrary", "arbitrary")),
        name="modulation",
    )(cond, w_mod, b_mod.reshape(L, 1, N))


def _inproj_kernel(x_ref, g_ref, sc_ref, sh_ref, w_ref, gq_ref, gk_ref, cos_ref, sin_ref, seg_ref,
                   q_ref, k_ref, v_ref, r_ref):
    x = x_ref[...]
    h = x * lax.rsqrt(jnp.mean(x * x, axis=-1, keepdims=True) + EPS) * g_ref[...]
    h = h * (1.0 + sc_ref[...]) + sh_ref[...]
    p = _dot(h.astype(BF16), w_ref[...])
    cos = cos_ref[...]
    sin = sin_ref[...]
    seg = seg_ref[...]
    lane = lax.broadcasted_iota(jnp.int32, cos.shape, 1)
    first = (lane & 15) < 8

    def prep(a, g, scale):
        ss = _split_dot(a * a, seg)
        n = a * lax.rsqrt(ss * (1.0 / ATTN_D) + EPS) * g
        partner = jnp.where(first, pltpu.roll(n, GROUP_W - 8, 1), pltpu.roll(n, 8, 1))
        return ((n * cos + partner * sin) * scale).astype(BF16)

    q_ref[...] = prep(p[:, 0:GROUP_W], gq_ref[...], ATTN_D ** -0.5)
    k_ref[...] = prep(p[:, GROUP_W:2 * GROUP_W], gk_ref[...], 1.0)
    v_ref[...] = p[:, 2 * GROUP_W:3 * GROUP_W].astype(BF16)
    r_ref[...] = p[:, 3 * GROUP_W:]


def _inproj(x, mod_row, g1, sc, sh, w_bf, gq_t, gk_t, cos_t, sin_t, seg32, tm):
    B, T, D = x.shape
    rest_w = IN_W - 3 * GROUP_W
    row = lambda b, i: (mod_row(b), 0, 0)
    const2 = lambda b, i: (0, 0)
    tok = lambda b, i: (b, i, 0)
    return pl.pallas_call(
        _inproj_kernel,
        grid=(B, T // tm),
        in_specs=[
            pl.BlockSpec((None, tm, D), tok),
            pl.BlockSpec((1, D), const2),
            pl.BlockSpec((None, 1, D), row),
            pl.BlockSpec((None, 1, D), row),
            pl.BlockSpec((D, IN_W), const2),
            pl.BlockSpec((1, GROUP_W), const2),
            pl.BlockSpec((1, GROUP_W), const2),
            pl.BlockSpec((tm, GROUP_W), lambda b, i: (i, 0)),
            pl.BlockSpec((tm, GROUP_W), lambda b, i: (i, 0)),
            pl.BlockSpec((GROUP_W, GROUP_W), const2),
        ],
        out_specs=[
            pl.BlockSpec((None, tm, GROUP_W), tok),
            pl.BlockSpec((None, tm, GROUP_W), tok),
            pl.BlockSpec((None, tm, GROUP_W), tok),
            pl.BlockSpec((None, tm, rest_w), tok),
        ],
        out_shape=[
            jax.ShapeDtypeStruct((B, T, GROUP_W), BF16),
            jax.ShapeDtypeStruct((B, T, GROUP_W), BF16),
            jax.ShapeDtypeStruct((B, T, GROUP_W), BF16),
            jax.ShapeDtypeStruct((B, T, rest_w), F32),
        ],
        compiler_params=_params(("arbitrary", "arbitrary")),
        name="inproj",
    )(x, g1, sc, sh, w_bf, gq_t, gk_t, cos_t, sin_t, seg32)


def _attn_kernel(*refs, n_seg, lam_init):
    q_ref = refs[0]
    kv_refs = refs[1:1 + 2 * n_seg]
    lq1_ref, lk1_ref, lq2_ref, lk2_ref, go_ref, seg_ref, o_ref, acc_ref = refs[1 + 2 * n_seg:]
    lam = (jnp.exp(jnp.sum(lq1_ref[...] * lk1_ref[...], axis=-1, keepdims=True))
           - jnp.exp(jnp.sum(lq2_ref[...] * lk2_ref[...], axis=-1, keepdims=True)) + lam_init)
    q = q_ref[...]
    tq = q.shape[0]
    lane_row = lax.broadcasted_iota(jnp.int32, (1, GROUP_W), 1)
    lane = lax.broadcasted_iota(jnp.int32, (tq, GROUP_W), 1)
    acc_ref[...] = jnp.zeros_like(acc_ref)

    def softmax_parts(qm):
        s = [_dot_t(qm, kv_refs[2 * i][...]) for i in range(n_seg)]
        mx = functools.reduce(jnp.maximum, [jnp.max(si, axis=-1, keepdims=True) for si in s])
        p = [jnp.exp(si - mx) for si in s]
        den = functools.reduce(jnp.add, [jnp.sum(pi, axis=-1, keepdims=True) for pi in p])
        return p, 1.0 / den

    def head(h, carry):
        m1 = ((lane_row >> 5) == 2 * h).astype(F32).astype(BF16)
        m2 = ((lane_row >> 5) == 2 * h + 1).astype(F32).astype(BF16)
        p1, r1 = softmax_parts(q * m1)
        p2, r2 = softmax_parts(q * m2)
        r2 = r2 * lam
        o = None
        for i in range(n_seg):
            w = (p1[i] * r1 - p2[i] * r2).astype(BF16)
            oi = _dot(w, kv_refs[2 * i + 1][...])
            o = oi if o is None else o + oi
        acc_ref[...] += jnp.where((lane >> 6) == h, o, 0.0)
        return carry

    lax.fori_loop(0, ATTN_HEADS, head, 0)
    o = acc_ref[...]
    ss = _split_dot(o * o, seg_ref[...])
    o_ref[...] = (o * lax.rsqrt(ss * (1.0 / ATTN_VD) + EPS) * go_ref[...] * (1.0 - lam_init)).astype(BF16)


def _attention(q, kvs, lams, go_t, seg64, lam_init, tq):
    B, T, _ = q.shape
    tok = lambda b, i: (b, i, 0)
    const2 = lambda b, i: (0, 0)
    in_specs = [pl.BlockSpec((None, tq, GROUP_W), tok)]
    args = [q]
    for k, v in kvs:
        n = k.shape[1]
        in_specs += [pl.BlockSpec((None, n, GROUP_W), lambda b, i: (b, 0, 0))] * 2
        args += [k, v]
    in_specs += [pl.BlockSpec((1, ATTN_D), const2)] * 4
    in_specs += [pl.BlockSpec((1, GROUP_W), const2), pl.BlockSpec((GROUP_W, GROUP_W), const2)]
    args += list(lams) + [go_t, seg64]
    return pl.pallas_call(
        functools.partial(_attn_kernel, n_seg=len(kvs), lam_init=lam_init),
        grid=(B, T // tq),
        in_specs=in_specs,
        out_specs=pl.BlockSpec((None, tq, GROUP_W), tok),
        out_shape=jax.ShapeDtypeStruct((B, T, GROUP_W), BF16),
        scratch_shapes=[pltpu.VMEM((tq, GROUP_W), F32)],
        compiler_params=_params(("arbitrary", "arbitrary")),
        name="diff_attention",
    )(*args)


def _gelu_tanh(x):
    return 0.5 * x * (1.0 + jnp.tanh(0.7978845608028654 * (x + 0.044715 * (x * x * x))))


def _mix_kernel(r_ref, wpool_ref, bpool_ref, pscale_ref, convw_ref, convb_ref, clng_ref, clnb_ref, wpw2_ref,
                slng_ref, slnb_ref, wsp_ref, bsp_ref, o_ref, zpad, ypad, *, T):
    n_chunk = T // TIME_CHUNK
    win_rows = TIME_CHUNK + 2 * HALO
    zeros_halo = jnp.zeros((HALO, GROUP_W), F32)
    zpad[0:HALO, :] = zeros_halo
    zpad[T + HALO:T + 2 * HALO, :] = zeros_halo
    ypad[0:HALO, :] = zeros_halo
    ypad[T + HALO:T + 2 * HALO, :] = zeros_halo

    def fill(i, carry):
        base = pl.multiple_of(i * TIME_CHUNK, TIME_CHUNK)
        rows = pl.ds(base, TIME_CHUNK)
        dst = pl.ds(base + HALO, TIME_CHUNK)
        zpad[dst, :] = r_ref[rows, 0:GROUP_W]
        val = r_ref[rows, GROUP_W:2 * GROUP_W]
        gate = r_ref[rows, 2 * GROUP_W:3 * GROUP_W]
        ypad[dst, :] = val * jax.nn.sigmoid(gate)
        return carry

    lax.fori_loop(0, n_chunk, fill, 0)

    def up(a, s):
        return pltpu.roll(a, s, 0)

    def down(a, s):
        return pltpu.roll(a, win_rows - s, 0)

    lane = lax.broadcasted_iota(jnp.int32, (TIME_CHUNK, GROUP_W), 1)
    pool_group = lane >> 6
    sgu_masks = [((lax.broadcasted_iota(jnp.int32, (1, GROUP_W), 1) >> 6) == g).astype(F32) for g in range(SGU_GROUPS)]

    def chunk(i, carry):
        base = pl.multiple_of(i * TIME_CHUNK, TIME_CHUNK)
        rows = pl.ds(base, TIME_CHUNK)
        t = base + lax.broadcasted_iota(jnp.int32, (TIME_CHUNK, 1), 0)

        win = zpad[pl.ds(base, win_rows), :]
        w2 = win + up(win, 1)
        w4 = up(w2, 1) + down(w2, 1)
        w8 = up(w4, 2) + down(w4, 2)
        w16 = up(w8, 4) + down(w8, 4)
        sums = (w2, w4, w8, w16)
        mean = None
        for g, w in enumerate(POOL_WINDOWS):
            cnt = (jnp.minimum(t + w // 2, T) - jnp.maximum(t - w // 2, 0)).astype(F32)
            mg = sums[g][HALO:HALO + TIME_CHUNK] / cnt
            mean = mg if mean is None else jnp.where(pool_group == g, mg, mean)
        d = mean - win[HALO:HALO + TIME_CHUNK]
        pool = (_dot(d.astype(BF16), wpool_ref[...]) + bpool_ref[...]) * pscale_ref[...]
        o_ref[rows, 0:GROUP_W] = pool.astype(BF16)

        ywin = ypad[pl.ds(base, win_rows), :]
        acc = jnp.zeros((TIME_CHUNK, GROUP_W), F32)
        for k in range(CONV_K):
            shift = k - CONV_K // 2 + HALO
            acc = acc + pltpu.roll(ywin, win_rows - shift, 0)[0:TIME_CHUNK] * convw_ref[k:k + 1, :]
        cn = _silu(_layer_norm(acc + convb_ref[...], clng_ref[...], clnb_ref[...]))
        o_ref[rows, GROUP_W:2 * GROUP_W] = _dot(cn.astype(BF16), wpw2_ref[...]).astype(BF16)

        gl = _gelu_tanh(r_ref[rows, 3 * GROUP_W:5 * GROUP_W])
        u = gl[:, 0:GROUP_W]
        vn = _layer_norm(gl[:, GROUP_W:2 * GROUP_W], slng_ref[...], slnb_ref[...])
        for j in range(TIME_CHUNK // SGU_CHUNK):
            vj = vn[j * SGU_CHUNK:(j + 1) * SGU_CHUNK]
            s = bsp_ref[...]
            for g in range(SGU_GROUPS):
                s = s + _dot(wsp_ref[g], (vj * sgu_masks[g]).astype(BF16))
            sub = pl.ds(base + j * SGU_CHUNK, SGU_CHUNK)
            o_ref[sub, 2 * GROUP_W:3 * GROUP_W] = (u[j * SGU_CHUNK:(j + 1) * SGU_CHUNK] * s).astype(BF16)
        return carry

    lax.fori_loop(0, n_chunk, chunk, 0)


def _mixers(rest, wpool_bd, bpool, pscale, convw, convb, clng, clnb, wpw2, slng, slnb, wsp, bsp):
    B, T, RW = rest.shape
    const2 = lambda b: (0, 0)
    vec = pl.BlockSpec((1, GROUP_W), const2)
    mat = pl.BlockSpec((GROUP_W, GROUP_W), const2)
    return pl.pallas_call(
        functools.partial(_mix_kernel, T=T),
        grid=(B,),
        in_specs=[
            pl.BlockSpec((None, T, RW), lambda b: (b, 0, 0)),
            mat, vec, vec,
            pl.BlockSpec((CONV_K, GROUP_W), const2), vec, vec, vec, mat,
            vec, vec,
            pl.BlockSpec((SGU_GROUPS, SGU_CHUNK, SGU_CHUNK), lambda b: (0, 0, 0)),
            pl.BlockSpec((SGU_CHUNK, GROUP_W), const2),
        ],
        out_specs=pl.BlockSpec((None, T, 3 * GROUP_W), lambda b: (b, 0, 0)),
        out_shape=jax.ShapeDtypeStruct((B, T, 3 * GROUP_W), BF16),
        scratch_shapes=[pltpu.VMEM((T + 2 * HALO, GROUP_W), F32), pltpu.VMEM((T + 2 * HALO, GROUP_W), F32)],
        compiler_params=_params(("arbitrary",)),
        name="mixers",
    )(rest, wpool_bd, bpool, pscale, convw, convb, clng, clnb, wpw2, slng, slnb, wsp, bsp)


def _outproj_kernel(x_ref, a_ref, m_ref, w_ref, gt_ref, g2_ref, sc_ref, sh_ref, wrh_ref, wrl_ref,
                    xo_ref, h_ref, aff_ref):
    mix = _dot(a_ref[...], w_ref[0:GROUP_W, :]) + _dot(m_ref[...], w_ref[GROUP_W:, :])
    xn = x_ref[...] + gt_ref[...] * mix
    xo_ref[...] = xn
    h = xn * lax.rsqrt(jnp.mean(xn * xn, axis=-1, keepdims=True) + EPS) * g2_ref[...]
    h = h * (1.0 + sc_ref[...]) + sh_ref[...]
    hi = h.astype(BF16)
    h_ref[...] = hi
    lo = (h - hi.astype(F32)).astype(BF16)
    logits = _dot(hi, wrh_ref[...]) + _dot(hi, wrl_ref[...]) + _dot(lo, wrh_ref[...])
    lt = jnp.transpose(logits)[0:N_EXPERTS, :]
    e = jnp.exp(lt - jnp.max(lt, axis=0, keepdims=True))
    aff_ref[...] = e / jnp.sum(e, axis=0, keepdims=True)


def _outproj(x, attn, mixr, mod_row, w_bf, gt, g2, sc, sh, wr_hi, wr_lo, tm):
    B, T, D = x.shape
    row = lambda b, i: (mod_row(b), 0, 0)
    const2 = lambda b, i: (0, 0)
    tok = lambda b, i: (b, i, 0)
    return pl.pallas_call(
        _outproj_kernel,
        grid=(B, T // tm),
        in_specs=[
            pl.BlockSpec((None, tm, D), tok),
            pl.BlockSpec((None, tm, GROUP_W), tok),
            pl.BlockSpec((None, tm, 3 * GROUP_W), tok),
            pl.BlockSpec((D, D), const2),
            pl.BlockSpec((None, 1, D), row),
            pl.BlockSpec((1, D), const2),
            pl.BlockSpec((None, 1, D), row),
            pl.BlockSpec((None, 1, D), row),
            pl.BlockSpec((D, LANES), const2),
            pl.BlockSpec((D, LANES), const2),
        ],
        out_specs=[
            pl.BlockSpec((None, tm, D), tok),
            pl.BlockSpec((None, tm, D), tok),
            pl.BlockSpec((None, N_EXPERTS, tm), lambda b, i: (b, 0, i)),
        ],
        out_shape=[
            jax.ShapeDtypeStruct((B, T, D), F32),
            jax.ShapeDtypeStruct((B, T, D), BF16),
            jax.ShapeDtypeStruct((B, N_EXPERTS, T), F32),
        ],
        compiler_params=_params(("arbitrary", "arbitrary")),
        name="outproj",
    )(x, attn, mixr, w_bf, gt, g2, sc, sh, wr_hi, wr_lo)


def _lane_prefix(x, tri):
    outs = []
    off = jnp.zeros((x.shape[0], 1), F32)
    for j in range(x.shape[1] // LANES):
        xb = x[:, j * LANES:(j + 1) * LANES]
        inc = _dot(xb.astype(BF16), tri)
        outs.append(inc - xb + off)
        off = off + inc[:, LANES - 1:LANES]
    return jnp.concatenate(outs, axis=1)


def _route_kernel(a_ref, tri_ref, pos_ref, gate_ref, ptok_ref, *, cap, n_expert):
    a = a_ref[...]
    R, T = a.shape

    def enough(c):
        return jnp.sum(jnp.where(a >= c, 1.0, 0.0), axis=-1, keepdims=True) >= cap

    def bit_search(i, lo):
        cand = lo | lax.shift_left(jnp.int32(1), 30 - i)
        return jnp.where(enough(pltpu.bitcast(cand, F32)), cand, lo)

    lo_bits = lax.fori_loop(0, 31, bit_search, jnp.zeros((R, 1), jnp.int32))

    def refine(i, c):
        lo, hi = c
        mid = lo + 0.5 * (hi - lo)
        ok = enough(mid)
        return jnp.where(ok, mid, lo), jnp.where(ok, hi, mid)

    thr, _ = lax.fori_loop(0, 30, refine, (pltpu.bitcast(lo_bits, F32), pltpu.bitcast(lo_bits + 1, F32)))
    gt = a > thr
    eq = a == thr
    need = cap - jnp.sum(jnp.where(gt, 1.0, 0.0), axis=-1, keepdims=True)
    tri = tri_ref[...]
    tie_rank = _lane_prefix(jnp.where(eq, 1.0, 0.0), tri)
    sel = jnp.logical_or(gt, jnp.logical_and(eq, tie_rank < need))
    pos = jnp.where(sel, _lane_prefix(jnp.where(sel, 1.0, 0.0), tri), -1.0)
    pos_ref[...] = pos
    gate_ref[...] = jnp.where(sel, a, 0.0)
    fill = jnp.full((LANES - n_expert, T), -1.0, F32)
    for b in range(R // n_expert):
        padded = jnp.concatenate([pos[b * n_expert:(b + 1) * n_expert], fill], axis=0)
        ptok_ref[b] = jnp.transpose(padded)


def _route(aff_t, tri, cap):
    B, E, T = aff_t.shape
    blk = pl.BlockSpec((B * E, T), lambda i: (0, 0))
    pos, gate, ptok = pl.pallas_call(
        functools.partial(_route_kernel, cap=cap, n_expert=E),
        grid=(1,),
        in_specs=[blk, pl.BlockSpec((LANES, LANES), lambda i: (0, 0))],
        out_specs=[blk, blk, pl.BlockSpec((B, T, LANES), lambda i: (0, 0, 0))],
        out_shape=[
            jax.ShapeDtypeStruct((B * E, T), F32),
            jax.ShapeDtypeStruct((B * E, T), F32),
            jax.ShapeDtypeStruct((B, T, LANES), F32),
        ],
        compiler_params=_params(("arbitrary",)),
        name="route",
    )(aff_t.reshape(B * E, T), tri)
    return pos.reshape(B, E, T), gate.reshape(B, E, T), ptok


def _gather_kernel(pos_ref, gate_ref, h_ref, xs_ref, gs_ref, *, cap):
    e = pl.program_id(1)
    prow = pos_ref[pl.ds(e, 1), :]
    grow = gate_ref[pl.ds(e, 1), :]
    T = prow.shape[1]
    slot = lax.broadcasted_iota(jnp.int32, (cap, T), 0).astype(F32)
    hit = slot == prow
    onehot = jnp.where(hit, 1.0, 0.0).astype(BF16)
    xs_ref[...] = _dot(onehot, h_ref[...]).astype(BF16)
    gs_ref[...] = jnp.sum(jnp.where(hit, grow, 0.0), axis=-1, keepdims=True)


def _gather(pos_t, gate_t, h):
    B, E, T = pos_t.shape
    D = h.shape[-1]
    cap = EC_CAPACITY * T // E
    et = pl.BlockSpec((None, E, T), lambda b, e: (b, 0, 0))
    return pl.pallas_call(
        functools.partial(_gather_kernel, cap=cap),
        grid=(B, E),
        in_specs=[et, et, pl.BlockSpec((None, T, D), lambda b, e: (b, 0, 0))],
        out_specs=[
            pl.BlockSpec((None, cap, D), lambda b, e: (e, b, 0)),
            pl.BlockSpec((None, cap, 1), lambda b, e: (e, b, 0)),
        ],
        out_shape=[
            jax.ShapeDtypeStruct((E, B * cap, D), BF16),
            jax.ShapeDtypeStruct((E, B * cap, 1), F32),
        ],
        compiler_params=_params(("arbitrary", "arbitrary")),
        name="gather",
    )(pos_t, gate_t, h)


def _moe_kernel(*refs, n_grp, tm):
    xs = refs[0:2 * n_grp:2]
    gs = refs[1:2 * n_grp:2]
    wg_ref, wu_ref, wd_ref = refs[2 * n_grp:2 * n_grp + 3]
    ys = refs[2 * n_grp + 3:3 * n_grp + 3]
    accs = refs[3 * n_grp + 3:4 * n_grp + 3]
    wgb, wub, wdb = refs[4 * n_grp + 3:]
    f = pl.program_id(1)
    nf = pl.num_programs(1)
    wgb[...] = wg_ref[...].astype(BF16)
    wub[...] = wu_ref[...].astype(BF16)
    wdb[...] = wd_ref[...].astype(BF16)

    for x_ref, g_ref, y_ref, acc in zip(xs, gs, ys, accs):
        rows = x_ref.shape[0]
        t = min(tm, rows)

        def tile(r, carry, x_ref=x_ref, acc=acc, t=t):
            sl = pl.ds(pl.multiple_of(r * t, t), t)
            xt = x_ref[sl, :]
            a = _dot(xt, wgb[...])
            u = _dot(xt, wub[...])
            contrib = _dot((_silu(a) * u).astype(BF16), wdb[...])

            @pl.when(f == 0)
            def _():
                acc[sl, :] = contrib

            @pl.when(f > 0)
            def _():
                acc[sl, :] += contrib
            return carry

        lax.fori_loop(0, rows // t, tile, 0)

        @pl.when(f == nf - 1)
        def _(y_ref=y_ref, acc=acc, g_ref=g_ref):
            y_ref[...] = (acc[...] * g_ref[...]).astype(BF16)


def _moe(groups, layer, w_gate, w_up, w_down, tf=512, tm=512):
    _, E, D, F = w_gate.shape
    in_specs, args, out_specs, out_shape, scratch = [], [], [], [], []
    for xs, gs in groups:
        R = xs.shape[1]
        in_specs += [pl.BlockSpec((None, R, D), lambda e, f: (e, 0, 0)), pl.BlockSpec((None, R, 1), lambda e, f: (e, 0, 0))]
        args += [xs, gs]
        out_specs.append(pl.BlockSpec((None, R, D), lambda e, f: (e, 0, 0)))
        out_shape.append(jax.ShapeDtypeStruct((E, R, D), BF16))
        scratch.append(pltpu.VMEM((R, D), F32))
    in_specs += [
        pl.BlockSpec((None, None, D, tf), lambda e, f: (layer, e, 0, f)),
        pl.BlockSpec((None, None, D, tf), lambda e, f: (layer, e, 0, f)),
        pl.BlockSpec((None, None, tf, D), lambda e, f: (layer, e, f, 0)),
    ]
    args += [w_gate, w_up, w_down]
    scratch += [pltpu.VMEM((D, tf), BF16), pltpu.VMEM((D, tf), BF16), pltpu.VMEM((tf, D), BF16)]
    return pl.pallas_call(
        functools.partial(_moe_kernel, n_grp=len(groups), tm=tm),
        grid=(E, F // tf),
        in_specs=in_specs,
        out_specs=out_specs,
        out_shape=out_shape,
        scratch_shapes=scratch,
        compiler_params=_params(("arbitrary", "arbitrary")),
        name="expert_ffn",
    )(*args)


def _combine_kernel(x_ref, gt_ref, ptok_ref, ys_ref, o_ref, *, tm):
    T, D = x_ref.shape
    E, cap, _ = ys_ref.shape
    t = min(tm, T)
    lane = lax.broadcasted_iota(jnp.int32, (t, cap), 1).astype(F32)

    def tile(r, carry):
        sl = pl.ds(pl.multiple_of(r * t, t), t)
        pt = ptok_ref[sl, :]
        acc = jnp.zeros((t, D), F32)
        for e in range(E):
            onehot = jnp.where(pt[:, e:e + 1] == lane, 1.0, 0.0).astype(BF16)
            acc = acc + _dot(onehot, ys_ref[e])
        o_ref[sl, :] = x_ref[sl, :] + gt_ref[...] * acc
        return carry

    lax.fori_loop(0, T // t, tile, 0)


def _combine(x, gt, mod_row, ptok, ys, tm=256):
    B, T, D = x.shape
    E = ys.shape[0]
    cap = ys.shape[1] // B
    ys4 = ys.reshape(E, B, cap, D)
    return pl.pallas_call(
        functools.partial(_combine_kernel, tm=tm),
        grid=(B,),
        in_specs=[
            pl.BlockSpec((None, T, D), lambda b: (b, 0, 0)),
            pl.BlockSpec((None, 1, D), lambda b: (mod_row(b), 0, 0)),
            pl.BlockSpec((None, T, LANES), lambda b: (b, 0, 0)),
            pl.BlockSpec((E, None, cap, D), lambda b: (0, b, 0, 0)),
        ],
        out_specs=pl.BlockSpec((None, T, D), lambda b: (b, 0, 0)),
        out_shape=jax.ShapeDtypeStruct((B, T, D), F32),
        compiler_params=_params(("arbitrary",)),
        name="combine",
    )(x, gt, ptok, ys4)


def _rope_tables(T):
    rows = (jnp.arange(T) // GRID_W).astype(F32)
    cols = (jnp.arange(T) % GRID_W).astype(F32)
    n_freq = ATTN_D // 4
    inv = ROPE_THETA ** (-jnp.arange(n_freq, dtype=F32) / n_freq)
    d = jnp.arange(GROUP_W) % ATTN_D
    freq = d % n_freq
    use_row = d < ATTN_D // 2
    ang = jnp.where(use_row[None, :], rows[:, None] * inv[freq][None, :], cols[:, None] * inv[freq][None, :])
    sign = jnp.where((d % (2 * n_freq)) < n_freq, -1.0, 1.0).astype(F32)
    return jnp.cos(ang).astype(F32), (jnp.sin(ang) * sign[None, :]).astype(F32)


def _segment_ones(width):
    i = jnp.arange(GROUP_W) // width
    return (i[:, None] == i[None, :]).astype(BF16)


def _block_diag(w):
    G, n, _ = w.shape
    eye = jnp.eye(G, dtype=w.dtype)
    return (eye[:, None, :, None] * w[:, :, None, :]).reshape(G * n, G * n)


def kernel(x, c, ctx, c_ctx, w_mod, b_mod, g_norm1, g_norm2, w_in, w_out, g_q, g_k, lam_q1, lam_k1, lam_q2, lam_k2,
           g_attn_out, w_pool, b_pool, pool_scale, conv_w, conv_b, conv_ln_g, conv_ln_b, w_pw2, sgu_ln_g, sgu_ln_b,
           w_spatial, b_spatial, w_router, w_gate, w_up, w_down):
    B, T, D = x.shape
    C = ctx.shape[1]
    L = w_mod.shape[0]
    assert B < MOD_ROWS and D == D_MODEL and T % TIME_CHUNK == 0 and C % TIME_CHUNK == 0

    cond = jnp.zeros((MOD_ROWS, D), F32).at[:B].set(c).at[B].set(c_ctx)
    mods = _modulation(cond, w_mod, b_mod).reshape(L, MOD_ROWS, N_MOD, 1, D)
    lat_row = lambda b: b
    ctx_row = lambda b: B

    cos_x, sin_x = _rope_tables(T)
    cos_c, sin_c = jnp.ones((C, GROUP_W), F32), jnp.zeros((C, GROUP_W), F32)
    seg32 = _segment_ones(ATTN_D)
    seg64 = _segment_ones(ATTN_VD)
    tri = (jnp.arange(LANES)[:, None] <= jnp.arange(LANES)[None, :]).astype(BF16)
    tile_g = lambda g: jnp.tile(g, GROUP_W // g.shape[0]).reshape(1, GROUP_W)
    vec = lambda v: v.reshape(1, -1)

    for l in range(L):
        last = l == L - 1
        lam_init = 0.8 - 0.6 * math.exp(-0.3 * l)
        sh1, sc1, gt1, sh2, sc2, gt2 = (mods[l, :, i] for i in range(N_MOD))
        w_in_bf = w_in[l].astype(BF16)
        w_out_bf = w_out[l].astype(BF16)
        wr = jnp.zeros((D, LANES), F32).at[:, :N_EXPERTS].set(w_router[l])
        wr_hi = wr.astype(BF16)
        wr_lo = (wr - wr_hi.astype(F32)).astype(BF16)
        lams = [vec(p[l]) for p in (lam_q1, lam_k1, lam_q2, lam_k2)]
        gq_t, gk_t, go_t = tile_g(g_q[l]), tile_g(g_k[l]), tile_g(g_attn_out[l])
        mix_w = (_block_diag(w_pool[l]).astype(BF16), vec(b_pool[l]), vec(pool_scale[l]), conv_w[l], vec(conv_b[l]),
                 vec(conv_ln_g[l]), vec(conv_ln_b[l]), w_pw2[l].astype(BF16), vec(sgu_ln_g[l]), vec(sgu_ln_b[l]),
                 w_spatial[l].astype(BF16), jnp.repeat(b_spatial[l].T, SGU_GW, axis=1))
        g1, g2 = vec(g_norm1[l]), vec(g_norm2[l])

        qx, kx, vx, rx = _inproj(x, lat_row, g1, sc1, sh1, w_in_bf, gq_t, gk_t, cos_x, sin_x, seg32, tm=512)
        qc, kc, vc, rc = _inproj(ctx, ctx_row, g1, sc1, sh1, w_in_bf, gq_t, gk_t, cos_c, sin_c, seg32, tm=256)

        attn_x = _attention(qx, [(kc, vc), (kx, vx)], lams, go_t, seg64, lam_init, tq=256)
        x, h2x, aff_x = _outproj(x, attn_x, _mixers(rx, *mix_w), lat_row, w_out_bf, gt1, g2, sc2, sh2, wr_hi, wr_lo, tm=512)
        pos_x, gate_x, ptok_x = _route(aff_x, tri, EC_CAPACITY * T // N_EXPERTS)
        groups = [_gather(pos_x, gate_x, h2x)]

        if not last:
            attn_c = _attention(qc, [(kc, vc)], lams, go_t, seg64, lam_init, tq=256)
            ctx, h2c, aff_c = _outproj(ctx, attn_c, _mixers(rc, *mix_w), ctx_row, w_out_bf, gt1, g2, sc2, sh2,
                                       wr_hi, wr_lo, tm=256)
            pos_c, gate_c, ptok_c = _route(aff_c, tri, EC_CAPACITY * C // N_EXPERTS)
            groups.append(_gather(pos_c, gate_c, h2c))

        ys = _moe(groups, l, w_gate, w_up, w_down)
        x = _combine(x, gt2, lat_row, ptok_x, ys[0])
        if not last:
            ctx = _combine(ctx, gt2, ctx_row, ptok_c, ys[1])
    return x
```

```python
import functools
import math

import jax
import jax.numpy as jnp
from jax import lax
from jax.experimental import pallas as pl
from jax.experimental.pallas import tpu as pltpu

F32 = jnp.float32
BF16 = jnp.bfloat16

D_MODEL = 1024
GRID_W = 64
GROUP_W = 256
ATTN_HEADS = 4
ATTN_D = 32
ATTN_VD = 64
ROPE_THETA = 10000.0
POOL_WINDOWS = (2, 4, 8, 16)
POOL_GW = 64
CONV_K = 31
SGU_CHUNK = 128
SGU_GROUPS = 4
SGU_GW = 64
N_EXPERTS = 16
EC_CAPACITY = 2
D_EXPERT = 2048
N_MOD = 6
IN_W = 2048
EPS = 1e-6
LOG2E = 1.4426950408889634

MOD_ROWS = 16
LANES = 128
HALO = 16
TIME_CHUNK = 256
VMEM_LIMIT = 56 * 1024 * 1024


def _params(sem):
    return pltpu.CompilerParams(dimension_semantics=sem, vmem_limit_bytes=VMEM_LIMIT)


def _dot(a, b):
    return jnp.dot(a, b, preferred_element_type=F32)


def _dot_t(a, b):
    return lax.dot_general(a, b, (((1,), (1,)), ((), ())), preferred_element_type=F32)


def _split_dot(a, w):
    hi = a.astype(BF16)
    lo = (a - hi.astype(F32)).astype(BF16)
    return _dot(hi, w) + _dot(lo, w)


def _silu(x):
    return x * jax.nn.sigmoid(x)


def _layer_norm(x, g, b):
    mu = jnp.mean(x, axis=-1, keepdims=True)
    xc = x - mu
    var = jnp.mean(xc * xc, axis=-1, keepdims=True)
    return xc * lax.rsqrt(var + EPS) * g + b


def _mod_kernel(c_ref, w_ref, b_ref, o_ref):
    s = _silu(c_ref[...])
    o_ref[...] = _dot(s.astype(BF16), w_ref[...].astype(BF16)) + b_ref[...]


def _modulation(cond, w_mod, b_mod):
    L, D, N = w_mod.shape
    tn = 1024
    return pl.pallas_call(
        _mod_kernel,
        grid=(L, N // tn),
        in_specs=[
            pl.BlockSpec((MOD_ROWS, D), lambda l, j: (0, 0)),
            pl.BlockSpec((None, D, tn), lambda l, j: (l, 0, j)),
            pl.BlockSpec((None, 1, tn), lambda l, j: (l, 0, j)),
        ],
        out_specs=pl.BlockSpec((None, MOD_ROWS, tn), lambda l, j: (l, 0, j)),
        out_shape=jax.ShapeDtypeStruct((L, MOD_ROWS, N), F32),
        compiler_params=_params(("arbitrary", "arbitrary")),
        name="modulation",
    )(cond, w_mod, b_mod.reshape(L, 1, N))


def _inproj_kernel(x_ref, g_ref, sc_ref, sh_ref, w_ref, gq_ref, gk_ref, cos_ref, sin_ref, seg_ref,
                   q_ref, k_ref, v_ref, r_ref):
    x = x_ref[...]
    h = x * lax.rsqrt(jnp.mean(x * x, axis=-1, keepdims=True) + EPS) * g_ref[...]
    h = h * (1.0 + sc_ref[...]) + sh_ref[...]
    p = _dot(h.astype(BF16), w_ref[...])
    cos = cos_ref[...]
    sin = sin_ref[...]
    seg = seg_ref[...]
    lane = lax.broadcasted_iota(jnp.int32, cos.shape, 1)
    first = (lane & 15) < 8

    def prep(a, g):
        ss = _split_dot(a * a, seg)
        n = a * lax.rsqrt(ss * (1.0 / ATTN_D) + EPS) * g
        partner = jnp.where(first, pltpu.roll(n, GROUP_W - 8, 1), pltpu.roll(n, 8, 1))
        return n * cos + partner * sin

    q_ref[...] = (prep(p[:, 0:GROUP_W], gq_ref[...]) * (ATTN_D ** -0.5 * LOG2E)).astype(BF16)
    k_ref[...] = jnp.transpose(prep(p[:, GROUP_W:2 * GROUP_W], gk_ref[...])).astype(BF16)
    v_ref[...] = p[:, 2 * GROUP_W:3 * GROUP_W].astype(BF16)
    r_ref[...] = p[:, 3 * GROUP_W:]


def _inproj(x, mod_row, g1, sc, sh, w_bf, gq_t, gk_t, cos_t, sin_t, seg32, tm):
    B, T, D = x.shape
    rest_w = IN_W - 3 * GROUP_W
    row = lambda b, i: (mod_row(b), 0, 0)
    const2 = lambda b, i: (0, 0)
    tok = lambda b, i: (b, i, 0)
    return pl.pallas_call(
        _inproj_kernel,
        grid=(B, T // tm),
        in_specs=[
            pl.BlockSpec((None, tm, D), tok),
            pl.BlockSpec((1, D), const2),
            pl.BlockSpec((None, 1, D), row),
            pl.BlockSpec((None, 1, D), row),
            pl.BlockSpec((D, IN_W), const2),
            pl.BlockSpec((1, GROUP_W), const2),
            pl.BlockSpec((1, GROUP_W), const2),
            pl.BlockSpec((tm, GROUP_W), lambda b, i: (i, 0)),
            pl.BlockSpec((tm, GROUP_W), lambda b, i: (i, 0)),
            pl.BlockSpec((GROUP_W, GROUP_W), const2),
        ],
        out_specs=[
            pl.BlockSpec((None, tm, GROUP_W), tok),
            pl.BlockSpec((None, GROUP_W, tm), lambda b, i: (b, 0, i)),
            pl.BlockSpec((None, tm, GROUP_W), tok),
            pl.BlockSpec((None, tm, rest_w), tok),
        ],
        out_shape=[
            jax.ShapeDtypeStruct((B, T, GROUP_W), BF16),
            jax.ShapeDtypeStruct((B, GROUP_W, T), BF16),
            jax.ShapeDtypeStruct((B, T, GROUP_W), BF16),
            jax.ShapeDtypeStruct((B, T, rest_w), F32),
        ],
        compiler_params=_params(("arbitrary", "arbitrary")),
        name="inproj",
    )(x, g1, sc, sh, w_bf, gq_t, gk_t, cos_t, sin_t, seg32)


def _attn_kernel(*refs, n_seg, lam_init):
    q_ref = refs[0]
    kv_refs = refs[1:1 + 2 * n_seg]
    lq1_ref, lk1_ref, lq2_ref, lk2_ref, go_ref, o_ref = refs[1 + 2 * n_seg:]
    lam = (jnp.exp(jnp.sum(lq1_ref[...] * lk1_ref[...], axis=-1, keepdims=True))
           - jnp.exp(jnp.sum(lq2_ref[...] * lk2_ref[...], axis=-1, keepdims=True)) + lam_init)
    q = q_ref[...]

    def softmax_parts(off):
        qs = q[:, off:off + ATTN_D]
        s = [_dot(qs, kv_refs[2 * i][off:off + ATTN_D, :]) for i in range(n_seg)]
        mx = functools.reduce(jnp.maximum, [jnp.max(si, axis=-1, keepdims=True) for si in s])
        p = [jnp.exp2(si - mx) for si in s]
        den = functools.reduce(jnp.add, [jnp.sum(pi, axis=-1, keepdims=True) for pi in p])
        return p, 1.0 / den

    heads = []
    for h in range(ATTN_HEADS):
        p1, r1 = softmax_parts(h * ATTN_VD)
        p2, r2 = softmax_parts(h * ATTN_VD + ATTN_D)
        r2 = r2 * lam
        o = None
        for i in range(n_seg):
            w = (p1[i] * r1 - p2[i] * r2).astype(BF16)
            oi = _dot(w, kv_refs[2 * i + 1][:, h * ATTN_VD:(h + 1) * ATTN_VD])
            o = oi if o is None else o + oi
        ms = jnp.mean(o * o, axis=-1, keepdims=True)
        heads.append(o * lax.rsqrt(ms + EPS))
    o_ref[...] = (jnp.concatenate(heads, axis=1) * go_ref[...] * (1.0 - lam_init)).astype(BF16)


def _attention(q, kvs, lams, go_t, lam_init, tq):
    B, T, _ = q.shape
    tok = lambda b, i: (b, i, 0)
    const2 = lambda b, i: (0, 0)
    in_specs = [pl.BlockSpec((None, tq, GROUP_W), tok)]
    args = [q]
    for kt, v in kvs:
        n = v.shape[1]
        in_specs += [pl.BlockSpec((None, GROUP_W, n), lambda b, i: (b, 0, 0)),
                     pl.BlockSpec((None, n, GROUP_W), lambda b, i: (b, 0, 0))]
        args += [kt, v]
    in_specs += [pl.BlockSpec((1, ATTN_D), const2)] * 4
    in_specs += [pl.BlockSpec((1, GROUP_W), const2)]
    args += list(lams) + [go_t]
    return pl.pallas_call(
        functools.partial(_attn_kernel, n_seg=len(kvs), lam_init=lam_init),
        grid=(B, T // tq),
        in_specs=in_specs,
        out_specs=pl.BlockSpec((None, tq, GROUP_W), tok),
        out_shape=jax.ShapeDtypeStruct((B, T, GROUP_W), BF16),
        compiler_params=_params(("arbitrary", "arbitrary")),
        name="diff_attention",
    )(*args)


def _gelu_tanh(x):
    return 0.5 * x * (1.0 + jnp.tanh(0.7978845608028654 * (x + 0.044715 * (x * x * x))))


def _mix_kernel(r_ref, wpool_ref, bpool_ref, pscale_ref, convw_ref, convb_ref, clng_ref, clnb_ref, wpw2_ref,
                slng_ref, slnb_ref, wsp_ref, bsp_ref, o_ref, zpad, ypad, *, T):
    n_chunk = T // TIME_CHUNK
    win_rows = TIME_CHUNK + 2 * HALO
    zeros_halo = jnp.zeros((HALO, GROUP_W), F32)
    zpad[0:HALO, :] = zeros_halo
    zpad[T + HALO:T + 2 * HALO, :] = zeros_halo
    ypad[0:HALO, :] = zeros_halo
    ypad[T + HALO:T + 2 * HALO, :] = zeros_halo

    def fill(i, carry):
        base = pl.multiple_of(i * TIME_CHUNK, TIME_CHUNK)
        rows = pl.ds(base, TIME_CHUNK)
        dst = pl.ds(base + HALO, TIME_CHUNK)
        zpad[dst, :] = r_ref[rows, 0:GROUP_W]
        val = r_ref[rows, GROUP_W:2 * GROUP_W]
        gate = r_ref[rows, 2 * GROUP_W:3 * GROUP_W]
        ypad[dst, :] = val * jax.nn.sigmoid(gate)
        return carry

    lax.fori_loop(0, n_chunk, fill, 0)

    def up(a, s):
        return pltpu.roll(a, s, 0)

    def down(a, s):
        return pltpu.roll(a, win_rows - s, 0)

    lane = lax.broadcasted_iota(jnp.int32, (TIME_CHUNK, GROUP_W), 1)
    pool_group = lane >> 6
    sgu_masks = [((lax.broadcasted_iota(jnp.int32, (1, GROUP_W), 1) >> 6) == g).astype(F32) for g in range(SGU_GROUPS)]

    def chunk(i, carry):
        base = pl.multiple_of(i * TIME_CHUNK, TIME_CHUNK)
        rows = pl.ds(base, TIME_CHUNK)
        t = base + lax.broadcasted_iota(jnp.int32, (TIME_CHUNK, 1), 0)

        win = zpad[pl.ds(base, win_rows), :]
        w2 = win + up(win, 1)
        w4 = up(w2, 1) + down(w2, 1)
        w8 = up(w4, 2) + down(w4, 2)
        w16 = up(w8, 4) + down(w8, 4)
        sums = (w2, w4, w8, w16)
        mean = None
        for g, w in enumerate(POOL_WINDOWS):
            cnt = (jnp.minimum(t + w // 2, T) - jnp.maximum(t - w // 2, 0)).astype(F32)
            mg = sums[g][HALO:HALO + TIME_CHUNK] / cnt
            mean = mg if mean is None else jnp.where(pool_group == g, mg, mean)
        d = mean - win[HALO:HALO + TIME_CHUNK]
        pool = (_dot(d.astype(BF16), wpool_ref[...]) + bpool_ref[...]) * pscale_ref[...]
        o_ref[rows, 0:GROUP_W] = pool.astype(BF16)

        ywin = ypad[pl.ds(base, win_rows), :]
        acc = jnp.zeros((TIME_CHUNK, GROUP_W), F32)
        for k in range(CONV_K):
            shift = k - CONV_K // 2 + HALO
            acc = acc + pltpu.roll(ywin, win_rows - shift, 0)[0:TIME_CHUNK] * convw_ref[k:k + 1, :]
        cn = _silu(_layer_norm(acc + convb_ref[...], clng_ref[...], clnb_ref[...]))
        o_ref[rows, GROUP_W:2 * GROUP_W] = _dot(cn.astype(BF16), wpw2_ref[...]).astype(BF16)

        gl = _gelu_tanh(r_ref[rows, 3 * GROUP_W:5 * GROUP_W])
        u = gl[:, 0:GROUP_W]
        vn = _layer_norm(gl[:, GROUP_W:2 * GROUP_W], slng_ref[...], slnb_ref[...])
        for j in range(TIME_CHUNK // SGU_CHUNK):
            vj = vn[j * SGU_CHUNK:(j + 1) * SGU_CHUNK]
            s = bsp_ref[...]
            for g in range(SGU_GROUPS):
                s = s + _dot(wsp_ref[g], (vj * sgu_masks[g]).astype(BF16))
            sub = pl.ds(base + j * SGU_CHUNK, SGU_CHUNK)
            o_ref[sub, 2 * GROUP_W:3 * GROUP_W] = (u[j * SGU_CHUNK:(j + 1) * SGU_CHUNK] * s).astype(BF16)
        return carry

    lax.fori_loop(0, n_chunk, chunk, 0)


def _mixers(rest, wpool_bd, bpool, pscale, convw, convb, clng, clnb, wpw2, slng, slnb, wsp, bsp):
    B, T, RW = rest.shape
    const2 = lambda b: (0, 0)
    vec = pl.BlockSpec((1, GROUP_W), const2)
    mat = pl.BlockSpec((GROUP_W, GROUP_W), const2)
    return pl.pallas_call(
        functools.partial(_mix_kernel, T=T),
        grid=(B,),
        in_specs=[
            pl.BlockSpec((None, T, RW), lambda b: (b, 0, 0)),
            mat, vec, vec,
            pl.BlockSpec((CONV_K, GROUP_W), const2), vec, vec, vec, mat,
            vec, vec,
            pl.BlockSpec((SGU_GROUPS, SGU_CHUNK, SGU_CHUNK), lambda b: (0, 0, 0)),
            pl.BlockSpec((SGU_CHUNK, GROUP_W), const2),
        ],
        out_specs=pl.BlockSpec((None, T, 3 * GROUP_W), lambda b: (b, 0, 0)),
        out_shape=jax.ShapeDtypeStruct((B, T, 3 * GROUP_W), BF16),
        scratch_shapes=[pltpu.VMEM((T + 2 * HALO, GROUP_W), F32), pltpu.VMEM((T + 2 * HALO, GROUP_W), F32)],
        compiler_params=_params(("arbitrary",)),
        name="mixers",
    )(rest, wpool_bd, bpool, pscale, convw, convb, clng, clnb, wpw2, slng, slnb, wsp, bsp)


def _outproj_kernel(x_ref, a_ref, m_ref, w_ref, gt_ref, g2_ref, sc_ref, sh_ref, wrh_ref, wrl_ref,
                    xo_ref, h_ref, aff_ref):
    mix = _dot(a_ref[...], w_ref[0:GROUP_W, :]) + _dot(m_ref[...], w_ref[GROUP_W:, :])
    xn = x_ref[...] + gt_ref[...] * mix
    xo_ref[...] = xn
    h = xn * lax.rsqrt(jnp.mean(xn * xn, axis=-1, keepdims=True) + EPS) * g2_ref[...]
    h = h * (1.0 + sc_ref[...]) + sh_ref[...]
    hi = h.astype(BF16)
    h_ref[...] = hi
    lo = (h - hi.astype(F32)).astype(BF16)
    logits = _dot(hi, wrh_ref[...]) + _dot(hi, wrl_ref[...]) + _dot(lo, wrh_ref[...])
    lt = jnp.transpose(logits)[0:N_EXPERTS, :]
    e = jnp.exp(lt - jnp.max(lt, axis=0, keepdims=True))
    aff_ref[...] = e / jnp.sum(e, axis=0, keepdims=True)


def _outproj(x, attn, mixr, mod_row, w_bf, gt, g2, sc, sh, wr_hi, wr_lo, tm):
    B, T, D = x.shape
    row = lambda b, i: (mod_row(b), 0, 0)
    const2 = lambda b, i: (0, 0)
    tok = lambda b, i: (b, i, 0)
    return pl.pallas_call(
        _outproj_kernel,
        grid=(B, T // tm),
        in_specs=[
            pl.BlockSpec((None, tm, D), tok),
            pl.BlockSpec((None, tm, GROUP_W), tok),
            pl.BlockSpec((None, tm, 3 * GROUP_W), tok),
            pl.BlockSpec((D, D), const2),
            pl.BlockSpec((None, 1, D), row),
            pl.BlockSpec((1, D), const2),
            pl.BlockSpec((None, 1, D), row),
            pl.BlockSpec((None, 1, D), row),
            pl.BlockSpec((D, LANES), const2),
            pl.BlockSpec((D, LANES), const2),
        ],
        out_specs=[
            pl.BlockSpec((None, tm, D), tok),
            pl.BlockSpec((None, tm, D), tok),
            pl.BlockSpec((None, N_EXPERTS, tm), lambda b, i: (b, 0, i)),
        ],
        out_shape=[
            jax.ShapeDtypeStruct((B, T, D), F32),
            jax.ShapeDtypeStruct((B, T, D), BF16),
            jax.ShapeDtypeStruct((B, N_EXPERTS, T), F32),
        ],
        compiler_params=_params(("arbitrary", "arbitrary")),
        name="outproj",
    )(x, attn, mixr, w_bf, gt, g2, sc, sh, wr_hi, wr_lo)


def _lane_prefix(x, tri):
    outs = []
    off = jnp.zeros((x.shape[0], 1), F32)
    for j in range(x.shape[1] // LANES):
        xb = x[:, j * LANES:(j + 1) * LANES]
        inc = _dot(xb.astype(BF16), tri)
        outs.append(inc - xb + off)
        off = off + inc[:, LANES - 1:LANES]
    return jnp.concatenate(outs, axis=1)


def _route_kernel(a_ref, tri_ref, pos_ref, gate_ref, ptok_ref, *, cap, n_expert):
    a = a_ref[...]
    R, T = a.shape

    def enough(c):
        return jnp.sum(jnp.where(a >= c, 1.0, 0.0), axis=-1, keepdims=True) >= cap

    def bit_search(i, lo):
        cand = lo | lax.shift_left(jnp.int32(1), 30 - i)
        return jnp.where(enough(pltpu.bitcast(cand, F32)), cand, lo)

    lo_bits = lax.fori_loop(0, 31, bit_search, jnp.zeros((R, 1), jnp.int32))

    def refine(i, c):
        lo, hi = c
        mid = lo + 0.5 * (hi - lo)
        ok = enough(mid)
        return jnp.where(ok, mid, lo), jnp.where(ok, hi, mid)

    thr, _ = lax.fori_loop(0, 30, refine, (pltpu.bitcast(lo_bits, F32), pltpu.bitcast(lo_bits + 1, F32)))
    gt = a > thr
    eq = a == thr
    need = cap - jnp.sum(jnp.where(gt, 1.0, 0.0), axis=-1, keepdims=True)
    tri = tri_ref[...]
    tie_rank = _lane_prefix(jnp.where(eq, 1.0, 0.0), tri)
    sel = jnp.logical_or(gt, jnp.logical_and(eq, tie_rank < need))
    pos = jnp.where(sel, _lane_prefix(jnp.where(sel, 1.0, 0.0), tri), -1.0)
    pos_ref[...] = pos
    gate_ref[...] = jnp.where(sel, a, 0.0)
    fill = jnp.full((LANES - n_expert, T), -1.0, F32)
    for b in range(R // n_expert):
        padded = jnp.concatenate([pos[b * n_expert:(b + 1) * n_expert], fill], axis=0)
        ptok_ref[b] = jnp.transpose(padded)


def _route(aff_t, tri, cap):
    B, E, T = aff_t.shape
    blk = pl.BlockSpec((B * E, T), lambda i: (0, 0))
    pos, gate, ptok = pl.pallas_call(
        functools.partial(_route_kernel, cap=cap, n_expert=E),
        grid=(1,),
        in_specs=[blk, pl.BlockSpec((LANES, LANES), lambda i: (0, 0))],
        out_specs=[blk, blk, pl.BlockSpec((B, T, LANES), lambda i: (0, 0, 0))],
        out_shape=[
            jax.ShapeDtypeStruct((B * E, T), F32),
            jax.ShapeDtypeStruct((B * E, T), F32),
            jax.ShapeDtypeStruct((B, T, LANES), F32),
        ],
        compiler_params=_params(("arbitrary",)),
        name="route",
    )(aff_t.reshape(B * E, T), tri)
    return pos.reshape(B, E, T), gate.reshape(B, E, T), ptok


def _gather_kernel(pos_ref, gate_ref, h_ref, xs_ref, gs_ref, *, cap):
    e = pl.program_id(1)
    prow = pos_ref[pl.ds(e, 1), :]
    grow = gate_ref[pl.ds(e, 1), :]
    T = prow.shape[1]
    slot = lax.broadcasted_iota(jnp.int32, (cap, T), 0).astype(F32)
    hit = slot == prow
    onehot = jnp.where(hit, 1.0, 0.0).astype(BF16)
    xs_ref[...] = _dot(onehot, h_ref[...]).astype(BF16)
    gs_ref[...] = jnp.sum(jnp.where(hit, grow, 0.0), axis=-1, keepdims=True)


def _gather(pos_t, gate_t, h):
    B, E, T = pos_t.shape
    D = h.shape[-1]
    cap = EC_CAPACITY * T // E
    et = pl.BlockSpec((None, E, T), lambda b, e: (b, 0, 0))
    return pl.pallas_call(
        functools.partial(_gather_kernel, cap=cap),
        grid=(B, E),
        in_specs=[et, et, pl.BlockSpec((None, T, D), lambda b, e: (b, 0, 0))],
        out_specs=[
            pl.BlockSpec((None, cap, D), lambda b, e: (e, b, 0)),
            pl.BlockSpec((None, cap, 1), lambda b, e: (e, b, 0)),
        ],
        out_shape=[
            jax.ShapeDtypeStruct((E, B * cap, D), BF16),
            jax.ShapeDtypeStruct((E, B * cap, 1), F32),
        ],
        compiler_params=_params(("arbitrary", "arbitrary")),
        name="gather",
    )(pos_t, gate_t, h)


def _moe_kernel(*refs, n_grp, tm):
    xs = refs[0:2 * n_grp:2]
    gs = refs[1:2 * n_grp:2]
    wg_ref, wu_ref, wd_ref = refs[2 * n_grp:2 * n_grp + 3]
    ys = refs[2 * n_grp + 3:3 * n_grp + 3]
    accs = refs[3 * n_grp + 3:4 * n_grp + 3]
    wgb, wub, wdb = refs[4 * n_grp + 3:]
    f = pl.program_id(1)
    nf = pl.num_programs(1)
    @pl.when(f == 0)
    def _():
        for acc in accs:
            acc[...] = jnp.zeros_like(acc)

    wgb[...] = wg_ref[...].astype(BF16)
    wub[...] = wu_ref[...].astype(BF16)
    wdb[...] = wd_ref[...].astype(BF16)

    for x_ref, acc in zip(xs, accs):
        rows = x_ref.shape[0]
        t = min(tm, rows)
        for r in range(rows // t):
            sl = slice(r * t, (r + 1) * t)
            xt = x_ref[sl, :]
            a = _dot(xt, wgb[...])
            u = _dot(xt, wub[...])
            acc[sl, :] += _dot((_silu(a) * u).astype(BF16), wdb[...])

    @pl.when(f == nf - 1)
    def _():
        for y_ref, acc, g_ref in zip(ys, accs, gs):
            y_ref[...] = (acc[...] * g_ref[...]).astype(BF16)


def _moe(groups, layer, w_gate, w_up, w_down, tf=512, tm=512):
    _, E, D, F = w_gate.shape
    in_specs, args, out_specs, out_shape, scratch = [], [], [], [], []
    for xs, gs in groups:
        R = xs.shape[1]
        in_specs += [pl.BlockSpec((None, R, D), lambda e, f: (e, 0, 0)), pl.BlockSpec((None, R, 1), lambda e, f: (e, 0, 0))]
        args += [xs, gs]
        out_specs.append(pl.BlockSpec((None, R, D), lambda e, f: (e, 0, 0)))
        out_shape.append(jax.ShapeDtypeStruct((E, R, D), BF16))
        scratch.append(pltpu.VMEM((R, D), F32))
    in_specs += [
        pl.BlockSpec((None, None, D, tf), lambda e, f: (layer, e, 0, f)),
        pl.BlockSpec((None, None, D, tf), lambda e, f: (layer, e, 0, f)),
        pl.BlockSpec((None, None, tf, D), lambda e, f: (layer, e, f, 0)),
    ]
    args += [w_gate, w_up, w_down]
    scratch += [pltpu.VMEM((D, tf), BF16), pltpu.VMEM((D, tf), BF16), pltpu.VMEM((tf, D), BF16)]
    return pl.pallas_call(
        functools.partial(_moe_kernel, n_grp=len(groups), tm=tm),
        grid=(E, F // tf),
        in_specs=in_specs,
        out_specs=out_specs,
        out_shape=out_shape,
        scratch_shapes=scratch,
        compiler_params=_params(("arbitrary", "arbitrary")),
        name="expert_ffn",
    )(*args)


def _combine_kernel(x_ref, gt_ref, ptok_ref, ys_ref, o_ref, *, tm):
    T, D = x_ref.shape
    E, cap, _ = ys_ref.shape
    t = min(tm, T)
    lane = lax.broadcasted_iota(jnp.int32, (t, cap), 1).astype(F32)

    def tile(r, carry):
        sl = pl.ds(pl.multiple_of(r * t, t), t)
        pt = ptok_ref[sl, :]
        acc = jnp.zeros((t, D), F32)
        for e in range(E):
            onehot = jnp.where(pt[:, e:e + 1] == lane, 1.0, 0.0).astype(BF16)
            acc = acc + _dot(onehot, ys_ref[e])
        o_ref[sl, :] = x_ref[sl, :] + gt_ref[...] * acc
        return carry

    lax.fori_loop(0, T // t, tile, 0)


def _combine(x, gt, mod_row, ptok, ys, tm=256):
    B, T, D = x.shape
    E = ys.shape[0]
    cap = ys.shape[1] // B
    ys4 = ys.reshape(E, B, cap, D)
    return pl.pallas_call(
        functools.partial(_combine_kernel, tm=tm),
        grid=(B,),
        in_specs=[
            pl.BlockSpec((None, T, D), lambda b: (b, 0, 0)),
            pl.BlockSpec((None, 1, D), lambda b: (mod_row(b), 0, 0)),
            pl.BlockSpec((None, T, LANES), lambda b: (b, 0, 0)),
            pl.BlockSpec((E, None, cap, D), lambda b: (0, b, 0, 0)),
        ],
        out_specs=pl.BlockSpec((None, T, D), lambda b: (b, 0, 0)),
        out_shape=jax.ShapeDtypeStruct((B, T, D), F32),
        compiler_params=_params(("arbitrary",)),
        name="combine",
    )(x, gt, ptok, ys4)


def _rope_tables(T):
    rows = (jnp.arange(T) // GRID_W).astype(F32)
    cols = (jnp.arange(T) % GRID_W).astype(F32)
    n_freq = ATTN_D // 4
    inv = ROPE_THETA ** (-jnp.arange(n_freq, dtype=F32) / n_freq)
    d = jnp.arange(GROUP_W) % ATTN_D
    freq = d % n_freq
    use_row = d < ATTN_D // 2
    ang = jnp.where(use_row[None, :], rows[:, None] * inv[freq][None, :], cols[:, None] * inv[freq][None, :])
    sign = jnp.where((d % (2 * n_freq)) < n_freq, -1.0, 1.0).astype(F32)
    return jnp.cos(ang).astype(F32), (jnp.sin(ang) * sign[None, :]).astype(F32)


def _segment_ones(width):
    i = jnp.arange(GROUP_W) // width
    return (i[:, None] == i[None, :]).astype(BF16)


def _block_diag(w):
    G, n, _ = w.shape
    eye = jnp.eye(G, dtype=w.dtype)
    return (eye[:, None, :, None] * w[:, :, None, :]).reshape(G * n, G * n)


def kernel(x, c, ctx, c_ctx, w_mod, b_mod, g_norm1, g_norm2, w_in, w_out, g_q, g_k, lam_q1, lam_k1, lam_q2, lam_k2,
           g_attn_out, w_pool, b_pool, pool_scale, conv_w, conv_b, conv_ln_g, conv_ln_b, w_pw2, sgu_ln_g, sgu_ln_b,
           w_spatial, b_spatial, w_router, w_gate, w_up, w_down):
    B, T, D = x.shape
    C = ctx.shape[1]
    L = w_mod.shape[0]
    assert B < MOD_ROWS and D == D_MODEL and T % TIME_CHUNK == 0 and C % TIME_CHUNK == 0

    cond = jnp.zeros((MOD_ROWS, D), F32).at[:B].set(c).at[B].set(c_ctx)
    mods = _modulation(cond, w_mod, b_mod).reshape(L, MOD_ROWS, N_MOD, 1, D)
    lat_row = lambda b: b
    ctx_row = lambda b: B

    cos_x, sin_x = _rope_tables(T)
    cos_c, sin_c = jnp.ones((C, GROUP_W), F32), jnp.zeros((C, GROUP_W), F32)
    seg32 = _segment_ones(ATTN_D)
    tri = (jnp.arange(LANES)[:, None] <= jnp.arange(LANES)[None, :]).astype(BF16)
    tile_g = lambda g: jnp.tile(g, GROUP_W // g.shape[0]).reshape(1, GROUP_W)
    vec = lambda v: v.reshape(1, -1)

    for l in range(L):
        last = l == L - 1
        lam_init = 0.8 - 0.6 * math.exp(-0.3 * l)
        sh1, sc1, gt1, sh2, sc2, gt2 = (mods[l, :, i] for i in range(N_MOD))
        w_in_bf = w_in[l].astype(BF16)
        w_out_bf = w_out[l].astype(BF16)
        wr = jnp.zeros((D, LANES), F32).at[:, :N_EXPERTS].set(w_router[l])
        wr_hi = wr.astype(BF16)
        wr_lo = (wr - wr_hi.astype(F32)).astype(BF16)
        lams = [vec(p[l]) for p in (lam_q1, lam_k1, lam_q2, lam_k2)]
        gq_t, gk_t, go_t = tile_g(g_q[l]), tile_g(g_k[l]), tile_g(g_attn_out[l])
        mix_w = (_block_diag(w_pool[l]).astype(BF16), vec(b_pool[l]), vec(pool_scale[l]), conv_w[l], vec(conv_b[l]),
                 vec(conv_ln_g[l]), vec(conv_ln_b[l]), w_pw2[l].astype(BF16), vec(sgu_ln_g[l]), vec(sgu_ln_b[l]),
                 w_spatial[l].astype(BF16), jnp.repeat(b_spatial[l].T, SGU_GW, axis=1))
        g1, g2 = vec(g_norm1[l]), vec(g_norm2[l])

        qx, kx, vx, rx = _inproj(x, lat_row, g1, sc1, sh1, w_in_bf, gq_t, gk_t, cos_x, sin_x, seg32, tm=512)
        qc, kc, vc, rc = _inproj(ctx, ctx_row, g1, sc1, sh1, w_in_bf, gq_t, gk_t, cos_c, sin_c, seg32, tm=256)

        attn_x = _attention(qx, [(kc, vc), (kx, vx)], lams, go_t, lam_init, tq=256)
        x, h2x, aff_x = _outproj(x, attn_x, _mixers(rx, *mix_w), lat_row, w_out_bf, gt1, g2, sc2, sh2, wr_hi, wr_lo, tm=512)
        pos_x, gate_x, ptok_x = _route(aff_x, tri, EC_CAPACITY * T // N_EXPERTS)
        groups = [_gather(pos_x, gate_x, h2x)]

        if not last:
            attn_c = _attention(qc, [(kc, vc)], lams, go_t, lam_init, tq=256)
            ctx, h2c, aff_c = _outproj(ctx, attn_c, _mixers(rc, *mix_w), ctx_row, w_out_bf, gt1, g2, sc2, sh2,
                                       wr_hi, wr_lo, tm=256)
            pos_c, gate_c, ptok_c = _route(aff_c, tri, EC_CAPACITY * C // N_EXPERTS)
            groups.append(_gather(pos_c, gate_c, h2c))

        ys = _moe(groups, l, w_gate, w_up, w_down)
        x = _combine(x, gt2, lat_row, ptok_x, ys[0])
        if not last:
            ctx = _combine(ctx, gt2, ctx_row, ptok_c, ys[1])
    return x
```

```python
import functools
import math

import jax
import jax.numpy as jnp
from jax import lax
from jax.experimental import pallas as pl
from jax.experimental.pallas import tpu as pltpu

F32 = jnp.float32
BF16 = jnp.bfloat16

D_MODEL = 1024
GRID_W = 64
GROUP_W = 256
ATTN_HEADS = 4
ATTN_D = 32
ATTN_VD = 64
ROPE_THETA = 10000.0
POOL_WINDOWS = (2, 4, 8, 16)
POOL_GW = 64
CONV_K = 31
SGU_CHUNK = 128
SGU_GROUPS = 4
SGU_GW = 64
N_EXPERTS = 16
EC_CAPACITY = 2
D_EXPERT = 2048
N_MOD = 6
IN_W = 2048
EPS = 1e-6
LOG2E = 1.4426950408889634

MOD_ROWS = 16
LANES = 128
SUBLANES = 8
HALO = 16
TIME_CHUNK = 256
VMEM_LIMIT = 56 * 1024 * 1024


def _params(sem):
    return pltpu.CompilerParams(dimension_semantics=sem, vmem_limit_bytes=VMEM_LIMIT)


def _dot(a, b):
    return jnp.dot(a, b, preferred_element_type=F32)


def _dot_t(a, b):
    return lax.dot_general(a, b, (((1,), (1,)), ((), ())), preferred_element_type=F32)


def _split_dot(a, w):
    hi = a.astype(BF16)
    lo = (a - hi.astype(F32)).astype(BF16)
    return _dot(hi, w) + _dot(lo, w)


def _silu(x):
    return x * jax.nn.sigmoid(x)


def _layer_norm(x, g, b):
    mu = jnp.mean(x, axis=-1, keepdims=True)
    xc = x - mu
    var = jnp.mean(xc * xc, axis=-1, keepdims=True)
    return xc * lax.rsqrt(var + EPS) * g + b


def _mod_kernel(c_ref, w_ref, b_ref, o_ref):
    s = _silu(c_ref[...])
    o_ref[...] = _dot(s.astype(BF16), w_ref[...].astype(BF16)) + b_ref[...]


def _modulation(cond, w_mod, b_mod):
    L, D, N = w_mod.shape
    tn = 1024
    return pl.pallas_call(
        _mod_kernel,
        grid=(L, N // tn),
        in_specs=[
            pl.BlockSpec((MOD_ROWS, D), lambda l, j: (0, 0)),
            pl.BlockSpec((None, D, tn), lambda l, j: (l, 0, j)),
            pl.BlockSpec((None, 1, tn), lambda l, j: (l, 0, j)),
        ],
        out_specs=pl.BlockSpec((None, MOD_ROWS, tn), lambda l, j: (l, 0, j)),
        out_shape=jax.ShapeDtypeStruct((L, MOD_ROWS, N), F32),
        compiler_params=_params(("arbitrary", "arbitrary")),
        name="modulation",
    )(cond, w_mod, b_mod.reshape(L, 1, N))


def _inproj_kernel(x_ref, g_ref, sc_ref, sh_ref, w_ref, gq_ref, gk_ref, cos_ref, sin_ref, seg_ref,
                   q_ref, k_ref, v_ref, r_ref, *, sub):
    gain = g_ref[...] * (1.0 + sc_ref[...])
    seg = seg_ref[...]
    first = (lax.broadcasted_iota(jnp.int32, (sub, GROUP_W), 1) & 15) < 8

    for r in range(x_ref.shape[0] // sub):
        rows = slice(r * sub, (r + 1) * sub)
        x = x_ref[rows, :]
        h = x * lax.rsqrt(jnp.mean(x * x, axis=-1, keepdims=True) + EPS) * gain + sh_ref[...]
        p = _dot(h.astype(BF16), w_ref[...])
        cos = cos_ref[rows, :]
        sin = sin_ref[rows, :]

        def prep(a, g):
            ss = _split_dot(a * a, seg)
            n = a * lax.rsqrt(ss * (1.0 / ATTN_D) + EPS) * g
            partner = jnp.where(first, pltpu.roll(n, GROUP_W - 8, 1), pltpu.roll(n, 8, 1))
            return n * cos + partner * sin

        q_ref[rows, :] = (prep(p[:, 0:GROUP_W], gq_ref[...]) * (ATTN_D ** -0.5 * LOG2E)).astype(BF16)
        k_ref[:, rows] = jnp.transpose(prep(p[:, GROUP_W:2 * GROUP_W], gk_ref[...])).astype(BF16)
        v_ref[rows, :] = p[:, 2 * GROUP_W:3 * GROUP_W].astype(BF16)
        r_ref[rows, :] = p[:, 3 * GROUP_W:]


def _inproj(x, mod_row, g1, sc, sh, w_bf, gq_t, gk_t, cos_t, sin_t, seg32, tm, sub=256):
    B, T, D = x.shape
    rest_w = IN_W - 3 * GROUP_W
    row = lambda b, i: (mod_row(b), 0, 0)
    const2 = lambda b, i: (0, 0)
    tok = lambda b, i: (b, i, 0)
    return pl.pallas_call(
        functools.partial(_inproj_kernel, sub=min(sub, tm)),
        grid=(B, T // tm),
        in_specs=[
            pl.BlockSpec((None, tm, D), tok),
            pl.BlockSpec((1, D), const2),
            pl.BlockSpec((None, 1, D), row),
            pl.BlockSpec((None, 1, D), row),
            pl.BlockSpec((D, IN_W), const2),
            pl.BlockSpec((1, GROUP_W), const2),
            pl.BlockSpec((1, GROUP_W), const2),
            pl.BlockSpec((tm, GROUP_W), lambda b, i: (i, 0)),
            pl.BlockSpec((tm, GROUP_W), lambda b, i: (i, 0)),
            pl.BlockSpec((GROUP_W, GROUP_W), const2),
        ],
        out_specs=[
            pl.BlockSpec((None, tm, GROUP_W), tok),
            pl.BlockSpec((None, GROUP_W, tm), lambda b, i: (b, 0, i)),
            pl.BlockSpec((None, tm, GROUP_W), tok),
            pl.BlockSpec((None, tm, rest_w), tok),
        ],
        out_shape=[
            jax.ShapeDtypeStruct((B, T, GROUP_W), BF16),
            jax.ShapeDtypeStruct((B, GROUP_W, T), BF16),
            jax.ShapeDtypeStruct((B, T, GROUP_W), BF16),
            jax.ShapeDtypeStruct((B, T, rest_w), F32),
        ],
        compiler_params=_params(("arbitrary", "arbitrary")),
        name="inproj",
    )(x, g1, sc, sh, w_bf, gq_t, gk_t, cos_t, sin_t, seg32)


def _attn_kernel(*refs, n_seg, lam_init):
    q_ref = refs[0]
    kv_refs = refs[1:1 + 2 * n_seg]
    lq1_ref, lk1_ref, lq2_ref, lk2_ref, go_ref, o_ref = refs[1 + 2 * n_seg:]
    lam = (jnp.exp(jnp.sum(lq1_ref[...] * lk1_ref[...], axis=-1, keepdims=True))
           - jnp.exp(jnp.sum(lq2_ref[...] * lk2_ref[...], axis=-1, keepdims=True)) + lam_init)
    q = q_ref[...]

    def scores(off):
        qs = q[:, off:off + ATTN_D]
        return [_dot(qs, kv_refs[2 * i][off:off + ATTN_D, :]) for i in range(n_seg)]

    def softmax_parts(s):
        mx = functools.reduce(jnp.maximum, [jnp.max(si, axis=-1, keepdims=True) for si in s])
        p = [jnp.exp2(si - mx) for si in s]
        den = functools.reduce(jnp.add, [jnp.sum(pi, axis=-1, keepdims=True) for pi in p])
        return [pi.astype(BF16) for pi in p], den

    heads = []
    nxt = (scores(0), scores(ATTN_D))
    for h in range(ATTN_HEADS):
        cur = nxt
        if h + 1 < ATTN_HEADS:
            nxt = (scores((h + 1) * ATTN_VD), scores((h + 1) * ATTN_VD + ATTN_D))
        p1, l1 = softmax_parts(cur[0])
        p2, l2 = softmax_parts(cur[1])
        c = (lam * l1 / l2).astype(BF16)
        o = None
        for i in range(n_seg):
            oi = _dot(p1[i] - c * p2[i], kv_refs[2 * i + 1][:, h * ATTN_VD:(h + 1) * ATTN_VD])
            o = oi if o is None else o + oi
        o = o * (1.0 / l1)
        ms = jnp.mean(o * o, axis=-1, keepdims=True)
        heads.append(o * lax.rsqrt(ms + EPS))
    o_ref[...] = (jnp.concatenate(heads, axis=1) * go_ref[...] * (1.0 - lam_init)).astype(BF16)


def _attention(q, kvs, lams, go_t, lam_init, tq):
    B, T, _ = q.shape
    tok = lambda b, i: (b, i, 0)
    const2 = lambda b, i: (0, 0)
    in_specs = [pl.BlockSpec((None, tq, GROUP_W), tok)]
    args = [q]
    for kt, v in kvs:
        n = v.shape[1]
        in_specs += [pl.BlockSpec((None, GROUP_W, n), lambda b, i: (b, 0, 0)),
                     pl.BlockSpec((None, n, GROUP_W), lambda b, i: (b, 0, 0))]
        args += [kt, v]
    in_specs += [pl.BlockSpec((1, ATTN_D), const2)] * 4
    in_specs += [pl.BlockSpec((1, GROUP_W), const2)]
    args += list(lams) + [go_t]
    return pl.pallas_call(
        functools.partial(_attn_kernel, n_seg=len(kvs), lam_init=lam_init),
        grid=(B, T // tq),
        in_specs=in_specs,
        out_specs=pl.BlockSpec((None, tq, GROUP_W), tok),
        out_shape=jax.ShapeDtypeStruct((B, T, GROUP_W), BF16),
        compiler_params=_params(("arbitrary", "arbitrary")),
        name="diff_attention",
    )(*args)


def _gelu_tanh(x):
    return 0.5 * x * (1.0 + jnp.tanh(0.7978845608028654 * (x + 0.044715 * (x * x * x))))


def _mix_kernel(cur_ref, prev_ref, next_ref, wpool_ref, bpool_ref, pscale_ref, convw_ref, convb_ref, clng_ref,
                clnb_ref, wpw2_ref, slng_ref, slnb_ref, wsp_ref, bsp_ref, o_ref, zwin, ywin, *, T):
    i = pl.program_id(1)
    has_prev = jnp.where(i > 0, 1.0, 0.0)
    has_next = jnp.where(i < pl.num_programs(1) - 1, 1.0, 0.0)

    def glu(ref):
        return ref[:, GROUP_W:2 * GROUP_W] * jax.nn.sigmoid(ref[:, 2 * GROUP_W:3 * GROUP_W])

    win_rows = TIME_CHUNK + 2 * HALO

    def fill(win, head, body, tail):
        win[0, 0:HALO, :] = head * has_prev
        win[0, HALO:HALO + TIME_CHUNK, :] = body
        win[0, HALO + TIME_CHUNK:, :] = tail * has_next
        whole = win[0]
        for r in range(1, SUBLANES):
            win[r] = pltpu.roll(whole, win_rows - r, 0)

    def shifted(win, row):
        base = row - row % SUBLANES
        return win[row % SUBLANES, base:base + TIME_CHUNK, :]

    fill(zwin, prev_ref[:, 0:GROUP_W], cur_ref[:, 0:GROUP_W], next_ref[:, 0:GROUP_W])
    fill(ywin, glu(prev_ref), glu(cur_ref), glu(next_ref))

    def zs(off):
        return shifted(zwin, HALO + off)

    t = i * TIME_CHUNK + lax.broadcasted_iota(jnp.int32, (TIME_CHUNK, 1), 0)
    pool_group = lax.broadcasted_iota(jnp.int32, (TIME_CHUNK, GROUP_W), 1) >> 6
    z0 = zs(0)
    sums = []
    acc = None
    for w in POOL_WINDOWS:
        for off in range(-(w // 2), w // 2):
            if acc is None or not (-(w // 4) <= off < w // 4):
                acc = zs(off) if acc is None else acc + zs(off)
        sums.append(acc)
    mean = None
    for g, w in enumerate(POOL_WINDOWS):
        cnt = (jnp.minimum(t + w // 2, T) - jnp.maximum(t - w // 2, 0)).astype(F32)
        mg = sums[g] / cnt
        mean = mg if mean is None else jnp.where(pool_group == g, mg, mean)
    pool = (_dot((mean - z0).astype(BF16), wpool_ref[...]) + bpool_ref[...]) * pscale_ref[...]
    o_ref[:, 0:GROUP_W] = pool.astype(BF16)

    acc = None
    for k in range(CONV_K):
        term = shifted(ywin, HALO + k - CONV_K // 2) * convw_ref[k:k + 1, :]
        acc = term if acc is None else acc + term
    cn = _silu(_layer_norm(acc + convb_ref[...], clng_ref[...], clnb_ref[...]))
    o_ref[:, GROUP_W:2 * GROUP_W] = _dot(cn.astype(BF16), wpw2_ref[...]).astype(BF16)

    sgu_masks = [((lax.broadcasted_iota(jnp.int32, (1, GROUP_W), 1) >> 6) == g).astype(F32) for g in range(SGU_GROUPS)]
    gl = _gelu_tanh(cur_ref[:, 3 * GROUP_W:5 * GROUP_W])
    u = gl[:, 0:GROUP_W]
    vn = _layer_norm(gl[:, GROUP_W:2 * GROUP_W], slng_ref[...], slnb_ref[...])
    for j in range(TIME_CHUNK // SGU_CHUNK):
        rows = slice(j * SGU_CHUNK, (j + 1) * SGU_CHUNK)
        s = bsp_ref[...]
        for g in range(SGU_GROUPS):
            s = s + _dot(wsp_ref[g], (vn[rows] * sgu_masks[g]).astype(BF16))
        o_ref[rows, 2 * GROUP_W:3 * GROUP_W] = (u[rows] * s).astype(BF16)


def _mixers(rest, wpool_bd, bpool, pscale, convw, convb, clng, clnb, wpw2, slng, slnb, wsp, bsp):
    B, T, RW = rest.shape
    n_chunk = T // TIME_CHUNK
    per_chunk = TIME_CHUNK // HALO
    const2 = lambda b, i: (0, 0)
    vec = pl.BlockSpec((1, GROUP_W), const2)
    mat = pl.BlockSpec((GROUP_W, GROUP_W), const2)
    return pl.pallas_call(
        functools.partial(_mix_kernel, T=T),
        grid=(B, n_chunk),
        in_specs=[
            pl.BlockSpec((None, TIME_CHUNK, RW), lambda b, i: (b, i, 0)),
            pl.BlockSpec((None, HALO, RW), lambda b, i: (b, jnp.maximum(i * per_chunk - 1, 0), 0)),
            pl.BlockSpec((None, HALO, RW), lambda b, i: (b, jnp.minimum((i + 1) * per_chunk, n_chunk * per_chunk - 1), 0)),
            mat, vec, vec,
            pl.BlockSpec((CONV_K, GROUP_W), const2), vec, vec, vec, mat,
            vec, vec,
            pl.BlockSpec((SGU_GROUPS, SGU_CHUNK, SGU_CHUNK), lambda b, i: (0, 0, 0)),
            pl.BlockSpec((SGU_CHUNK, GROUP_W), const2),
        ],
        out_specs=pl.BlockSpec((None, TIME_CHUNK, 3 * GROUP_W), lambda b, i: (b, i, 0)),
        out_shape=jax.ShapeDtypeStruct((B, T, 3 * GROUP_W), BF16),
        scratch_shapes=[pltpu.VMEM((SUBLANES, TIME_CHUNK + 2 * HALO, GROUP_W), F32)] * 2,
        compiler_params=_params(("arbitrary", "arbitrary")),
        name="mixers",
    )(rest, rest, rest, wpool_bd, bpool, pscale, convw, convb, clng, clnb, wpw2, slng, slnb, wsp, bsp)


def _outproj_kernel(x_ref, a_ref, m_ref, w_ref, gt_ref, g2_ref, sc_ref, sh_ref, wr_ref,
                    xo_ref, h_ref, aff_ref, *, sub):
    gain = g2_ref[...] * (1.0 + sc_ref[...])
    for r in range(x_ref.shape[0] // sub):
        rows = slice(r * sub, (r + 1) * sub)
        mix = _dot(a_ref[rows, :], w_ref[0:GROUP_W, :]) + _dot(m_ref[rows, :], w_ref[GROUP_W:, :])
        xn = x_ref[rows, :] + gt_ref[...] * mix
        xo_ref[rows, :] = xn
        h = xn * lax.rsqrt(jnp.mean(xn * xn, axis=-1, keepdims=True) + EPS) * gain + sh_ref[...]
        hi = h.astype(BF16)
        h_ref[rows, :] = hi
        lo = (h - hi.astype(F32)).astype(BF16)
        d = hi.shape[1]
        logits = _dot(hi, wr_ref[0:d, :]) + _dot(hi, wr_ref[d:2 * d, :]) + _dot(lo, wr_ref[0:d, :])
        lt = jnp.transpose(logits)[0:N_EXPERTS, :]
        e = jnp.exp(lt - jnp.max(lt, axis=0, keepdims=True))
        aff_ref[:, rows] = e / jnp.sum(e, axis=0, keepdims=True)


def _outproj(x, attn, mixr, mod_row, w_bf, gt, g2, sc, sh, wr_cat, tm, sub=256):
    B, T, D = x.shape
    row = lambda b, i: (mod_row(b), 0, 0)
    const2 = lambda b, i: (0, 0)
    tok = lambda b, i: (b, i, 0)
    return pl.pallas_call(
        functools.partial(_outproj_kernel, sub=min(sub, tm)),
        grid=(B, T // tm),
        in_specs=[
            pl.BlockSpec((None, tm, D), tok),
            pl.BlockSpec((None, tm, GROUP_W), tok),
            pl.BlockSpec((None, tm, 3 * GROUP_W), tok),
            pl.BlockSpec((D, D), const2),
            pl.BlockSpec((None, 1, D), row),
            pl.BlockSpec((1, D), const2),
            pl.BlockSpec((None, 1, D), row),
            pl.BlockSpec((None, 1, D), row),
            pl.BlockSpec((2 * D, LANES), const2),
        ],
        out_specs=[
            pl.BlockSpec((None, tm, D), tok),
            pl.BlockSpec((None, tm, D), tok),
            pl.BlockSpec((None, N_EXPERTS, tm), lambda b, i: (b, 0, i)),
        ],
        out_shape=[
            jax.ShapeDtypeStruct((B, T, D), F32),
            jax.ShapeDtypeStruct((B, T, D), BF16),
            jax.ShapeDtypeStruct((B, N_EXPERTS, T), F32),
        ],
        compiler_params=_params(("arbitrary", "arbitrary")),
        name="outproj",
    )(x, attn, mixr, w_bf, gt, g2, sc, sh, wr_cat)


def _lane_prefix(x, tri):
    outs = []
    off = jnp.zeros((x.shape[0], 1), F32)
    for j in range(x.shape[1] // LANES):
        xb = x[:, j * LANES:(j + 1) * LANES]
        inc = _dot(xb.astype(BF16), tri)
        outs.append(inc - xb + off)
        off = off + inc[:, LANES - 1:LANES]
    return jnp.concatenate(outs, axis=1)


def _route_kernel(a_ref, tri_ref, pos_ref, gate_ref, ptok_ref, *, cap, n_expert):
    a = a_ref[...]
    R, T = a.shape

    def enough(c):
        return jnp.sum(jnp.where(a >= c, 1.0, 0.0), axis=-1, keepdims=True) >= cap

    def bit_search(i, lo):
        cand = lo | lax.shift_left(jnp.int32(1), 30 - i)
        return jnp.where(enough(pltpu.bitcast(cand, F32)), cand, lo)

    lo_bits = lax.fori_loop(0, 31, bit_search, jnp.zeros((R, 1), jnp.int32))

    def refine(i, c):
        lo, hi = c
        mid = lo + 0.5 * (hi - lo)
        ok = enough(mid)
        return jnp.where(ok, mid, lo), jnp.where(ok, hi, mid)

    thr, _ = lax.fori_loop(0, 30, refine, (pltpu.bitcast(lo_bits, F32), pltpu.bitcast(lo_bits + 1, F32)))
    gt = a > thr
    eq = a == thr
    need = cap - jnp.sum(jnp.where(gt, 1.0, 0.0), axis=-1, keepdims=True)
    tri = tri_ref[...]
    tie_rank = _lane_prefix(jnp.where(eq, 1.0, 0.0), tri)
    sel = jnp.logical_or(gt, jnp.logical_and(eq, tie_rank < need))
    pos = jnp.where(sel, _lane_prefix(jnp.where(sel, 1.0, 0.0), tri), -1.0)
    pos_ref[...] = pos
    gate_ref[...] = jnp.where(sel, a, 0.0)
    fill = jnp.full((LANES - n_expert, T), -1.0, F32)
    for b in range(R // n_expert):
        padded = jnp.concatenate([pos[b * n_expert:(b + 1) * n_expert], fill], axis=0)
        ptok_ref[b] = jnp.transpose(padded)


def _route(aff_t, tri, cap):
    B, E, T = aff_t.shape
    blk = pl.BlockSpec((B * E, T), lambda i: (0, 0))
    pos, gate, ptok = pl.pallas_call(
        functools.partial(_route_kernel, cap=cap, n_expert=E),
        grid=(1,),
        in_specs=[blk, pl.BlockSpec((LANES, LANES), lambda i: (0, 0))],
        out_specs=[blk, blk, pl.BlockSpec((B, T, LANES), lambda i: (0, 0, 0))],
        out_shape=[
            jax.ShapeDtypeStruct((B * E, T), F32),
            jax.ShapeDtypeStruct((B * E, T), F32),
            jax.ShapeDtypeStruct((B, T, LANES), F32),
        ],
        compiler_params=_params(("arbitrary",)),
        name="route",
    )(aff_t.reshape(B * E, T), tri)
    return pos.reshape(B, E, T), gate.reshape(B, E, T), ptok


def _gather_kernel(pos_ref, gate_ref, h_ref, xs_ref, gs_ref, *, cap):
    e = pl.program_id(1)
    prow = pos_ref[pl.ds(e, 1), :]
    grow = gate_ref[pl.ds(e, 1), :]
    T = prow.shape[1]
    slot = lax.broadcasted_iota(jnp.int32, (cap, T), 0).astype(F32)
    hit = slot == prow
    onehot = jnp.where(hit, 1.0, 0.0).astype(BF16)
    xs_ref[...] = _dot(onehot, h_ref[...]).astype(BF16)
    gs_ref[...] = jnp.sum(jnp.where(hit, grow, 0.0), axis=-1, keepdims=True)


def _gather(pos_t, gate_t, h):
    B, E, T = pos_t.shape
    D = h.shape[-1]
    cap = EC_CAPACITY * T // E
    et = pl.BlockSpec((None, E, T), lambda b, e: (b, 0, 0))
    return pl.pallas_call(
        functools.partial(_gather_kernel, cap=cap),
        grid=(B, E),
        in_specs=[et, et, pl.BlockSpec((None, T, D), lambda b, e: (b, 0, 0))],
        out_specs=[
            pl.BlockSpec((None, cap, D), lambda b, e: (e, b, 0)),
            pl.BlockSpec((None, cap, 1), lambda b, e: (e, b, 0)),
        ],
        out_shape=[
            jax.ShapeDtypeStruct((E, B * cap, D), BF16),
            jax.ShapeDtypeStruct((E, B * cap, 1), F32),
        ],
        compiler_params=_params(("arbitrary", "arbitrary")),
        name="gather",
    )(pos_t, gate_t, h)


def _moe_kernel(*refs, n_grp, tm):
    xs = refs[0:2 * n_grp:2]
    gs = refs[1:2 * n_grp:2]
    wg_ref, wu_ref, wd_ref = refs[2 * n_grp:2 * n_grp + 3]
    ys = refs[2 * n_grp + 3:3 * n_grp + 3]
    accs = refs[3 * n_grp + 3:4 * n_grp + 3]
    wgb, wub, wdb = refs[4 * n_grp + 3:]
    f = pl.program_id(1)
    nf = pl.num_programs(1)
    @pl.when(f == 0)
    def _():
        for acc in accs:
            acc[...] = jnp.zeros_like(acc)

    wgb[...] = wg_ref[...].astype(BF16)
    wub[...] = wu_ref[...].astype(BF16)
    wdb[...] = wd_ref[...].astype(BF16)

    for x_ref, acc in zip(xs, accs):
        rows = x_ref.shape[0]
        t = min(tm, rows)
        for r in range(rows // t):
            sl = slice(r * t, (r + 1) * t)
            xt = x_ref[sl, :]
            a = _dot(xt, wgb[...])
            u = _dot(xt, wub[...])
            acc[sl, :] += _dot((_silu(a) * u).astype(BF16), wdb[...])

    @pl.when(f == nf - 1)
    def _():
        for y_ref, acc, g_ref in zip(ys, accs, gs):
            y_ref[...] = (acc[...] * g_ref[...]).astype(BF16)


def _moe(groups, layer, w_gate, w_up, w_down, tf=512, tm=512):
    _, E, D, F = w_gate.shape
    in_specs, args, out_specs, out_shape, scratch = [], [], [], [], []
    for xs, gs in groups:
        R = xs.shape[1]
        in_specs += [pl.BlockSpec((None, R, D), lambda e, f: (e, 0, 0)), pl.BlockSpec((None, R, 1), lambda e, f: (e, 0, 0))]
        args += [xs, gs]
        out_specs.append(pl.BlockSpec((None, R, D), lambda e, f: (e, 0, 0)))
        out_shape.append(jax.ShapeDtypeStruct((E, R, D), BF16))
        scratch.append(pltpu.VMEM((R, D), F32))
    in_specs += [
        pl.BlockSpec((None, None, D, tf), lambda e, f: (layer, e, 0, f)),
        pl.BlockSpec((None, None, D, tf), lambda e, f: (layer, e, 0, f)),
        pl.BlockSpec((None, None, tf, D), lambda e, f: (layer, e, f, 0)),
    ]
    args += [w_gate, w_up, w_down]
    scratch += [pltpu.VMEM((D, tf), BF16), pltpu.VMEM((D, tf), BF16), pltpu.VMEM((tf, D), BF16)]
    return pl.pallas_call(
        functools.partial(_moe_kernel, n_grp=len(groups), tm=tm),
        grid=(E, F // tf),
        in_specs=in_specs,
        out_specs=out_specs,
        out_shape=out_shape,
        scratch_shapes=scratch,
        compiler_params=_params(("arbitrary", "arbitrary")),
        name="expert_ffn",
    )(*args)


def _combine_kernel(x_ref, gt_ref, ptok_ref, ys_ref, o_ref, *, tm):
    T, D = x_ref.shape
    E, cap, _ = ys_ref.shape
    t = min(tm, T)
    lane = lax.broadcasted_iota(jnp.int32, (t, cap), 1).astype(F32)

    def tile(r, carry):
        sl = pl.ds(pl.multiple_of(r * t, t), t)
        pt = ptok_ref[sl, :]
        acc = jnp.zeros((t, D), F32)
        for e in range(E):
            onehot = jnp.where(pt[:, e:e + 1] == lane, 1.0, 0.0).astype(BF16)
            acc = acc + _dot(onehot, ys_ref[e])
        o_ref[sl, :] = x_ref[sl, :] + gt_ref[...] * acc
        return carry

    lax.fori_loop(0, T // t, tile, 0)


def _combine(x, gt, mod_row, ptok, ys, tm=256):
    B, T, D = x.shape
    E = ys.shape[0]
    cap = ys.shape[1] // B
    ys4 = ys.reshape(E, B, cap, D)
    return pl.pallas_call(
        functools.partial(_combine_kernel, tm=tm),
        grid=(B,),
        in_specs=[
            pl.BlockSpec((None, T, D), lambda b: (b, 0, 0)),
            pl.BlockSpec((None, 1, D), lambda b: (mod_row(b), 0, 0)),
            pl.BlockSpec((None, T, LANES), lambda b: (b, 0, 0)),
            pl.BlockSpec((E, None, cap, D), lambda b: (0, b, 0, 0)),
        ],
        out_specs=pl.BlockSpec((None, T, D), lambda b: (b, 0, 0)),
        out_shape=jax.ShapeDtypeStruct((B, T, D), F32),
        compiler_params=_params(("arbitrary",)),
        name="combine",
    )(x, gt, ptok, ys4)


def _rope_tables(T):
    rows = (jnp.arange(T) // GRID_W).astype(F32)
    cols = (jnp.arange(T) % GRID_W).astype(F32)
    n_freq = ATTN_D // 4
    inv = ROPE_THETA ** (-jnp.arange(n_freq, dtype=F32) / n_freq)
    d = jnp.arange(GROUP_W) % ATTN_D
    freq = d % n_freq
    use_row = d < ATTN_D // 2
    ang = jnp.where(use_row[None, :], rows[:, None] * inv[freq][None, :], cols[:, None] * inv[freq][None, :])
    sign = jnp.where((d % (2 * n_freq)) < n_freq, -1.0, 1.0).astype(F32)
    return jnp.cos(ang).astype(F32), (jnp.sin(ang) * sign[None, :]).astype(F32)


def _segment_ones(width):
    i = jnp.arange(GROUP_W) // width
    return (i[:, None] == i[None, :]).astype(BF16)


def _block_diag(w):
    G, n, _ = w.shape
    eye = jnp.eye(G, dtype=w.dtype)
    return (eye[:, None, :, None] * w[:, :, None, :]).reshape(G * n, G * n)


def kernel(x, c, ctx, c_ctx, w_mod, b_mod, g_norm1, g_norm2, w_in, w_out, g_q, g_k, lam_q1, lam_k1, lam_q2, lam_k2,
           g_attn_out, w_pool, b_pool, pool_scale, conv_w, conv_b, conv_ln_g, conv_ln_b, w_pw2, sgu_ln_g, sgu_ln_b,
           w_spatial, b_spatial, w_router, w_gate, w_up, w_down):
    B, T, D = x.shape
    C = ctx.shape[1]
    L = w_mod.shape[0]
    assert B < MOD_ROWS and D == D_MODEL and T % TIME_CHUNK == 0 and C % TIME_CHUNK == 0

    cond = jnp.zeros((MOD_ROWS, D), F32).at[:B].set(c).at[B].set(c_ctx)
    mods = _modulation(cond, w_mod, b_mod).reshape(L, MOD_ROWS, N_MOD, 1, D)
    lat_row = lambda b: b
    ctx_row = lambda b: B

    cos_x, sin_x = _rope_tables(T)
    cos_c, sin_c = jnp.ones((C, GROUP_W), F32), jnp.zeros((C, GROUP_W), F32)
    seg32 = _segment_ones(ATTN_D)
    tri = (jnp.arange(LANES)[:, None] <= jnp.arange(LANES)[None, :]).astype(BF16)
    tile_g = lambda g: jnp.tile(g, GROUP_W // g.shape[0]).reshape(1, GROUP_W)
    vec = lambda v: v.reshape(1, -1)

    for l in range(L):
        last = l == L - 1
        lam_init = 0.8 - 0.6 * math.exp(-0.3 * l)
        sh1, sc1, gt1, sh2, sc2, gt2 = (mods[l, :, i] for i in range(N_MOD))
        w_in_bf = w_in[l].astype(BF16)
        w_out_bf = w_out[l].astype(BF16)
        wr = jnp.zeros((D, LANES), F32).at[:, :N_EXPERTS].set(w_router[l])
        wr_hi = wr.astype(BF16)
        wr_cat = jnp.concatenate([wr_hi, (wr - wr_hi.astype(F32)).astype(BF16)], axis=0)
        lams = [vec(p[l]) for p in (lam_q1, lam_k1, lam_q2, lam_k2)]
        gq_t, gk_t, go_t = tile_g(g_q[l]), tile_g(g_k[l]), tile_g(g_attn_out[l])
        mix_w = (_block_diag(w_pool[l]).astype(BF16), vec(b_pool[l]), vec(pool_scale[l]), conv_w[l], vec(conv_b[l]),
                 vec(conv_ln_g[l]), vec(conv_ln_b[l]), w_pw2[l].astype(BF16), vec(sgu_ln_g[l]), vec(sgu_ln_b[l]),
                 w_spatial[l].astype(BF16), jnp.repeat(b_spatial[l].T, SGU_GW, axis=1))
        g1, g2 = vec(g_norm1[l]), vec(g_norm2[l])

        qx, kx, vx, rx = _inproj(x, lat_row, g1, sc1, sh1, w_in_bf, gq_t, gk_t, cos_x, sin_x, seg32, tm=512)
        qc, kc, vc, rc = _inproj(ctx, ctx_row, g1, sc1, sh1, w_in_bf, gq_t, gk_t, cos_c, sin_c, seg32, tm=256)

        attn_x = _attention(qx, [(kc, vc), (kx, vx)], lams, go_t, lam_init, tq=256)
        x, h2x, aff_x = _outproj(x, attn_x, _mixers(rx, *mix_w), lat_row, w_out_bf, gt1, g2, sc2, sh2, wr_cat, tm=512)
        pos_x, gate_x, ptok_x = _route(aff_x, tri, EC_CAPACITY * T // N_EXPERTS)
        groups = [_gather(pos_x, gate_x, h2x)]

        if not last:
            attn_c = _attention(qc, [(kc, vc)], lams, go_t, lam_init, tq=256)
            ctx, h2c, aff_c = _outproj(ctx, attn_c, _mixers(rc, *mix_w), ctx_row, w_out_bf, gt1, g2, sc2, sh2,
                                       wr_cat, tm=256)
            pos_c, gate_c, ptok_c = _route(aff_c, tri, EC_CAPACITY * C // N_EXPERTS)
            groups.append(_gather(pos_c, gate_c, h2c))

        ys = _moe(groups, l, w_gate, w_up, w_down)
        x = _combine(x, gt2, lat_row, ptok_x, ys[0])
        if not last:
            ctx = _combine(ctx, gt2, ctx_row, ptok_c, ys[1])
    return x
```

```python
import functools
import math

import jax
import jax.numpy as jnp
from jax import lax
from jax.experimental import pallas as pl
from jax.experimental.pallas import tpu as pltpu

F32 = jnp.float32
BF16 = jnp.bfloat16

D_MODEL = 1024
GRID_W = 64
GROUP_W = 256
ATTN_HEADS = 4
ATTN_D = 32
ATTN_VD = 64
ROPE_THETA = 10000.0
POOL_WINDOWS = (2, 4, 8, 16)
POOL_GW = 64
CONV_K = 31
SGU_CHUNK = 128
SGU_GROUPS = 4
SGU_GW = 64
N_EXPERTS = 16
EC_CAPACITY = 2
D_EXPERT = 2048
N_MOD = 6
IN_W = 2048
EPS = 1e-6
LOG2E = 1.4426950408889634
MAX_UNSTABILISED_SCORE = 60.0
SCORE_BOUND_SLACK = 1.02

MOD_ROWS = 16
LANES = 128
SUBLANES = 8
HALO = 16
TIME_CHUNK = 256
VMEM_LIMIT = 56 * 1024 * 1024


def _params(sem):
    return pltpu.CompilerParams(dimension_semantics=sem, vmem_limit_bytes=VMEM_LIMIT)


def _dot(a, b):
    return jnp.dot(a, b, preferred_element_type=F32)


def _dot_t(a, b):
    return lax.dot_general(a, b, (((1,), (1,)), ((), ())), preferred_element_type=F32)


def _split_dot(a, w):
    hi = a.astype(BF16)
    lo = (a - hi.astype(F32)).astype(BF16)
    return _dot(hi, w) + _dot(lo, w)


def _silu(x):
    return x * jax.nn.sigmoid(x)


def _layer_norm(x, g, b):
    mu = jnp.mean(x, axis=-1, keepdims=True)
    xc = x - mu
    var = jnp.mean(xc * xc, axis=-1, keepdims=True)
    return xc * lax.rsqrt(var + EPS) * g + b


def _mod_kernel(c_ref, w_ref, b_ref, o_ref):
    s = _silu(c_ref[...])
    o_ref[...] = _dot(s.astype(BF16), w_ref[...].astype(BF16)) + b_ref[...]


def _modulation(cond, w_mod, b_mod):
    L, D, N = w_mod.shape
    tn = 1024
    return pl.pallas_call(
        _mod_kernel,
        grid=(L, N // tn),
        in_specs=[
            pl.BlockSpec((MOD_ROWS, D), lambda l, j: (0, 0)),
            pl.BlockSpec((None, D, tn), lambda l, j: (l, 0, j)),
            pl.BlockSpec((None, 1, tn), lambda l, j: (l, 0, j)),
        ],
        out_specs=pl.BlockSpec((None, MOD_ROWS, tn), lambda l, j: (l, 0, j)),
        out_shape=jax.ShapeDtypeStruct((L, MOD_ROWS, N), F32),
        compiler_params=_params(("arbitrary", "arbitrary")),
        name="modulation",
    )(cond, w_mod, b_mod.reshape(L, 1, N))


def _inproj_kernel(x_ref, g_ref, sc_ref, sh_ref, w_ref, gq_ref, gk_ref, cos_ref, sin_ref, seg_ref,
                   q_ref, k_ref, v_ref, r_ref, *, sub):
    gain = g_ref[...] * (1.0 + sc_ref[...])
    seg = seg_ref[...]
    first = (lax.broadcasted_iota(jnp.int32, (sub, GROUP_W), 1) & 15) < 8

    for r in range(x_ref.shape[0] // sub):
        rows = slice(r * sub, (r + 1) * sub)
        x = x_ref[rows, :]
        h = x * lax.rsqrt(jnp.mean(x * x, axis=-1, keepdims=True) + EPS) * gain + sh_ref[...]
        p = _dot(h.astype(BF16), w_ref[...])
        cos = cos_ref[rows, :]
        sin = sin_ref[rows, :]

        def prep(a, g):
            ss = _split_dot(a * a, seg)
            n = a * lax.rsqrt(ss * (1.0 / ATTN_D) + EPS) * g
            partner = jnp.where(first, pltpu.roll(n, GROUP_W - 8, 1), pltpu.roll(n, 8, 1))
            return n * cos + partner * sin

        q_ref[rows, :] = (prep(p[:, 0:GROUP_W], gq_ref[...]) * (ATTN_D ** -0.5 * LOG2E)).astype(BF16)
        k_ref[:, rows] = jnp.transpose(prep(p[:, GROUP_W:2 * GROUP_W], gk_ref[...])).astype(BF16)
        v_ref[rows, :] = p[:, 2 * GROUP_W:3 * GROUP_W].astype(BF16)
        r_ref[rows, :] = p[:, 3 * GROUP_W:]


def _inproj(x, mod_row, g1, sc, sh, w_bf, gq_t, gk_t, cos_t, sin_t, seg32, tm, sub=256):
    B, T, D = x.shape
    tm = min(tm, T)
    rest_w = IN_W - 3 * GROUP_W
    row = lambda b, i: (mod_row(b), 0, 0)
    const2 = lambda b, i: (0, 0)
    tok = lambda b, i: (b, i, 0)
    return pl.pallas_call(
        functools.partial(_inproj_kernel, sub=min(sub, tm)),
        grid=(B, T // tm),
        in_specs=[
            pl.BlockSpec((None, tm, D), tok),
            pl.BlockSpec((1, D), const2),
            pl.BlockSpec((None, 1, D), row),
            pl.BlockSpec((None, 1, D), row),
            pl.BlockSpec((D, IN_W), const2),
            pl.BlockSpec((1, GROUP_W), const2),
            pl.BlockSpec((1, GROUP_W), const2),
            pl.BlockSpec((tm, GROUP_W), lambda b, i: (i, 0)),
            pl.BlockSpec((tm, GROUP_W), lambda b, i: (i, 0)),
            pl.BlockSpec((GROUP_W, GROUP_W), const2),
        ],
        out_specs=[
            pl.BlockSpec((None, tm, GROUP_W), tok),
            pl.BlockSpec((None, GROUP_W, tm), lambda b, i: (b, 0, i)),
            pl.BlockSpec((None, tm, GROUP_W), tok),
            pl.BlockSpec((None, tm, rest_w), tok),
        ],
        out_shape=[
            jax.ShapeDtypeStruct((B, T, GROUP_W), BF16),
            jax.ShapeDtypeStruct((B, GROUP_W, T), BF16),
            jax.ShapeDtypeStruct((B, T, GROUP_W), BF16),
            jax.ShapeDtypeStruct((B, T, rest_w), F32),
        ],
        compiler_params=_params(("arbitrary", "arbitrary")),
        name="inproj",
    )(x, g1, sc, sh, w_bf, gq_t, gk_t, cos_t, sin_t, seg32)


def _attn_kernel(*refs, n_seg, lam_init):
    q_ref = refs[0]
    kv_refs = refs[1:1 + 2 * n_seg]
    lq1_ref, lk1_ref, lq2_ref, lk2_ref, go_ref, gq_ref, gk_ref, o_ref = refs[1 + 2 * n_seg:]
    lam = (jnp.exp(jnp.sum(lq1_ref[...] * lk1_ref[...], axis=-1, keepdims=True))
           - jnp.exp(jnp.sum(lq2_ref[...] * lk2_ref[...], axis=-1, keepdims=True)) + lam_init)
    q = q_ref[...]

    def scores(off):
        qs = q[:, off:off + ATTN_D]
        return [_dot(qs, kv_refs[2 * i][off:off + ATTN_D, :]) for i in range(n_seg)]

    def run(stabilise):
        def softmax_parts(s):
            if stabilise:
                mx = functools.reduce(jnp.maximum, [jnp.max(si, axis=-1, keepdims=True) for si in s])
                s = [si - mx for si in s]
            p = [jnp.exp2(si) for si in s]
            den = functools.reduce(jnp.add, [jnp.sum(pi, axis=-1, keepdims=True) for pi in p])
            return [pi.astype(BF16) for pi in p], den

        heads = []
        nxt = (scores(0), scores(ATTN_D))
        for h in range(ATTN_HEADS):
            cur = nxt
            if h + 1 < ATTN_HEADS:
                nxt = (scores((h + 1) * ATTN_VD), scores((h + 1) * ATTN_VD + ATTN_D))
            p1, l1 = softmax_parts(cur[0])
            p2, l2 = softmax_parts(cur[1])
            c = (lam * l1 / l2).astype(BF16)
            o = None
            for i in range(n_seg):
                oi = _dot(p1[i] - c * p2[i], kv_refs[2 * i + 1][:, h * ATTN_VD:(h + 1) * ATTN_VD])
                o = oi if o is None else o + oi
            o = o * (1.0 / l1)
            ms = jnp.mean(o * o, axis=-1, keepdims=True)
            heads.append(o * lax.rsqrt(ms + EPS))
        o_ref[...] = (jnp.concatenate(heads, axis=1) * go_ref[...] * (1.0 - lam_init)).astype(BF16)

    bound = (jnp.max(jnp.abs(gq_ref[...])) * jnp.max(jnp.abs(gk_ref[...]))
             * (ATTN_D * ATTN_D ** -0.5 * LOG2E * SCORE_BOUND_SLACK))
    small = bound < MAX_UNSTABILISED_SCORE

    @pl.when(small)
    def _():
        run(False)

    @pl.when(jnp.logical_not(small))
    def _():
        run(True)


def _attention(q, kvs, lams, go_t, gq_t, gk_t, lam_init, tq):
    B, T, _ = q.shape
    tok = lambda b, i: (b, i, 0)
    const2 = lambda b, i: (0, 0)
    in_specs = [pl.BlockSpec((None, tq, GROUP_W), tok)]
    args = [q]
    for kt, v in kvs:
        n = v.shape[1]
        in_specs += [pl.BlockSpec((None, GROUP_W, n), lambda b, i: (b, 0, 0)),
                     pl.BlockSpec((None, n, GROUP_W), lambda b, i: (b, 0, 0))]
        args += [kt, v]
    in_specs += [pl.BlockSpec((1, ATTN_D), const2)] * 4
    in_specs += [pl.BlockSpec((1, GROUP_W), const2)] * 3
    args += list(lams) + [go_t, gq_t, gk_t]
    return pl.pallas_call(
        functools.partial(_attn_kernel, n_seg=len(kvs), lam_init=lam_init),
        grid=(B, T // tq),
        in_specs=in_specs,
        out_specs=pl.BlockSpec((None, tq, GROUP_W), tok),
        out_shape=jax.ShapeDtypeStruct((B, T, GROUP_W), BF16),
        compiler_params=_params(("arbitrary", "arbitrary")),
        name="diff_attention",
    )(*args)


def _gelu_tanh(x):
    return 0.5 * x * (1.0 + jnp.tanh(0.7978845608028654 * (x + 0.044715 * (x * x * x))))


def _mix_kernel(cur_ref, prev_ref, next_ref, wpool_ref, bpool_ref, pscale_ref, convw_ref, convb_ref, clng_ref,
                clnb_ref, wpw2_ref, slng_ref, slnb_ref, wsp_ref, bsp_ref, o_ref, zwin, ywin, *, T):
    i = pl.program_id(1)
    has_prev = jnp.where(i > 0, 1.0, 0.0)
    has_next = jnp.where(i < pl.num_programs(1) - 1, 1.0, 0.0)

    def glu(ref):
        return ref[:, GROUP_W:2 * GROUP_W] * jax.nn.sigmoid(ref[:, 2 * GROUP_W:3 * GROUP_W])

    win_rows = TIME_CHUNK + 2 * HALO

    def fill(win, head, body, tail):
        win[0, 0:HALO, :] = head * has_prev
        win[0, HALO:HALO + TIME_CHUNK, :] = body
        win[0, HALO + TIME_CHUNK:, :] = tail * has_next
        whole = win[0]
        for r in range(1, SUBLANES):
            win[r] = pltpu.roll(whole, win_rows - r, 0)

    def shifted(win, row):
        base = row - row % SUBLANES
        return win[row % SUBLANES, base:base + TIME_CHUNK, :]

    fill(zwin, prev_ref[:, 0:GROUP_W], cur_ref[:, 0:GROUP_W], next_ref[:, 0:GROUP_W])
    fill(ywin, glu(prev_ref), glu(cur_ref), glu(next_ref))

    def zs(off):
        return shifted(zwin, HALO + off)

    t = i * TIME_CHUNK + lax.broadcasted_iota(jnp.int32, (TIME_CHUNK, 1), 0)
    pool_group = lax.broadcasted_iota(jnp.int32, (TIME_CHUNK, GROUP_W), 1) >> 6
    z0 = zs(0)
    sums = []
    acc = None
    for w in POOL_WINDOWS:
        for off in range(-(w // 2), w // 2):
            if acc is None or not (-(w // 4) <= off < w // 4):
                acc = zs(off) if acc is None else acc + zs(off)
        sums.append(acc)
    mean = None
    for g, w in enumerate(POOL_WINDOWS):
        cnt = (jnp.minimum(t + w // 2, T) - jnp.maximum(t - w // 2, 0)).astype(F32)
        mg = sums[g] / cnt
        mean = mg if mean is None else jnp.where(pool_group == g, mg, mean)
    pool = (_dot((mean - z0).astype(BF16), wpool_ref[...]) + bpool_ref[...]) * pscale_ref[...]
    o_ref[:, 0:GROUP_W] = pool.astype(BF16)

    acc = None
    for k in range(CONV_K):
        term = shifted(ywin, HALO + k - CONV_K // 2) * convw_ref[k:k + 1, :]
        acc = term if acc is None else acc + term
    cn = _silu(_layer_norm(acc + convb_ref[...], clng_ref[...], clnb_ref[...]))
    o_ref[:, GROUP_W:2 * GROUP_W] = _dot(cn.astype(BF16), wpw2_ref[...]).astype(BF16)

    sgu_masks = [((lax.broadcasted_iota(jnp.int32, (1, GROUP_W), 1) >> 6) == g).astype(F32) for g in range(SGU_GROUPS)]
    gl = _gelu_tanh(cur_ref[:, 3 * GROUP_W:5 * GROUP_W])
    u = gl[:, 0:GROUP_W]
    vn = _layer_norm(gl[:, GROUP_W:2 * GROUP_W], slng_ref[...], slnb_ref[...])
    for j in range(TIME_CHUNK // SGU_CHUNK):
        rows = slice(j * SGU_CHUNK, (j + 1) * SGU_CHUNK)
        s = bsp_ref[...]
        for g in range(SGU_GROUPS):
            s = s + _dot(wsp_ref[g], (vn[rows] * sgu_masks[g]).astype(BF16))
        o_ref[rows, 2 * GROUP_W:3 * GROUP_W] = (u[rows] * s).astype(BF16)


def _mixers(rest, wpool_bd, bpool, pscale, convw, convb, clng, clnb, wpw2, slng, slnb, wsp, bsp):
    B, T, RW = rest.shape
    n_chunk = T // TIME_CHUNK
    per_chunk = TIME_CHUNK // HALO
    const2 = lambda b, i: (0, 0)
    vec = pl.BlockSpec((1, GROUP_W), const2)
    mat = pl.BlockSpec((GROUP_W, GROUP_W), const2)
    return pl.pallas_call(
        functools.partial(_mix_kernel, T=T),
        grid=(B, n_chunk),
        in_specs=[
            pl.BlockSpec((None, TIME_CHUNK, RW), lambda b, i: (b, i, 0)),
            pl.BlockSpec((None, HALO, RW), lambda b, i: (b, jnp.maximum(i * per_chunk - 1, 0), 0)),
            pl.BlockSpec((None, HALO, RW), lambda b, i: (b, jnp.minimum((i + 1) * per_chunk, n_chunk * per_chunk - 1), 0)),
            mat, vec, vec,
            pl.BlockSpec((CONV_K, GROUP_W), const2), vec, vec, vec, mat,
            vec, vec,
            pl.BlockSpec((SGU_GROUPS, SGU_CHUNK, SGU_CHUNK), lambda b, i: (0, 0, 0)),
            pl.BlockSpec((SGU_CHUNK, GROUP_W), const2),
        ],
        out_specs=pl.BlockSpec((None, TIME_CHUNK, 3 * GROUP_W), lambda b, i: (b, i, 0)),
        out_shape=jax.ShapeDtypeStruct((B, T, 3 * GROUP_W), BF16),
        scratch_shapes=[pltpu.VMEM((SUBLANES, TIME_CHUNK + 2 * HALO, GROUP_W), F32)] * 2,
        compiler_params=_params(("arbitrary", "arbitrary")),
        name="mixers",
    )(rest, rest, rest, wpool_bd, bpool, pscale, convw, convb, clng, clnb, wpw2, slng, slnb, wsp, bsp)


def _outproj_kernel(x_ref, a_ref, m_ref, w_ref, gt_ref, g2_ref, sc_ref, sh_ref, wr_ref,
                    xo_ref, h_ref, aff_ref, *, sub):
    gain = g2_ref[...] * (1.0 + sc_ref[...])
    for r in range(x_ref.shape[0] // sub):
        rows = slice(r * sub, (r + 1) * sub)
        mix = _dot(a_ref[rows, :], w_ref[0:GROUP_W, :]) + _dot(m_ref[rows, :], w_ref[GROUP_W:, :])
        xn = x_ref[rows, :] + gt_ref[...] * mix
        xo_ref[rows, :] = xn
        h = xn * lax.rsqrt(jnp.mean(xn * xn, axis=-1, keepdims=True) + EPS) * gain + sh_ref[...]
        hi = h.astype(BF16)
        h_ref[rows, :] = hi
        lo = (h - hi.astype(F32)).astype(BF16)
        d = hi.shape[1]
        logits = _dot(hi, wr_ref[0:d, :]) + _dot(hi, wr_ref[d:2 * d, :]) + _dot(lo, wr_ref[0:d, :])
        lt = jnp.transpose(logits)[0:N_EXPERTS, :]
        e = jnp.exp(lt - jnp.max(lt, axis=0, keepdims=True))
        aff_ref[:, rows] = e / jnp.sum(e, axis=0, keepdims=True)


def _outproj(x, attn, mixr, mod_row, w_bf, gt, g2, sc, sh, wr_cat, tm, sub=256):
    B, T, D = x.shape
    row = lambda b, i: (mod_row(b), 0, 0)
    const2 = lambda b, i: (0, 0)
    tok = lambda b, i: (b, i, 0)
    return pl.pallas_call(
        functools.partial(_outproj_kernel, sub=min(sub, tm)),
        grid=(B, T // tm),
        in_specs=[
            pl.BlockSpec((None, tm, D), tok),
            pl.BlockSpec((None, tm, GROUP_W), tok),
            pl.BlockSpec((None, tm, 3 * GROUP_W), tok),
            pl.BlockSpec((D, D), const2),
            pl.BlockSpec((None, 1, D), row),
            pl.BlockSpec((1, D), const2),
            pl.BlockSpec((None, 1, D), row),
            pl.BlockSpec((None, 1, D), row),
            pl.BlockSpec((2 * D, LANES), const2),
        ],
        out_specs=[
            pl.BlockSpec((None, tm, D), tok),
            pl.BlockSpec((None, tm, D), tok),
            pl.BlockSpec((None, N_EXPERTS, tm), lambda b, i: (b, 0, i)),
        ],
        out_shape=[
            jax.ShapeDtypeStruct((B, T, D), F32),
            jax.ShapeDtypeStruct((B, T, D), BF16),
            jax.ShapeDtypeStruct((B, N_EXPERTS, T), F32),
        ],
        compiler_params=_params(("arbitrary", "arbitrary")),
        name="outproj",
    )(x, attn, mixr, w_bf, gt, g2, sc, sh, wr_cat)


def _lane_prefix(x, tri):
    outs = []
    off = jnp.zeros((x.shape[0], 1), F32)
    for j in range(x.shape[1] // LANES):
        xb = x[:, j * LANES:(j + 1) * LANES]
        inc = _dot(xb.astype(BF16), tri)
        outs.append(inc - xb + off)
        off = off + inc[:, LANES - 1:LANES]
    return jnp.concatenate(outs, axis=1)


def _route_kernel(a_ref, tri_ref, pos_ref, gate_ref, ptok_ref, *, cap, n_expert):
    a = a_ref[...]
    R, T = a.shape

    def enough(c):
        return jnp.sum(jnp.where(a >= c, 1.0, 0.0), axis=-1, keepdims=True) >= cap

    def bit_search(i, lo):
        cand = lo | lax.shift_left(jnp.int32(1), 30 - i)
        return jnp.where(enough(pltpu.bitcast(cand, F32)), cand, lo)

    lo_bits = lax.fori_loop(0, 31, bit_search, jnp.zeros((R, 1), jnp.int32))

    def refine(i, c):
        lo, hi = c
        mid = lo + 0.5 * (hi - lo)
        ok = enough(mid)
        return jnp.where(ok, mid, lo), jnp.where(ok, hi, mid)

    thr, _ = lax.fori_loop(0, 30, refine, (pltpu.bitcast(lo_bits, F32), pltpu.bitcast(lo_bits + 1, F32)))
    gt = a > thr
    eq = a == thr
    need = cap - jnp.sum(jnp.where(gt, 1.0, 0.0), axis=-1, keepdims=True)
    tri = tri_ref[...]
    tie_rank = _lane_prefix(jnp.where(eq, 1.0, 0.0), tri)
    sel = jnp.logical_or(gt, jnp.logical_and(eq, tie_rank < need))
    pos = jnp.where(sel, _lane_prefix(jnp.where(sel, 1.0, 0.0), tri), -1.0)
    pos_ref[...] = pos
    gate_ref[...] = jnp.where(sel, a, 0.0)
    fill = jnp.full((LANES - n_expert, T), -1.0, F32)
    for b in range(R // n_expert):
        padded = jnp.concatenate([pos[b * n_expert:(b + 1) * n_expert], fill], axis=0)
        ptok_ref[b] = jnp.transpose(padded)


def _route(aff_t, tri, cap):
    B, E, T = aff_t.shape
    blk = pl.BlockSpec((B * E, T), lambda i: (0, 0))
    pos, gate, ptok = pl.pallas_call(
        functools.partial(_route_kernel, cap=cap, n_expert=E),
        grid=(1,),
        in_specs=[blk, pl.BlockSpec((LANES, LANES), lambda i: (0, 0))],
        out_specs=[blk, blk, pl.BlockSpec((B, T, LANES), lambda i: (0, 0, 0))],
        out_shape=[
            jax.ShapeDtypeStruct((B * E, T), F32),
            jax.ShapeDtypeStruct((B * E, T), F32),
            jax.ShapeDtypeStruct((B, T, LANES), F32),
        ],
        compiler_params=_params(("arbitrary",)),
        name="route",
    )(aff_t.reshape(B * E, T), tri)
    return pos.reshape(B, E, T), gate.reshape(B, E, T), ptok


def _gather_kernel(pos_ref, gate_ref, h_ref, xs_ref, gs_ref, *, cap):
    n_exp = xs_ref.shape[0]
    first = pl.program_id(1) * n_exp
    T = pos_ref.shape[1]
    slot = lax.broadcasted_iota(jnp.int32, (cap, T), 0).astype(F32)
    for j in range(n_exp):
        prow = pos_ref[pl.ds(first + j, 1), :]
        grow = gate_ref[pl.ds(first + j, 1), :]
        hit = slot == prow
        onehot = jnp.where(hit, 1.0, 0.0).astype(BF16)
        xs_ref[j] = _dot(onehot, h_ref[...]).astype(BF16)
        gs_ref[j] = jnp.sum(jnp.where(hit, grow, 0.0), axis=-1, keepdims=True)


def _gather(pos_t, gate_t, h, n_exp):
    B, E, T = pos_t.shape
    D = h.shape[-1]
    cap = EC_CAPACITY * T // E
    et = pl.BlockSpec((None, E, T), lambda b, e: (b, 0, 0))
    return pl.pallas_call(
        functools.partial(_gather_kernel, cap=cap),
        grid=(B, E // n_exp),
        in_specs=[et, et, pl.BlockSpec((None, T, D), lambda b, e: (b, 0, 0))],
        out_specs=[
            pl.BlockSpec((n_exp, cap, D), lambda b, e: (e, b, 0)),
            pl.BlockSpec((n_exp, cap, 1), lambda b, e: (e, b, 0)),
        ],
        out_shape=[
            jax.ShapeDtypeStruct((E, B * cap, D), BF16),
            jax.ShapeDtypeStruct((E, B * cap, 1), F32),
        ],
        compiler_params=_params(("arbitrary", "arbitrary")),
        name="gather",
    )(pos_t, gate_t, h)


def _moe_kernel(*refs, n_grp, tm):
    xs = refs[0:2 * n_grp:2]
    gs = refs[1:2 * n_grp:2]
    wg_ref, wu_ref, wd_ref = refs[2 * n_grp:2 * n_grp + 3]
    ys = refs[2 * n_grp + 3:3 * n_grp + 3]
    accs = refs[3 * n_grp + 3:4 * n_grp + 3]
    wgb, wub, wdb = refs[4 * n_grp + 3:]
    f = pl.program_id(1)
    nf = pl.num_programs(1)
    @pl.when(f == 0)
    def _():
        for acc in accs:
            acc[...] = jnp.zeros_like(acc)

    wgb[...] = wg_ref[...].astype(BF16)
    wub[...] = wu_ref[...].astype(BF16)
    wdb[...] = wd_ref[...].astype(BF16)

    for x_ref, acc in zip(xs, accs):
        rows = x_ref.shape[0]
        t = min(tm, rows)
        for r in range(rows // t):
            sl = slice(r * t, (r + 1) * t)
            xt = x_ref[sl, :]
            a = _dot(xt, wgb[...])
            u = _dot(xt, wub[...])
            acc[sl, :] += _dot((_silu(a) * u).astype(BF16), wdb[...])

    @pl.when(f == nf - 1)
    def _():
        for y_ref, acc, g_ref in zip(ys, accs, gs):
            y_ref[...] = (acc[...] * g_ref[...]).astype(BF16)


def _moe(groups, layer, w_gate, w_up, w_down, tf=512, tm=512):
    _, E, D, F = w_gate.shape
    in_specs, args, out_specs, out_shape, scratch = [], [], [], [], []
    for xs, gs in groups:
        R = xs.shape[1]
        in_specs += [pl.BlockSpec((None, R, D), lambda e, f: (e, 0, 0)), pl.BlockSpec((None, R, 1), lambda e, f: (e, 0, 0))]
        args += [xs, gs]
        out_specs.append(pl.BlockSpec((None, R, D), lambda e, f: (e, 0, 0)))
        out_shape.append(jax.ShapeDtypeStruct((E, R, D), BF16))
        scratch.append(pltpu.VMEM((R, D), F32))
    in_specs += [
        pl.BlockSpec((None, None, D, tf), lambda e, f: (layer, e, 0, f)),
        pl.BlockSpec((None, None, D, tf), lambda e, f: (layer, e, 0, f)),
        pl.BlockSpec((None, None, tf, D), lambda e, f: (layer, e, f, 0)),
    ]
    args += [w_gate, w_up, w_down]
    scratch += [pltpu.VMEM((D, tf), BF16), pltpu.VMEM((D, tf), BF16), pltpu.VMEM((tf, D), BF16)]
    return pl.pallas_call(
        functools.partial(_moe_kernel, n_grp=len(groups), tm=tm),
        grid=(E, F // tf),
        in_specs=in_specs,
        out_specs=out_specs,
        out_shape=out_shape,
        scratch_shapes=scratch,
        compiler_params=_params(("arbitrary", "arbitrary")),
        name="expert_ffn",
    )(*args)


def _combine_kernel(x_ref, gt_ref, ptok_ref, ys_ref, o_ref, *, tm):
    T, D = x_ref.shape
    E, cap, _ = ys_ref.shape
    t = min(tm, T)
    lane = lax.broadcasted_iota(jnp.int32, (t, cap), 1).astype(F32)

    def tile(r, carry):
        sl = pl.ds(pl.multiple_of(r * t, t), t)
        pt = ptok_ref[sl, :]
        acc = jnp.zeros((t, D), F32)
        for e in range(E):
            onehot = jnp.where(pt[:, e:e + 1] == lane, 1.0, 0.0).astype(BF16)
            acc = acc + _dot(onehot, ys_ref[e])
        o_ref[sl, :] = x_ref[sl, :] + gt_ref[...] * acc
        return carry

    lax.fori_loop(0, T // t, tile, 0)


def _combine(x, gt, mod_row, ptok, ys, tm=256):
    B, T, D = x.shape
    E = ys.shape[0]
    cap = ys.shape[1] // B
    ys4 = ys.reshape(E, B, cap, D)
    return pl.pallas_call(
        functools.partial(_combine_kernel, tm=tm),
        grid=(B,),
        in_specs=[
            pl.BlockSpec((None, T, D), lambda b: (b, 0, 0)),
            pl.BlockSpec((None, 1, D), lambda b: (mod_row(b), 0, 0)),
            pl.BlockSpec((None, T, LANES), lambda b: (b, 0, 0)),
            pl.BlockSpec((E, None, cap, D), lambda b: (0, b, 0, 0)),
        ],
        out_specs=pl.BlockSpec((None, T, D), lambda b: (b, 0, 0)),
        out_shape=jax.ShapeDtypeStruct((B, T, D), F32),
        compiler_params=_params(("arbitrary",)),
        name="combine",
    )(x, gt, ptok, ys4)


def _rope_tables(T):
    rows = (jnp.arange(T) // GRID_W).astype(F32)
    cols = (jnp.arange(T) % GRID_W).astype(F32)
    n_freq = ATTN_D // 4
    inv = ROPE_THETA ** (-jnp.arange(n_freq, dtype=F32) / n_freq)
    ang_r, ang_c = rows[:, None] * inv, cols[:, None] * inv
    cos = jnp.concatenate([jnp.cos(ang_r)] * 2 + [jnp.cos(ang_c)] * 2, axis=1)
    sin = jnp.concatenate([-jnp.sin(ang_r), jnp.sin(ang_r), -jnp.sin(ang_c), jnp.sin(ang_c)], axis=1)
    reps = GROUP_W // ATTN_D
    return jnp.tile(cos, (1, reps)).astype(F32), jnp.tile(sin, (1, reps)).astype(F32)


def _segment_ones(width):
    i = jnp.arange(GROUP_W) // width
    return (i[:, None] == i[None, :]).astype(BF16)


def _block_diag(w):
    G, n, _ = w.shape
    eye = jnp.eye(G, dtype=w.dtype)
    return (eye[:, None, :, None] * w[:, :, None, :]).reshape(G * n, G * n)


def kernel(x, c, ctx, c_ctx, w_mod, b_mod, g_norm1, g_norm2, w_in, w_out, g_q, g_k, lam_q1, lam_k1, lam_q2, lam_k2,
           g_attn_out, w_pool, b_pool, pool_scale, conv_w, conv_b, conv_ln_g, conv_ln_b, w_pw2, sgu_ln_g, sgu_ln_b,
           w_spatial, b_spatial, w_router, w_gate, w_up, w_down):
    B, T, D = x.shape
    C = ctx.shape[1]
    L = w_mod.shape[0]
    assert B < MOD_ROWS and D == D_MODEL and T % TIME_CHUNK == 0 and C % TIME_CHUNK == 0

    cond = jnp.zeros((MOD_ROWS, D), F32).at[:B].set(c).at[B].set(c_ctx)
    mods = _modulation(cond, w_mod, b_mod).reshape(L, MOD_ROWS, N_MOD, 1, D)
    lat_row = lambda b: b
    ctx_row = lambda b: B

    cos_x, sin_x = _rope_tables(T)
    cos_c, sin_c = jnp.ones((C, GROUP_W), F32), jnp.zeros((C, GROUP_W), F32)
    seg32 = _segment_ones(ATTN_D)
    tri = (jnp.arange(LANES)[:, None] <= jnp.arange(LANES)[None, :]).astype(BF16)
    tile_g = lambda g: jnp.tile(g, GROUP_W // g.shape[0]).reshape(1, GROUP_W)
    vec = lambda v: v.reshape(1, -1)

    for l in range(L):
        last = l == L - 1
        lam_init = 0.8 - 0.6 * math.exp(-0.3 * l)
        sh1, sc1, gt1, sh2, sc2, gt2 = (mods[l, :, i] for i in range(N_MOD))
        w_in_bf = w_in[l].astype(BF16)
        w_out_bf = w_out[l].astype(BF16)
        wr = jnp.zeros((D, LANES), F32).at[:, :N_EXPERTS].set(w_router[l])
        wr_hi = wr.astype(BF16)
        wr_cat = jnp.concatenate([wr_hi, (wr - wr_hi.astype(F32)).astype(BF16)], axis=0)
        lams = [vec(p[l]) for p in (lam_q1, lam_k1, lam_q2, lam_k2)]
        gq_t, gk_t, go_t = tile_g(g_q[l]), tile_g(g_k[l]), tile_g(g_attn_out[l])
        mix_w = (_block_diag(w_pool[l]).astype(BF16), vec(b_pool[l]), vec(pool_scale[l]), conv_w[l], vec(conv_b[l]),
                 vec(conv_ln_g[l]), vec(conv_ln_b[l]), w_pw2[l].astype(BF16), vec(sgu_ln_g[l]), vec(sgu_ln_b[l]),
                 w_spatial[l].astype(BF16), jnp.repeat(b_spatial[l].T, SGU_GW, axis=1))
        g1, g2 = vec(g_norm1[l]), vec(g_norm2[l])

        qx, kx, vx, rx = _inproj(x, lat_row, g1, sc1, sh1, w_in_bf, gq_t, gk_t, cos_x, sin_x, seg32, tm=1024, sub=512)
        qc, kc, vc, rc = _inproj(ctx, ctx_row, g1, sc1, sh1, w_in_bf, gq_t, gk_t, cos_c, sin_c, seg32, tm=256)

        attn_x = _attention(qx, [(kc, vc), (kx, vx)], lams, go_t, gq_t, gk_t, lam_init, tq=256)
        x, h2x, aff_x = _outproj(x, attn_x, _mixers(rx, *mix_w), lat_row, w_out_bf, gt1, g2, sc2, sh2, wr_cat, tm=512)
        pos_x, gate_x, ptok_x = _route(aff_x, tri, EC_CAPACITY * T // N_EXPERTS)
        groups = [_gather(pos_x, gate_x, h2x, n_exp=4)]

        if not last:
            attn_c = _attention(qc, [(kc, vc)], lams, go_t, gq_t, gk_t, lam_init, tq=256)
            ctx, h2c, aff_c = _outproj(ctx, attn_c, _mixers(rc, *mix_w), ctx_row, w_out_bf, gt1, g2, sc2, sh2,
                                       wr_cat, tm=256)
            pos_c, gate_c, ptok_c = _route(aff_c, tri, EC_CAPACITY * C // N_EXPERTS)
            groups.append(_gather(pos_c, gate_c, h2c, n_exp=N_EXPERTS))

        ys = _moe(groups, l, w_gate, w_up, w_down)
        x = _combine(x, gt2, lat_row, ptok_x, ys[0])
        if not last:
            ctx = _combine(ctx, gt2, ctx_row, ptok_c, ys[1])
    return x
```

```python
import functools
import math

import jax
import jax.numpy as jnp
from jax import lax
from jax.experimental import pallas as pl
from jax.experimental.pallas import tpu as pltpu

F32 = jnp.float32
BF16 = jnp.bfloat16

D_MODEL = 1024
GRID_W = 64
GROUP_W = 256
ATTN_HEADS = 4
ATTN_D = 32
ATTN_VD = 64
ROPE_THETA = 10000.0
POOL_WINDOWS = (2, 4, 8, 16)
POOL_GW = 64
CONV_K = 31
SGU_CHUNK = 128
SGU_GROUPS = 4
SGU_GW = 64
N_EXPERTS = 16
EC_CAPACITY = 2
D_EXPERT = 2048
N_MOD = 6
IN_W = 2048
EPS = 1e-6
LOG2E = 1.4426950408889634
MAX_UNSTABILISED_SCORE = 60.0
SCORE_BOUND_SLACK = 1.02
SUB_ULP_STEPS = 12

MOD_ROWS = 16
LANES = 128
SUBLANES = 8
HALO = 16
TIME_CHUNK = 256
VMEM_LIMIT = 56 * 1024 * 1024


def _params(sem):
    return pltpu.CompilerParams(dimension_semantics=sem, vmem_limit_bytes=VMEM_LIMIT)


def _dot(a, b):
    return jnp.dot(a, b, preferred_element_type=F32)


def _dot_t(a, b):
    return lax.dot_general(a, b, (((1,), (1,)), ((), ())), preferred_element_type=F32)


def _split_dot(a, w):
    hi = a.astype(BF16)
    lo = (a - hi.astype(F32)).astype(BF16)
    return _dot(hi, w) + _dot(lo, w)


def _silu(x):
    return x * jax.nn.sigmoid(x)


def _layer_norm(x, g, b):
    mu = jnp.mean(x, axis=-1, keepdims=True)
    xc = x - mu
    var = jnp.mean(xc * xc, axis=-1, keepdims=True)
    return xc * lax.rsqrt(var + EPS) * g + b


def _mod_kernel(c_ref, w_ref, b_ref, o_ref):
    s = _silu(c_ref[...])
    o_ref[...] = _dot(s.astype(BF16), w_ref[...].astype(BF16)) + b_ref[...]


def _modulation(cond, w_mod, b_mod):
    L, D, N = w_mod.shape
    tn = 1024
    return pl.pallas_call(
        _mod_kernel,
        grid=(L, N // tn),
        in_specs=[
            pl.BlockSpec((MOD_ROWS, D), lambda l, j: (0, 0)),
            pl.BlockSpec((None, D, tn), lambda l, j: (l, 0, j)),
            pl.BlockSpec((None, 1, tn), lambda l, j: (l, 0, j)),
        ],
        out_specs=pl.BlockSpec((None, MOD_ROWS, tn), lambda l, j: (l, 0, j)),
        out_shape=jax.ShapeDtypeStruct((L, MOD_ROWS, N), F32),
        compiler_params=_params(("arbitrary", "arbitrary")),
        name="modulation",
    )(cond, w_mod, b_mod.reshape(L, 1, N))


def _inproj_kernel(x_ref, g_ref, sc_ref, sh_ref, w_ref, gq_ref, gk_ref, cos_ref, sin_ref, seg_ref,
                   q_ref, k_ref, v_ref, r_ref, *, sub):
    gain = g_ref[...] * (1.0 + sc_ref[...])
    seg = seg_ref[...]
    first = (lax.broadcasted_iota(jnp.int32, (sub, GROUP_W), 1) & 15) < 8

    for r in range(x_ref.shape[0] // sub):
        rows = slice(r * sub, (r + 1) * sub)
        x = x_ref[rows, :]
        h = x * lax.rsqrt(jnp.mean(x * x, axis=-1, keepdims=True) + EPS) * gain + sh_ref[...]
        p = _dot(h.astype(BF16), w_ref[...])
        cos = cos_ref[rows, :]
        sin = sin_ref[rows, :]

        def prep(a, g):
            ss = _split_dot(a * a, seg)
            n = a * lax.rsqrt(ss * (1.0 / ATTN_D) + EPS) * g
            partner = jnp.where(first, pltpu.roll(n, GROUP_W - 8, 1), pltpu.roll(n, 8, 1))
            return n * cos + partner * sin

        q_ref[rows, :] = (prep(p[:, 0:GROUP_W], gq_ref[...]) * (ATTN_D ** -0.5 * LOG2E)).astype(BF16)
        k_ref[:, rows] = jnp.transpose(prep(p[:, GROUP_W:2 * GROUP_W], gk_ref[...])).astype(BF16)
        v_ref[rows, :] = p[:, 2 * GROUP_W:3 * GROUP_W].astype(BF16)
        r_ref[rows, :] = p[:, 3 * GROUP_W:]


def _inproj(x, mod_row, g1, sc, sh, w_bf, gq_t, gk_t, cos_t, sin_t, seg32, tm, sub=256):
    B, T, D = x.shape
    tm = min(tm, T)
    rest_w = IN_W - 3 * GROUP_W
    row = lambda b, i: (mod_row(b), 0, 0)
    const2 = lambda b, i: (0, 0)
    tok = lambda b, i: (b, i, 0)
    return pl.pallas_call(
        functools.partial(_inproj_kernel, sub=min(sub, tm)),
        grid=(B, T // tm),
        in_specs=[
            pl.BlockSpec((None, tm, D), tok),
            pl.BlockSpec((1, D), const2),
            pl.BlockSpec((None, 1, D), row),
            pl.BlockSpec((None, 1, D), row),
            pl.BlockSpec((D, IN_W), const2),
            pl.BlockSpec((1, GROUP_W), const2),
            pl.BlockSpec((1, GROUP_W), const2),
            pl.BlockSpec((tm, GROUP_W), lambda b, i: (i, 0)),
            pl.BlockSpec((tm, GROUP_W), lambda b, i: (i, 0)),
            pl.BlockSpec((GROUP_W, GROUP_W), const2),
        ],
        out_specs=[
            pl.BlockSpec((None, tm, GROUP_W), tok),
            pl.BlockSpec((None, GROUP_W, tm), lambda b, i: (b, 0, i)),
            pl.BlockSpec((None, tm, GROUP_W), tok),
            pl.BlockSpec((None, tm, rest_w), tok),
        ],
        out_shape=[
            jax.ShapeDtypeStruct((B, T, GROUP_W), BF16),
            jax.ShapeDtypeStruct((B, GROUP_W, T), BF16),
            jax.ShapeDtypeStruct((B, T, GROUP_W), BF16),
            jax.ShapeDtypeStruct((B, T, rest_w), F32),
        ],
        compiler_params=_params(("arbitrary", "arbitrary")),
        name="inproj",
    )(x, g1, sc, sh, w_bf, gq_t, gk_t, cos_t, sin_t, seg32)


def _attn_kernel(*refs, n_seg, lam_init):
    q_ref = refs[0]
    kv_refs = refs[1:1 + 2 * n_seg]
    lq1_ref, lk1_ref, lq2_ref, lk2_ref, go_ref, gq_ref, gk_ref, o_ref = refs[1 + 2 * n_seg:]
    lam = (jnp.exp(jnp.sum(lq1_ref[...] * lk1_ref[...], axis=-1, keepdims=True))
           - jnp.exp(jnp.sum(lq2_ref[...] * lk2_ref[...], axis=-1, keepdims=True)) + lam_init)
    q = q_ref[...]

    def scores(off):
        qs = q[:, off:off + ATTN_D]
        return [_dot(qs, kv_refs[2 * i][off:off + ATTN_D, :]) for i in range(n_seg)]

    def run(stabilise):
        def softmax_parts(s):
            if stabilise:
                mx = functools.reduce(jnp.maximum, [jnp.max(si, axis=-1, keepdims=True) for si in s])
                s = [si - mx for si in s]
            p = [jnp.exp2(si) for si in s]
            den = functools.reduce(jnp.add, [jnp.sum(pi, axis=-1, keepdims=True) for pi in p])
            return [pi.astype(BF16) for pi in p], den

        heads = []
        nxt = (scores(0), scores(ATTN_D))
        for h in range(ATTN_HEADS):
            cur = nxt
            if h + 1 < ATTN_HEADS:
                nxt = (scores((h + 1) * ATTN_VD), scores((h + 1) * ATTN_VD + ATTN_D))
            p1, l1 = softmax_parts(cur[0])
            p2, l2 = softmax_parts(cur[1])
            c = (lam * l1 / l2).astype(BF16)
            o = None
            for i in range(n_seg):
                oi = _dot(p1[i] - c * p2[i], kv_refs[2 * i + 1][:, h * ATTN_VD:(h + 1) * ATTN_VD])
                o = oi if o is None else o + oi
            o = o * (1.0 / l1)
            ms = jnp.mean(o * o, axis=-1, keepdims=True)
            heads.append(o * lax.rsqrt(ms + EPS))
        o_ref[...] = (jnp.concatenate(heads, axis=1) * go_ref[...] * (1.0 - lam_init)).astype(BF16)

    bound = (jnp.max(jnp.abs(gq_ref[...])) * jnp.max(jnp.abs(gk_ref[...]))
             * (ATTN_D * ATTN_D ** -0.5 * LOG2E * SCORE_BOUND_SLACK))
    small = bound < MAX_UNSTABILISED_SCORE

    @pl.when(small)
    def _():
        run(False)

    @pl.when(jnp.logical_not(small))
    def _():
        run(True)


def _attention(q, kvs, lams, go_t, gq_t, gk_t, lam_init, tq):
    B, T, _ = q.shape
    tok = lambda b, i: (b, i, 0)
    const2 = lambda b, i: (0, 0)
    in_specs = [pl.BlockSpec((None, tq, GROUP_W), tok)]
    args = [q]
    for kt, v in kvs:
        n = v.shape[1]
        in_specs += [pl.BlockSpec((None, GROUP_W, n), lambda b, i: (b, 0, 0)),
                     pl.BlockSpec((None, n, GROUP_W), lambda b, i: (b, 0, 0))]
        args += [kt, v]
    in_specs += [pl.BlockSpec((1, ATTN_D), const2)] * 4
    in_specs += [pl.BlockSpec((1, GROUP_W), const2)] * 3
    args += list(lams) + [go_t, gq_t, gk_t]
    return pl.pallas_call(
        functools.partial(_attn_kernel, n_seg=len(kvs), lam_init=lam_init),
        grid=(B, T // tq),
        in_specs=in_specs,
        out_specs=pl.BlockSpec((None, tq, GROUP_W), tok),
        out_shape=jax.ShapeDtypeStruct((B, T, GROUP_W), BF16),
        compiler_params=_params(("arbitrary", "arbitrary")),
        name="diff_attention",
    )(*args)


def _gelu_tanh(x):
    return 0.5 * x * (1.0 + jnp.tanh(0.7978845608028654 * (x + 0.044715 * (x * x * x))))


def _mixout_kernel(cur_ref, prev_ref, next_ref, wpool_ref, bpool_ref, pscale_ref, convw_ref, convb_ref, clng_ref,
                   clnb_ref, wpw2_ref, slng_ref, slnb_ref, wsp_ref, bsp_ref,
                   x_ref, a_ref, w_ref, gt_ref, g2_ref, sc_ref, sh_ref, wr_ref,
                   xo_ref, h_ref, aff_ref, zwin, ywin, *, T):
    i = pl.program_id(1)
    has_prev = jnp.where(i > 0, 1.0, 0.0)
    has_next = jnp.where(i < pl.num_programs(1) - 1, 1.0, 0.0)

    def glu(ref):
        return ref[:, GROUP_W:2 * GROUP_W] * jax.nn.sigmoid(ref[:, 2 * GROUP_W:3 * GROUP_W])

    step = cur_ref.shape[0]
    win_rows = step + 2 * HALO

    def fill(win, head, body, tail):
        win[0, 0:HALO, :] = head * has_prev
        win[0, HALO:HALO + step, :] = body
        win[0, HALO + step:, :] = tail * has_next
        whole = win[0]
        for r in range(1, SUBLANES):
            win[r] = pltpu.roll(whole, win_rows - r, 0)

    def shifted(win, row):
        base = row - row % SUBLANES
        return win[row % SUBLANES, base:base + TIME_CHUNK, :]

    fill(zwin, prev_ref[:, 0:GROUP_W], cur_ref[:, 0:GROUP_W], next_ref[:, 0:GROUP_W])
    fill(ywin, glu(prev_ref), glu(cur_ref), glu(next_ref))

    pool_group = lax.broadcasted_iota(jnp.int32, (TIME_CHUNK, GROUP_W), 1) >> 6
    sgu_masks = [((lax.broadcasted_iota(jnp.int32, (1, GROUP_W), 1) >> 6) == g).astype(F32) for g in range(SGU_GROUPS)]
    gain = g2_ref[...] * (1.0 + sc_ref[...])

    for c in range(step // TIME_CHUNK):
        first = c * TIME_CHUNK
        rows = slice(first, first + TIME_CHUNK)

        def zs(off):
            return shifted(zwin, first + HALO + off)

        t = i * step + first + lax.broadcasted_iota(jnp.int32, (TIME_CHUNK, 1), 0)
        z0 = zs(0)
        sums = []
        acc = None
        for w in POOL_WINDOWS:
            for off in range(-(w // 2), w // 2):
                if acc is None or not (-(w // 4) <= off < w // 4):
                    acc = zs(off) if acc is None else acc + zs(off)
            sums.append(acc)
        mean = None
        for g, w in enumerate(POOL_WINDOWS):
            cnt = (jnp.minimum(t + w // 2, T) - jnp.maximum(t - w // 2, 0)).astype(F32)
            mg = sums[g] / cnt
            mean = mg if mean is None else jnp.where(pool_group == g, mg, mean)
        pool = (_dot((mean - z0).astype(BF16), wpool_ref[...]) + bpool_ref[...]) * pscale_ref[...]
        mix = _dot(a_ref[rows, :], w_ref[0:GROUP_W, :]) + _dot(pool.astype(BF16), w_ref[GROUP_W:2 * GROUP_W, :])

        acc = None
        for k in range(CONV_K):
            term = shifted(ywin, first + HALO + k - CONV_K // 2) * convw_ref[k:k + 1, :]
            acc = term if acc is None else acc + term
        cn = _silu(_layer_norm(acc + convb_ref[...], clng_ref[...], clnb_ref[...]))
        conv = _dot(cn.astype(BF16), wpw2_ref[...])
        mix = mix + _dot(conv.astype(BF16), w_ref[2 * GROUP_W:3 * GROUP_W, :])

        gl = _gelu_tanh(cur_ref[rows, 3 * GROUP_W:5 * GROUP_W])
        u = gl[:, 0:GROUP_W]
        vn = _layer_norm(gl[:, GROUP_W:2 * GROUP_W], slng_ref[...], slnb_ref[...])
        gated = []
        for j in range(TIME_CHUNK // SGU_CHUNK):
            sub = slice(j * SGU_CHUNK, (j + 1) * SGU_CHUNK)
            s = bsp_ref[...]
            for g in range(SGU_GROUPS):
                s = s + _dot(wsp_ref[g], (vn[sub] * sgu_masks[g]).astype(BF16))
            gated.append((u[sub] * s).astype(BF16))
        mix = mix + _dot(jnp.concatenate(gated, axis=0), w_ref[3 * GROUP_W:, :])

        xn = x_ref[rows, :] + gt_ref[...] * mix
        xo_ref[rows, :] = xn
        h = xn * lax.rsqrt(jnp.mean(xn * xn, axis=-1, keepdims=True) + EPS) * gain + sh_ref[...]
        hi = h.astype(BF16)
        h_ref[rows, :] = hi
        lo = (h - hi.astype(F32)).astype(BF16)
        d = hi.shape[1]
        logits = _dot(hi, wr_ref[0:d, :]) + _dot(hi, wr_ref[d:2 * d, :]) + _dot(lo, wr_ref[0:d, :])
        lt = jnp.transpose(logits)[0:N_EXPERTS, :]
        e = jnp.exp(lt - jnp.max(lt, axis=0, keepdims=True))
        aff_ref[:, rows] = e / jnp.sum(e, axis=0, keepdims=True)


def _mixout(x, attn, rest, mod_row, mix_w, w_bf, gt, g2, sc, sh, wr_cat, step=2 * TIME_CHUNK):
    B, T, D = x.shape
    RW = rest.shape[-1]
    step = min(step, T)
    n_step = T // step
    per_step = step // HALO
    const2 = lambda b, i: (0, 0)
    row = lambda b, i: (mod_row(b), 0, 0)
    tok = lambda b, i: (b, i, 0)
    vec = pl.BlockSpec((1, GROUP_W), const2)
    mat = pl.BlockSpec((GROUP_W, GROUP_W), const2)
    return pl.pallas_call(
        functools.partial(_mixout_kernel, T=T),
        grid=(B, n_step),
        in_specs=[
            pl.BlockSpec((None, step, RW), tok),
            pl.BlockSpec((None, HALO, RW), lambda b, i: (b, jnp.maximum(i * per_step - 1, 0), 0)),
            pl.BlockSpec((None, HALO, RW), lambda b, i: (b, jnp.minimum((i + 1) * per_step, n_step * per_step - 1), 0)),
            mat, vec, vec,
            pl.BlockSpec((CONV_K, GROUP_W), const2), vec, vec, vec, mat,
            vec, vec,
            pl.BlockSpec((SGU_GROUPS, SGU_CHUNK, SGU_CHUNK), lambda b, i: (0, 0, 0)),
            pl.BlockSpec((SGU_CHUNK, GROUP_W), const2),
            pl.BlockSpec((None, step, D), tok),
            pl.BlockSpec((None, step, GROUP_W), tok),
            pl.BlockSpec((D, D), const2),
            pl.BlockSpec((None, 1, D), row),
            pl.BlockSpec((1, D), const2),
            pl.BlockSpec((None, 1, D), row),
            pl.BlockSpec((None, 1, D), row),
            pl.BlockSpec((2 * D, LANES), const2),
        ],
        out_specs=[
            pl.BlockSpec((None, step, D), tok),
            pl.BlockSpec((None, step, D), tok),
            pl.BlockSpec((None, N_EXPERTS, step), lambda b, i: (b, 0, i)),
        ],
        out_shape=[
            jax.ShapeDtypeStruct((B, T, D), F32),
            jax.ShapeDtypeStruct((B, T, D), BF16),
            jax.ShapeDtypeStruct((B, N_EXPERTS, T), F32),
        ],
        scratch_shapes=[pltpu.VMEM((SUBLANES, step + 2 * HALO, GROUP_W), F32)] * 2,
        compiler_params=_params(("arbitrary", "arbitrary")),
        name="mixers_outproj",
    )(rest, rest, rest, *mix_w, x, attn, w_bf, gt, g2, sc, sh, wr_cat)


def _lane_prefix(x, tri):
    outs = []
    off = jnp.zeros((x.shape[0], 1), F32)
    for j in range(x.shape[1] // LANES):
        xb = x[:, j * LANES:(j + 1) * LANES]
        inc = _dot(xb.astype(BF16), tri)
        outs.append(inc - xb + off)
        off = off + inc[:, LANES - 1:LANES]
    return jnp.concatenate(outs, axis=1)


def _route_kernel(a_ref, tri_ref, pos_ref, gate_ref, ptok_ref, *, cap, n_expert):
    a = a_ref[...]
    R, T = a.shape

    def enough(c):
        return jnp.sum(jnp.where(a >= c, 1.0, 0.0), axis=-1, keepdims=True) >= cap

    def bit_search(i, lo):
        cand = lo | lax.shift_left(jnp.int32(1), 30 - i)
        return jnp.where(enough(pltpu.bitcast(cand, F32)), cand, lo)

    lo_bits = lax.fori_loop(0, 31, bit_search, jnp.zeros((R, 1), jnp.int32))

    def refine(i, c):
        lo, hi = c
        mid = lo + 0.5 * (hi - lo)
        ok = enough(mid)
        return jnp.where(ok, mid, lo), jnp.where(ok, hi, mid)

    thr, _ = lax.fori_loop(0, SUB_ULP_STEPS, refine,
                           (pltpu.bitcast(lo_bits, F32), pltpu.bitcast(lo_bits + 1, F32)))
    gt = a > thr
    eq = a == thr
    need = cap - jnp.sum(jnp.where(gt, 1.0, 0.0), axis=-1, keepdims=True)
    tri = tri_ref[...]
    tie_rank = _lane_prefix(jnp.where(eq, 1.0, 0.0), tri)
    sel = jnp.logical_or(gt, jnp.logical_and(eq, tie_rank < need))
    pos = jnp.where(sel, _lane_prefix(jnp.where(sel, 1.0, 0.0), tri), -1.0)
    pos_ref[...] = pos
    gate_ref[...] = jnp.where(sel, a, 0.0)
    fill = jnp.full((LANES - n_expert, T), -1.0, F32)
    for b in range(R // n_expert):
        padded = jnp.concatenate([pos[b * n_expert:(b + 1) * n_expert], fill], axis=0)
        ptok_ref[b] = jnp.transpose(padded)


def _route(aff_t, tri, cap):
    B, E, T = aff_t.shape
    blk = pl.BlockSpec((B * E, T), lambda i: (0, 0))
    pos, gate, ptok = pl.pallas_call(
        functools.partial(_route_kernel, cap=cap, n_expert=E),
        grid=(1,),
        in_specs=[blk, pl.BlockSpec((LANES, LANES), lambda i: (0, 0))],
        out_specs=[blk, blk, pl.BlockSpec((B, T, LANES), lambda i: (0, 0, 0))],
        out_shape=[
            jax.ShapeDtypeStruct((B * E, T), F32),
            jax.ShapeDtypeStruct((B * E, T), F32),
            jax.ShapeDtypeStruct((B, T, LANES), F32),
        ],
        compiler_params=_params(("arbitrary",)),
        name="route",
    )(aff_t.reshape(B * E, T), tri)
    return pos.reshape(B, E, T), gate.reshape(B, E, T), ptok


def _gather_kernel(pos_ref, gate_ref, h_ref, xs_ref, gs_ref, *, cap):
    n_exp = xs_ref.shape[0]
    first = pl.program_id(1) * n_exp
    T = pos_ref.shape[1]
    slot = lax.broadcasted_iota(jnp.int32, (cap, T), 0).astype(F32)
    for j in range(n_exp):
        prow = pos_ref[pl.ds(first + j, 1), :]
        grow = gate_ref[pl.ds(first + j, 1), :]
        hit = slot == prow
        onehot = jnp.where(hit, 1.0, 0.0).astype(BF16)
        xs_ref[j] = _dot(onehot, h_ref[...]).astype(BF16)
        gs_ref[j] = jnp.sum(jnp.where(hit, grow, 0.0), axis=-1, keepdims=True)


def _gather(pos_t, gate_t, h, n_exp):
    B, E, T = pos_t.shape
    D = h.shape[-1]
    cap = EC_CAPACITY * T // E
    et = pl.BlockSpec((None, E, T), lambda b, e: (b, 0, 0))
    return pl.pallas_call(
        functools.partial(_gather_kernel, cap=cap),
        grid=(B, E // n_exp),
        in_specs=[et, et, pl.BlockSpec((None, T, D), lambda b, e: (b, 0, 0))],
        out_specs=[
            pl.BlockSpec((n_exp, cap, D), lambda b, e: (e, b, 0)),
            pl.BlockSpec((n_exp, cap, 1), lambda b, e: (e, b, 0)),
        ],
        out_shape=[
            jax.ShapeDtypeStruct((E, B * cap, D), BF16),
            jax.ShapeDtypeStruct((E, B * cap, 1), F32),
        ],
        compiler_params=_params(("arbitrary", "arbitrary")),
        name="gather",
    )(pos_t, gate_t, h)


def _moe_kernel(*refs, n_grp, tm):
    xs = refs[0:2 * n_grp:2]
    gs = refs[1:2 * n_grp:2]
    wg_ref, wu_ref, wd_ref = refs[2 * n_grp:2 * n_grp + 3]
    ys = refs[2 * n_grp + 3:3 * n_grp + 3]
    accs = refs[3 * n_grp + 3:4 * n_grp + 3]
    wgb, wub, wdb = refs[4 * n_grp + 3:]
    f = pl.program_id(1)
    nf = pl.num_programs(1)
    @pl.when(f == 0)
    def _():
        for acc in accs:
            acc[...] = jnp.zeros_like(acc)

    wgb[...] = wg_ref[...].astype(BF16)
    wub[...] = wu_ref[...].astype(BF16)
    wdb[...] = wd_ref[...].astype(BF16)

    for x_ref, acc in zip(xs, accs):
        rows = x_ref.shape[0]
        t = min(tm, rows)
        for r in range(rows // t):
            sl = slice(r * t, (r + 1) * t)
            xt = x_ref[sl, :]
            a = _dot(xt, wgb[...])
            u = _dot(xt, wub[...])
            acc[sl, :] += _dot((_silu(a) * u).astype(BF16), wdb[...])

    @pl.when(f == nf - 1)
    def _():
        for y_ref, acc, g_ref in zip(ys, accs, gs):
            y_ref[...] = (acc[...] * g_ref[...]).astype(BF16)


def _moe(groups, layer, w_gate, w_up, w_down, tf=512, tm=512):
    _, E, D, F = w_gate.shape
    in_specs, args, out_specs, out_shape, scratch = [], [], [], [], []
    for xs, gs in groups:
        R = xs.shape[1]
        in_specs += [pl.BlockSpec((None, R, D), lambda e, f: (e, 0, 0)), pl.BlockSpec((None, R, 1), lambda e, f: (e, 0, 0))]
        args += [xs, gs]
        out_specs.append(pl.BlockSpec((None, R, D), lambda e, f: (e, 0, 0)))
        out_shape.append(jax.ShapeDtypeStruct((E, R, D), BF16))
        scratch.append(pltpu.VMEM((R, D), F32))
    in_specs += [
        pl.BlockSpec((None, None, D, tf), lambda e, f: (layer, e, 0, f)),
        pl.BlockSpec((None, None, D, tf), lambda e, f: (layer, e, 0, f)),
        pl.BlockSpec((None, None, tf, D), lambda e, f: (layer, e, f, 0)),
    ]
    args += [w_gate, w_up, w_down]
    scratch += [pltpu.VMEM((D, tf), BF16), pltpu.VMEM((D, tf), BF16), pltpu.VMEM((tf, D), BF16)]
    return pl.pallas_call(
        functools.partial(_moe_kernel, n_grp=len(groups), tm=tm),
        grid=(E, F // tf),
        in_specs=in_specs,
        out_specs=out_specs,
        out_shape=out_shape,
        scratch_shapes=scratch,
        compiler_params=_params(("arbitrary", "arbitrary")),
        name="expert_ffn",
    )(*args)


def _combine_kernel(x_ref, gt_ref, ptok_ref, ys_ref, o_ref, *, tm):
    T, D = x_ref.shape
    E, cap, _ = ys_ref.shape
    t = min(tm, T)
    lane = lax.broadcasted_iota(jnp.int32, (t, cap), 1).astype(F32)

    def tile(r, carry):
        sl = pl.ds(pl.multiple_of(r * t, t), t)
        pt = ptok_ref[sl, :]
        acc = jnp.zeros((t, D), F32)
        for e in range(E):
            onehot = jnp.where(pt[:, e:e + 1] == lane, 1.0, 0.0).astype(BF16)
            acc = acc + _dot(onehot, ys_ref[e])
        o_ref[sl, :] = x_ref[sl, :] + gt_ref[...] * acc
        return carry

    lax.fori_loop(0, T // t, tile, 0)


def _combine(x, gt, mod_row, ptok, ys, tm=256):
    B, T, D = x.shape
    E = ys.shape[0]
    cap = ys.shape[1] // B
    ys4 = ys.reshape(E, B, cap, D)
    return pl.pallas_call(
        functools.partial(_combine_kernel, tm=tm),
        grid=(B,),
        in_specs=[
            pl.BlockSpec((None, T, D), lambda b: (b, 0, 0)),
            pl.BlockSpec((None, 1, D), lambda b: (mod_row(b), 0, 0)),
            pl.BlockSpec((None, T, LANES), lambda b: (b, 0, 0)),
            pl.BlockSpec((E, None, cap, D), lambda b: (0, b, 0, 0)),
        ],
        out_specs=pl.BlockSpec((None, T, D), lambda b: (b, 0, 0)),
        out_shape=jax.ShapeDtypeStruct((B, T, D), F32),
        compiler_params=_params(("arbitrary",)),
        name="combine",
    )(x, gt, ptok, ys4)


def _rope_tables(T):
    rows = (jnp.arange(T) // GRID_W).astype(F32)
    cols = (jnp.arange(T) % GRID_W).astype(F32)
    n_freq = ATTN_D // 4
    inv = ROPE_THETA ** (-jnp.arange(n_freq, dtype=F32) / n_freq)
    ang_r, ang_c = rows[:, None] * inv, cols[:, None] * inv
    cos = jnp.concatenate([jnp.cos(ang_r)] * 2 + [jnp.cos(ang_c)] * 2, axis=1)
    sin = jnp.concatenate([-jnp.sin(ang_r), jnp.sin(ang_r), -jnp.sin(ang_c), jnp.sin(ang_c)], axis=1)
    reps = GROUP_W // ATTN_D
    return jnp.tile(cos, (1, reps)).astype(F32), jnp.tile(sin, (1, reps)).astype(F32)


def _segment_ones(width):
    i = jnp.arange(GROUP_W) // width
    return (i[:, None] == i[None, :]).astype(BF16)


def _block_diag(w):
    G, n, _ = w.shape
    eye = jnp.eye(G, dtype=w.dtype)
    return (eye[:, None, :, None] * w[:, :, None, :]).reshape(G * n, G * n)


def kernel(x, c, ctx, c_ctx, w_mod, b_mod, g_norm1, g_norm2, w_in, w_out, g_q, g_k, lam_q1, lam_k1, lam_q2, lam_k2,
           g_attn_out, w_pool, b_pool, pool_scale, conv_w, conv_b, conv_ln_g, conv_ln_b, w_pw2, sgu_ln_g, sgu_ln_b,
           w_spatial, b_spatial, w_router, w_gate, w_up, w_down):
    B, T, D = x.shape
    C = ctx.shape[1]
    L = w_mod.shape[0]
    assert B < MOD_ROWS and D == D_MODEL and T % TIME_CHUNK == 0 and C % TIME_CHUNK == 0

    cond = jnp.zeros((MOD_ROWS, D), F32).at[:B].set(c).at[B].set(c_ctx)
    mods = _modulation(cond, w_mod, b_mod).reshape(L, MOD_ROWS, N_MOD, 1, D)
    lat_row = lambda b: b
    ctx_row = lambda b: B

    cos_x, sin_x = _rope_tables(T)
    cos_c, sin_c = jnp.ones((C, GROUP_W), F32), jnp.zeros((C, GROUP_W), F32)
    seg32 = _segment_ones(ATTN_D)
    tri = (jnp.arange(LANES)[:, None] <= jnp.arange(LANES)[None, :]).astype(BF16)
    tile_g = lambda g: jnp.tile(g, GROUP_W // g.shape[0]).reshape(1, GROUP_W)
    vec = lambda v: v.reshape(1, -1)

    for l in range(L):
        last = l == L - 1
        lam_init = 0.8 - 0.6 * math.exp(-0.3 * l)
        sh1, sc1, gt1, sh2, sc2, gt2 = (mods[l, :, i] for i in range(N_MOD))
        w_in_bf = w_in[l].astype(BF16)
        w_out_bf = w_out[l].astype(BF16)
        wr = jnp.zeros((D, LANES), F32).at[:, :N_EXPERTS].set(w_router[l])
        wr_hi = wr.astype(BF16)
        wr_cat = jnp.concatenate([wr_hi, (wr - wr_hi.astype(F32)).astype(BF16)], axis=0)
        lams = [vec(p[l]) for p in (lam_q1, lam_k1, lam_q2, lam_k2)]
        gq_t, gk_t, go_t = tile_g(g_q[l]), tile_g(g_k[l]), tile_g(g_attn_out[l])
        mix_w = (_block_diag(w_pool[l]).astype(BF16), vec(b_pool[l]), vec(pool_scale[l]), conv_w[l], vec(conv_b[l]),
                 vec(conv_ln_g[l]), vec(conv_ln_b[l]), w_pw2[l].astype(BF16), vec(sgu_ln_g[l]), vec(sgu_ln_b[l]),
                 w_spatial[l].astype(BF16), jnp.repeat(b_spatial[l].T, SGU_GW, axis=1))
        g1, g2 = vec(g_norm1[l]), vec(g_norm2[l])

        qx, kx, vx, rx = _inproj(x, lat_row, g1, sc1, sh1, w_in_bf, gq_t, gk_t, cos_x, sin_x, seg32, tm=1024, sub=512)
        qc, kc, vc, rc = _inproj(ctx, ctx_row, g1, sc1, sh1, w_in_bf, gq_t, gk_t, cos_c, sin_c, seg32, tm=256)

        attn_x = _attention(qx, [(kc, vc), (kx, vx)], lams, go_t, gq_t, gk_t, lam_init, tq=256)
        x, h2x, aff_x = _mixout(x, attn_x, rx, lat_row, mix_w, w_out_bf, gt1, g2, sc2, sh2, wr_cat)
        pos_x, gate_x, ptok_x = _route(aff_x, tri, EC_CAPACITY * T // N_EXPERTS)
        groups = [_gather(pos_x, gate_x, h2x, n_exp=4)]

        if not last:
            attn_c = _attention(qc, [(kc, vc)], lams, go_t, gq_t, gk_t, lam_init, tq=256)
            ctx, h2c, aff_c = _mixout(ctx, attn_c, rc, ctx_row, mix_w, w_out_bf, gt1, g2, sc2, sh2, wr_cat)
            pos_c, gate_c, ptok_c = _route(aff_c, tri, EC_CAPACITY * C // N_EXPERTS)
            groups.append(_gather(pos_c, gate_c, h2c, n_exp=N_EXPERTS))

        ys = _moe(groups, l, w_gate, w_up, w_down)
        x = _combine(x, gt2, lat_row, ptok_x, ys[0])
        if not last:
            ctx = _combine(ctx, gt2, ctx_row, ptok_c, ys[1])
    return x
```

```python
import functools
import math

import jax
import jax.numpy as jnp
from jax import lax
from jax.experimental import pallas as pl
from jax.experimental.pallas import tpu as pltpu

F32 = jnp.float32
BF16 = jnp.bfloat16

D_MODEL = 1024
GRID_W = 64
GROUP_W = 256
ATTN_HEADS = 4
ATTN_D = 32
ATTN_VD = 64
ROPE_THETA = 10000.0
POOL_WINDOWS = (2, 4, 8, 16)
POOL_GW = 64
CONV_K = 31
SGU_CHUNK = 128
SGU_GROUPS = 4
SGU_GW = 64
N_EXPERTS = 16
EC_CAPACITY = 2
D_EXPERT = 2048
N_MOD = 6
IN_W = 2048
EPS = 1e-6
LOG2E = 1.4426950408889634
MAX_UNSTABILISED_SCORE = 60.0
SCORE_BOUND_SLACK = 1.02
SUB_ULP_STEPS = 12

MOD_ROWS = 16
LANES = 128
SUBLANES = 8
HALO = 16
TIME_CHUNK = 256
VMEM_LIMIT = 56 * 1024 * 1024


def _params(sem):
    return pltpu.CompilerParams(dimension_semantics=sem, vmem_limit_bytes=VMEM_LIMIT)


def _dot(a, b):
    return jnp.dot(a, b, preferred_element_type=F32)


def _dot_t(a, b):
    return lax.dot_general(a, b, (((1,), (1,)), ((), ())), preferred_element_type=F32)


def _split_dot(a, w):
    hi = a.astype(BF16)
    lo = (a - hi.astype(F32)).astype(BF16)
    return _dot(hi, w) + _dot(lo, w)


def _silu(x):
    return x * jax.nn.sigmoid(x)


def _layer_norm(x, g, b):
    mu = jnp.mean(x, axis=-1, keepdims=True)
    xc = x - mu
    var = jnp.mean(xc * xc, axis=-1, keepdims=True)
    return xc * lax.rsqrt(var + EPS) * g + b


def _mod_kernel(c_ref, w_ref, b_ref, o_ref):
    s = _silu(c_ref[...])
    o_ref[...] = _dot(s.astype(BF16), w_ref[...].astype(BF16)) + b_ref[...]


def _modulation(cond, w_mod, b_mod):
    L, D, N = w_mod.shape
    tn = 1024
    return pl.pallas_call(
        _mod_kernel,
        grid=(L, N // tn),
        in_specs=[
            pl.BlockSpec((MOD_ROWS, D), lambda l, j: (0, 0)),
            pl.BlockSpec((None, D, tn), lambda l, j: (l, 0, j)),
            pl.BlockSpec((None, 1, tn), lambda l, j: (l, 0, j)),
        ],
        out_specs=pl.BlockSpec((None, MOD_ROWS, tn), lambda l, j: (l, 0, j)),
        out_shape=jax.ShapeDtypeStruct((L, MOD_ROWS, N), F32),
        compiler_params=_params(("arbitrary", "arbitrary")),
        name="modulation",
    )(cond, w_mod, b_mod.reshape(L, 1, N))


def _inproj_kernel(x_ref, g_ref, sc_ref, sh_ref, w_ref, gq_ref, gk_ref, cos_ref, sin_ref, seg_ref,
                   q_ref, k_ref, v_ref, r_ref, *, sub):
    gain = g_ref[...] * (1.0 + sc_ref[...])
    seg = seg_ref[...]
    first = (lax.broadcasted_iota(jnp.int32, (sub, GROUP_W), 1) & 15) < 8

    for r in range(x_ref.shape[0] // sub):
        rows = slice(r * sub, (r + 1) * sub)
        x = x_ref[rows, :]
        h = x * lax.rsqrt(jnp.mean(x * x, axis=-1, keepdims=True) + EPS) * gain + sh_ref[...]
        p = _dot(h.astype(BF16), w_ref[...])
        cos = cos_ref[rows, :]
        sin = sin_ref[rows, :]

        def prep(a, g):
            ss = _split_dot(a * a, seg)
            n = a * lax.rsqrt(ss * (1.0 / ATTN_D) + EPS) * g
            partner = jnp.where(first, pltpu.roll(n, GROUP_W - 8, 1), pltpu.roll(n, 8, 1))
            return n * cos + partner * sin

        q_ref[rows, :] = (prep(p[:, 0:GROUP_W], gq_ref[...]) * (ATTN_D ** -0.5 * LOG2E)).astype(BF16)
        k_ref[:, rows] = jnp.transpose(prep(p[:, GROUP_W:2 * GROUP_W], gk_ref[...])).astype(BF16)
        v_ref[rows, :] = p[:, 2 * GROUP_W:3 * GROUP_W].astype(BF16)
        r_ref[rows, :] = p[:, 3 * GROUP_W:]


def _inproj(x, mod_row, g1, sc, sh, w_bf, gq_t, gk_t, cos_t, sin_t, seg32, tm, sub=256):
    B, T, D = x.shape
    tm = min(tm, T)
    rest_w = IN_W - 3 * GROUP_W
    row = lambda b, i: (mod_row(b), 0, 0)
    const2 = lambda b, i: (0, 0)
    tok = lambda b, i: (b, i, 0)
    return pl.pallas_call(
        functools.partial(_inproj_kernel, sub=min(sub, tm)),
        grid=(B, T // tm),
        in_specs=[
            pl.BlockSpec((None, tm, D), tok),
            pl.BlockSpec((1, D), const2),
            pl.BlockSpec((None, 1, D), row),
            pl.BlockSpec((None, 1, D), row),
            pl.BlockSpec((D, IN_W), const2),
            pl.BlockSpec((1, GROUP_W), const2),
            pl.BlockSpec((1, GROUP_W), const2),
            pl.BlockSpec((tm, GROUP_W), lambda b, i: (i, 0)),
            pl.BlockSpec((tm, GROUP_W), lambda b, i: (i, 0)),
            pl.BlockSpec((GROUP_W, GROUP_W), const2),
        ],
        out_specs=[
            pl.BlockSpec((None, tm, GROUP_W), tok),
            pl.BlockSpec((None, GROUP_W, tm), lambda b, i: (b, 0, i)),
            pl.BlockSpec((None, tm, GROUP_W), tok),
            pl.BlockSpec((None, tm, rest_w), tok),
        ],
        out_shape=[
            jax.ShapeDtypeStruct((B, T, GROUP_W), BF16),
            jax.ShapeDtypeStruct((B, GROUP_W, T), BF16),
            jax.ShapeDtypeStruct((B, T, GROUP_W), BF16),
            jax.ShapeDtypeStruct((B, T, rest_w), F32),
        ],
        compiler_params=_params(("arbitrary", "arbitrary")),
        name="inproj",
    )(x, g1, sc, sh, w_bf, gq_t, gk_t, cos_t, sin_t, seg32)


def _attn_kernel(*refs, n_seg, lam_init):
    q_ref = refs[0]
    kv_refs = refs[1:1 + 2 * n_seg]
    lq1_ref, lk1_ref, lq2_ref, lk2_ref, go_ref, gq_ref, gk_ref, o_ref = refs[1 + 2 * n_seg:]
    lam = (jnp.exp(jnp.sum(lq1_ref[...] * lk1_ref[...], axis=-1, keepdims=True))
           - jnp.exp(jnp.sum(lq2_ref[...] * lk2_ref[...], axis=-1, keepdims=True)) + lam_init)
    q = q_ref[...]

    def scores(off):
        qs = q[:, off:off + ATTN_D]
        return [_dot(qs, kv_refs[2 * i][off:off + ATTN_D, :]) for i in range(n_seg)]

    def run(stabilise):
        def softmax_parts(s):
            if stabilise:
                mx = functools.reduce(jnp.maximum, [jnp.max(si, axis=-1, keepdims=True) for si in s])
                s = [si - mx for si in s]
            p = [jnp.exp2(si) for si in s]
            den = functools.reduce(jnp.add, [jnp.sum(pi, axis=-1, keepdims=True) for pi in p])
            return [pi.astype(BF16) for pi in p], den

        heads = []
        nxt = (scores(0), scores(ATTN_D))
        for h in range(ATTN_HEADS):
            cur = nxt
            if h + 1 < ATTN_HEADS:
                nxt = (scores((h + 1) * ATTN_VD), scores((h + 1) * ATTN_VD + ATTN_D))
            p1, l1 = softmax_parts(cur[0])
            p2, l2 = softmax_parts(cur[1])
            c = (lam * l1 / l2).astype(BF16)
            o = None
            for i in range(n_seg):
                oi = _dot(p1[i] - c * p2[i], kv_refs[2 * i + 1][:, h * ATTN_VD:(h + 1) * ATTN_VD])
                o = oi if o is None else o + oi
            o = o * (1.0 / l1)
            ms = jnp.mean(o * o, axis=-1, keepdims=True)
            heads.append(o * lax.rsqrt(ms + EPS))
        o_ref[...] = (jnp.concatenate(heads, axis=1) * go_ref[...] * (1.0 - lam_init)).astype(BF16)

    bound = (jnp.max(jnp.abs(gq_ref[...])) * jnp.max(jnp.abs(gk_ref[...]))
             * (ATTN_D * ATTN_D ** -0.5 * LOG2E * SCORE_BOUND_SLACK))
    small = bound < MAX_UNSTABILISED_SCORE

    @pl.when(small)
    def _():
        run(False)

    @pl.when(jnp.logical_not(small))
    def _():
        run(True)


def _attention(q, kvs, lams, go_t, gq_t, gk_t, lam_init, tq):
    B, T, _ = q.shape
    tok = lambda b, i: (b, i, 0)
    const2 = lambda b, i: (0, 0)
    in_specs = [pl.BlockSpec((None, tq, GROUP_W), tok)]
    args = [q]
    for kt, v in kvs:
        n = v.shape[1]
        in_specs += [pl.BlockSpec((None, GROUP_W, n), lambda b, i: (b, 0, 0)),
                     pl.BlockSpec((None, n, GROUP_W), lambda b, i: (b, 0, 0))]
        args += [kt, v]
    in_specs += [pl.BlockSpec((1, ATTN_D), const2)] * 4
    in_specs += [pl.BlockSpec((1, GROUP_W), const2)] * 3
    args += list(lams) + [go_t, gq_t, gk_t]
    return pl.pallas_call(
        functools.partial(_attn_kernel, n_seg=len(kvs), lam_init=lam_init),
        grid=(B, T // tq),
        in_specs=in_specs,
        out_specs=pl.BlockSpec((None, tq, GROUP_W), tok),
        out_shape=jax.ShapeDtypeStruct((B, T, GROUP_W), BF16),
        compiler_params=_params(("arbitrary", "arbitrary")),
        name="diff_attention",
    )(*args)


def _gelu_tanh(x):
    return 0.5 * x * (1.0 + jnp.tanh(0.7978845608028654 * (x + 0.044715 * (x * x * x))))


def _mixout_kernel(cur_ref, prev_ref, next_ref, wpool_ref, bpool_ref, pscale_ref, convw_ref, convb_ref, clng_ref,
                   clnb_ref, wpw2_ref, slng_ref, slnb_ref, wsp_ref, bsp_ref,
                   x_ref, a_ref, w_ref, gt_ref, g2_ref, sc_ref, sh_ref, wr_ref,
                   xo_ref, h_ref, aff_ref, zwin, ywin, *, T):
    i = pl.program_id(1)
    has_prev = jnp.where(i > 0, 1.0, 0.0)
    has_next = jnp.where(i < pl.num_programs(1) - 1, 1.0, 0.0)

    def glu(ref):
        return ref[:, GROUP_W:2 * GROUP_W] * jax.nn.sigmoid(ref[:, 2 * GROUP_W:3 * GROUP_W])

    step = cur_ref.shape[0]
    win_rows = step + 2 * HALO

    def fill(win, head, body, tail):
        win[0, 0:HALO, :] = head * has_prev
        win[0, HALO:HALO + step, :] = body
        win[0, HALO + step:, :] = tail * has_next
        whole = win[0]
        for r in range(1, SUBLANES):
            win[r] = pltpu.roll(whole, win_rows - r, 0)

    def shifted(win, row):
        base = row - row % SUBLANES
        return win[row % SUBLANES, base:base + TIME_CHUNK, :]

    fill(zwin, prev_ref[:, 0:GROUP_W], cur_ref[:, 0:GROUP_W], next_ref[:, 0:GROUP_W])
    fill(ywin, glu(prev_ref), glu(cur_ref), glu(next_ref))

    pool_group = lax.broadcasted_iota(jnp.int32, (TIME_CHUNK, GROUP_W), 1) >> 6
    sgu_masks = [((lax.broadcasted_iota(jnp.int32, (1, GROUP_W), 1) >> 6) == g).astype(F32) for g in range(SGU_GROUPS)]
    gain = g2_ref[...] * (1.0 + sc_ref[...])

    for c in range(step // TIME_CHUNK):
        first = c * TIME_CHUNK
        rows = slice(first, first + TIME_CHUNK)

        def zs(off):
            return shifted(zwin, first + HALO + off)

        t = i * step + first + lax.broadcasted_iota(jnp.int32, (TIME_CHUNK, 1), 0)
        z0 = zs(0)
        sums = []
        acc = None
        for w in POOL_WINDOWS:
            for off in range(-(w // 2), w // 2):
                if acc is None or not (-(w // 4) <= off < w // 4):
                    acc = zs(off) if acc is None else acc + zs(off)
            sums.append(acc)
        mean = None
        for g, w in enumerate(POOL_WINDOWS):
            cnt = (jnp.minimum(t + w // 2, T) - jnp.maximum(t - w // 2, 0)).astype(F32)
            mg = sums[g] / cnt
            mean = mg if mean is None else jnp.where(pool_group == g, mg, mean)
        pool = (_dot((mean - z0).astype(BF16), wpool_ref[...]) + bpool_ref[...]) * pscale_ref[...]
        mix = _dot(a_ref[rows, :], w_ref[0:GROUP_W, :]) + _dot(pool.astype(BF16), w_ref[GROUP_W:2 * GROUP_W, :])

        acc = None
        for k in range(CONV_K):
            term = shifted(ywin, first + HALO + k - CONV_K // 2) * convw_ref[k:k + 1, :]
            acc = term if acc is None else acc + term
        cn = _silu(_layer_norm(acc + convb_ref[...], clng_ref[...], clnb_ref[...]))
        conv = _dot(cn.astype(BF16), wpw2_ref[...])
        mix = mix + _dot(conv.astype(BF16), w_ref[2 * GROUP_W:3 * GROUP_W, :])

        gl = _gelu_tanh(cur_ref[rows, 3 * GROUP_W:5 * GROUP_W])
        u = gl[:, 0:GROUP_W]
        vn = _layer_norm(gl[:, GROUP_W:2 * GROUP_W], slng_ref[...], slnb_ref[...])
        gated = []
        for j in range(TIME_CHUNK // SGU_CHUNK):
            sub = slice(j * SGU_CHUNK, (j + 1) * SGU_CHUNK)
            s = bsp_ref[...]
            for g in range(SGU_GROUPS):
                s = s + _dot(wsp_ref[g], (vn[sub] * sgu_masks[g]).astype(BF16))
            gated.append((u[sub] * s).astype(BF16))
        mix = mix + _dot(jnp.concatenate(gated, axis=0), w_ref[3 * GROUP_W:, :])

        xn = x_ref[rows, :] + gt_ref[...] * mix
        xo_ref[rows, :] = xn
        h = xn * lax.rsqrt(jnp.mean(xn * xn, axis=-1, keepdims=True) + EPS) * gain + sh_ref[...]
        hi = h.astype(BF16)
        h_ref[rows, :] = hi
        lo = (h - hi.astype(F32)).astype(BF16)
        d = hi.shape[1]
        logits = _dot(hi, wr_ref[0:d, :]) + _dot(hi, wr_ref[d:2 * d, :]) + _dot(lo, wr_ref[0:d, :])
        lt = jnp.transpose(logits)[0:N_EXPERTS, :]
        e = jnp.exp(lt - jnp.max(lt, axis=0, keepdims=True))
        aff_ref[:, rows] = e / jnp.sum(e, axis=0, keepdims=True)


def _mixout(x, attn, rest, mod_row, mix_w, w_bf, gt, g2, sc, sh, wr_cat, step=2 * TIME_CHUNK):
    B, T, D = x.shape
    RW = rest.shape[-1]
    step = min(step, T)
    n_step = T // step
    per_step = step // HALO
    const2 = lambda b, i: (0, 0)
    row = lambda b, i: (mod_row(b), 0, 0)
    tok = lambda b, i: (b, i, 0)
    vec = pl.BlockSpec((1, GROUP_W), const2)
    mat = pl.BlockSpec((GROUP_W, GROUP_W), const2)
    return pl.pallas_call(
        functools.partial(_mixout_kernel, T=T),
        grid=(B, n_step),
        in_specs=[
            pl.BlockSpec((None, step, RW), tok),
            pl.BlockSpec((None, HALO, RW), lambda b, i: (b, jnp.maximum(i * per_step - 1, 0), 0)),
            pl.BlockSpec((None, HALO, RW), lambda b, i: (b, jnp.minimum((i + 1) * per_step, n_step * per_step - 1), 0)),
            mat, vec, vec,
            pl.BlockSpec((CONV_K, GROUP_W), const2), vec, vec, vec, mat,
            vec, vec,
            pl.BlockSpec((SGU_GROUPS, SGU_CHUNK, SGU_CHUNK), lambda b, i: (0, 0, 0)),
            pl.BlockSpec((SGU_CHUNK, GROUP_W), const2),
            pl.BlockSpec((None, step, D), tok),
            pl.BlockSpec((None, step, GROUP_W), tok),
            pl.BlockSpec((D, D), const2),
            pl.BlockSpec((None, 1, D), row),
            pl.BlockSpec((1, D), const2),
            pl.BlockSpec((None, 1, D), row),
            pl.BlockSpec((None, 1, D), row),
            pl.BlockSpec((2 * D, LANES), const2),
        ],
        out_specs=[
            pl.BlockSpec((None, step, D), tok),
            pl.BlockSpec((None, step, D), tok),
            pl.BlockSpec((None, N_EXPERTS, step), lambda b, i: (b, 0, i)),
        ],
        out_shape=[
            jax.ShapeDtypeStruct((B, T, D), F32),
            jax.ShapeDtypeStruct((B, T, D), BF16),
            jax.ShapeDtypeStruct((B, N_EXPERTS, T), F32),
        ],
        scratch_shapes=[pltpu.VMEM((SUBLANES, step + 2 * HALO, GROUP_W), F32)] * 2,
        compiler_params=_params(("arbitrary", "arbitrary")),
        name="mixers_outproj",
    )(rest, rest, rest, *mix_w, x, attn, w_bf, gt, g2, sc, sh, wr_cat)


def _lane_prefix(x, tri):
    outs = []
    off = jnp.zeros((x.shape[0], 1), F32)
    for j in range(x.shape[1] // LANES):
        xb = x[:, j * LANES:(j + 1) * LANES]
        inc = _dot(xb.astype(BF16), tri)
        outs.append(inc - xb + off)
        off = off + inc[:, LANES - 1:LANES]
    return jnp.concatenate(outs, axis=1)


def _route_kernel(a_ref, tri_ref, pos_ref, gate_ref, ptok_ref, *, cap, n_expert):
    a = a_ref[...]
    R, T = a.shape

    def enough(c):
        return jnp.sum(jnp.where(a >= c, 1.0, 0.0), axis=-1, keepdims=True) >= cap

    def bit_search(i, lo):
        cand = lo | lax.shift_left(jnp.int32(1), 30 - i)
        return jnp.where(enough(pltpu.bitcast(cand, F32)), cand, lo)

    lo_bits = lax.fori_loop(0, 31, bit_search, jnp.zeros((R, 1), jnp.int32))

    def refine(i, c):
        lo, hi = c
        mid = lo + 0.5 * (hi - lo)
        ok = enough(mid)
        return jnp.where(ok, mid, lo), jnp.where(ok, hi, mid)

    thr, _ = lax.fori_loop(0, SUB_ULP_STEPS, refine,
                           (pltpu.bitcast(lo_bits, F32), pltpu.bitcast(lo_bits + 1, F32)))
    gt = a > thr
    eq = a == thr
    need = cap - jnp.sum(jnp.where(gt, 1.0, 0.0), axis=-1, keepdims=True)
    tri = tri_ref[...]
    tie_rank = _lane_prefix(jnp.where(eq, 1.0, 0.0), tri)
    sel = jnp.logical_or(gt, jnp.logical_and(eq, tie_rank < need))
    pos = jnp.where(sel, _lane_prefix(jnp.where(sel, 1.0, 0.0), tri), -1.0)
    pos_ref[...] = pos
    gate_ref[...] = jnp.where(sel, a, 0.0)
    fill = jnp.full((LANES - n_expert, T), -1.0, F32)
    for b in range(R // n_expert):
        padded = jnp.concatenate([pos[b * n_expert:(b + 1) * n_expert], fill], axis=0)
        ptok_ref[b] = jnp.transpose(padded)


def _route(aff_t, tri, cap):
    B, E, T = aff_t.shape
    blk = pl.BlockSpec((B * E, T), lambda i: (0, 0))
    pos, gate, ptok = pl.pallas_call(
        functools.partial(_route_kernel, cap=cap, n_expert=E),
        grid=(1,),
        in_specs=[blk, pl.BlockSpec((LANES, LANES), lambda i: (0, 0))],
        out_specs=[blk, blk, pl.BlockSpec((B, T, LANES), lambda i: (0, 0, 0))],
        out_shape=[
            jax.ShapeDtypeStruct((B * E, T), F32),
            jax.ShapeDtypeStruct((B * E, T), F32),
            jax.ShapeDtypeStruct((B, T, LANES), F32),
        ],
        compiler_params=_params(("arbitrary",)),
        name="route",
    )(aff_t.reshape(B * E, T), tri)
    return pos.reshape(B, E, T), gate.reshape(B, E, T), ptok


def _gather_kernel(pos_ref, gate_ref, h_ref, xs_ref, gs_ref, *, cap):
    n_exp = xs_ref.shape[0]
    first = pl.program_id(1) * n_exp
    T = pos_ref.shape[1]
    slot = lax.broadcasted_iota(jnp.int32, (cap, T), 0).astype(F32)
    for j in range(n_exp):
        prow = pos_ref[pl.ds(first + j, 1), :]
        grow = gate_ref[pl.ds(first + j, 1), :]
        hit = slot == prow
        onehot = jnp.where(hit, 1.0, 0.0).astype(BF16)
        xs_ref[j] = _dot(onehot, h_ref[...]).astype(BF16)
        gs_ref[j] = jnp.sum(jnp.where(hit, grow, 0.0), axis=-1, keepdims=True)


def _gather(pos_t, gate_t, h, n_exp):
    B, E, T = pos_t.shape
    D = h.shape[-1]
    cap = EC_CAPACITY * T // E
    et = pl.BlockSpec((None, E, T), lambda b, e: (b, 0, 0))
    return pl.pallas_call(
        functools.partial(_gather_kernel, cap=cap),
        grid=(B, E // n_exp),
        in_specs=[et, et, pl.BlockSpec((None, T, D), lambda b, e: (b, 0, 0))],
        out_specs=[
            pl.BlockSpec((n_exp, cap, D), lambda b, e: (e, b, 0)),
            pl.BlockSpec((n_exp, cap, 1), lambda b, e: (e, b, 0)),
        ],
        out_shape=[
            jax.ShapeDtypeStruct((E, B * cap, D), BF16),
            jax.ShapeDtypeStruct((E, B * cap, 1), F32),
        ],
        compiler_params=_params(("arbitrary", "arbitrary")),
        name="gather",
    )(pos_t, gate_t, h)


def _moe_kernel(*refs, n_grp, tm):
    xs = refs[0:2 * n_grp:2]
    gs = refs[1:2 * n_grp:2]
    wg_ref, wu_ref, wd_ref = refs[2 * n_grp:2 * n_grp + 3]
    ys = refs[2 * n_grp + 3:3 * n_grp + 3]
    accs = refs[3 * n_grp + 3:4 * n_grp + 3]
    wgb, wub, wdb = refs[4 * n_grp + 3:]
    f = pl.program_id(1)

    @pl.when(jnp.logical_and(pl.program_id(0) == 0, f == 0))
    def _():
        for acc in accs:
            acc[...] = jnp.zeros_like(acc)

    wgb[...] = wg_ref[...].astype(BF16)
    wub[...] = wu_ref[...].astype(BF16)
    wdb[...] = wd_ref[...].astype(BF16)
    carry_on = f > 0

    for x_ref, g_ref, y_ref, acc in zip(xs, gs, ys, accs):
        rows = x_ref.shape[0]
        t = min(tm, rows)
        for r in range(rows // t):
            sl = slice(r * t, (r + 1) * t)
            xt = x_ref[sl, :]
            a = _dot(xt, wgb[...])
            u = _dot(xt, wub[...])
            total = jnp.where(carry_on, acc[sl, :], 0.0) + _dot((_silu(a) * u).astype(BF16), wdb[...])
            acc[sl, :] = total
            y_ref[sl, :] = (total * g_ref[sl, :]).astype(BF16)


def _moe(groups, layer, w_gate, w_up, w_down, tf=512, tm=512):
    _, E, D, F = w_gate.shape
    in_specs, args, out_specs, out_shape, scratch = [], [], [], [], []
    for xs, gs in groups:
        R = xs.shape[1]
        in_specs += [pl.BlockSpec((None, R, D), lambda e, f: (e, 0, 0)), pl.BlockSpec((None, R, 1), lambda e, f: (e, 0, 0))]
        args += [xs, gs]
        out_specs.append(pl.BlockSpec((None, R, D), lambda e, f: (e, 0, 0)))
        out_shape.append(jax.ShapeDtypeStruct((E, R, D), BF16))
        scratch.append(pltpu.VMEM((R, D), F32))
    in_specs += [
        pl.BlockSpec((None, None, D, tf), lambda e, f: (layer, e, 0, f)),
        pl.BlockSpec((None, None, D, tf), lambda e, f: (layer, e, 0, f)),
        pl.BlockSpec((None, None, tf, D), lambda e, f: (layer, e, f, 0)),
    ]
    args += [w_gate, w_up, w_down]
    scratch += [pltpu.VMEM((D, tf), BF16), pltpu.VMEM((D, tf), BF16), pltpu.VMEM((tf, D), BF16)]
    return pl.pallas_call(
        functools.partial(_moe_kernel, n_grp=len(groups), tm=tm),
        grid=(E, F // tf),
        in_specs=in_specs,
        out_specs=out_specs,
        out_shape=out_shape,
        scratch_shapes=scratch,
        compiler_params=_params(("arbitrary", "arbitrary")),
        name="expert_ffn",
    )(*args)


def _combine_kernel(x_ref, gt_ref, ptok_ref, ys_ref, o_ref, *, tm):
    T, D = x_ref.shape
    E, cap, _ = ys_ref.shape
    t = min(tm, T)
    lane = lax.broadcasted_iota(jnp.int32, (t, cap), 1).astype(F32)

    def tile(r, carry):
        sl = pl.ds(pl.multiple_of(r * t, t), t)
        pt = ptok_ref[sl, :]
        acc = jnp.zeros((t, D), F32)
        for e in range(E):
            onehot = jnp.where(pt[:, e:e + 1] == lane, 1.0, 0.0).astype(BF16)
            acc = acc + _dot(onehot, ys_ref[e])
        o_ref[sl, :] = x_ref[sl, :] + gt_ref[...] * acc
        return carry

    lax.fori_loop(0, T // t, tile, 0)


def _combine(x, gt, mod_row, ptok, ys, tm=256):
    B, T, D = x.shape
    E = ys.shape[0]
    cap = ys.shape[1] // B
    ys4 = ys.reshape(E, B, cap, D)
    return pl.pallas_call(
        functools.partial(_combine_kernel, tm=tm),
        grid=(B,),
        in_specs=[
            pl.BlockSpec((None, T, D), lambda b: (b, 0, 0)),
            pl.BlockSpec((None, 1, D), lambda b: (mod_row(b), 0, 0)),
            pl.BlockSpec((None, T, LANES), lambda b: (b, 0, 0)),
            pl.BlockSpec((E, None, cap, D), lambda b: (0, b, 0, 0)),
        ],
        out_specs=pl.BlockSpec((None, T, D), lambda b: (b, 0, 0)),
        out_shape=jax.ShapeDtypeStruct((B, T, D), F32),
        compiler_params=_params(("arbitrary",)),
        name="combine",
    )(x, gt, ptok, ys4)


def _rope_tables(T):
    rows = (jnp.arange(T) // GRID_W).astype(F32)
    cols = (jnp.arange(T) % GRID_W).astype(F32)
    n_freq = ATTN_D // 4
    inv = ROPE_THETA ** (-jnp.arange(n_freq, dtype=F32) / n_freq)
    ang_r, ang_c = rows[:, None] * inv, cols[:, None] * inv
    cos = jnp.concatenate([jnp.cos(ang_r)] * 2 + [jnp.cos(ang_c)] * 2, axis=1)
    sin = jnp.concatenate([-jnp.sin(ang_r), jnp.sin(ang_r), -jnp.sin(ang_c), jnp.sin(ang_c)], axis=1)
    reps = GROUP_W // ATTN_D
    return jnp.tile(cos, (1, reps)).astype(F32), jnp.tile(sin, (1, reps)).astype(F32)


def _segment_ones(width):
    i = jnp.arange(GROUP_W) // width
    return (i[:, None] == i[None, :]).astype(BF16)


def _block_diag(w):
    G, n, _ = w.shape
    eye = jnp.eye(G, dtype=w.dtype)
    return (eye[:, None, :, None] * w[:, :, None, :]).reshape(G * n, G * n)


def kernel(x, c, ctx, c_ctx, w_mod, b_mod, g_norm1, g_norm2, w_in, w_out, g_q, g_k, lam_q1, lam_k1, lam_q2, lam_k2,
           g_attn_out, w_pool, b_pool, pool_scale, conv_w, conv_b, conv_ln_g, conv_ln_b, w_pw2, sgu_ln_g, sgu_ln_b,
           w_spatial, b_spatial, w_router, w_gate, w_up, w_down):
    B, T, D = x.shape
    C = ctx.shape[1]
    L = w_mod.shape[0]
    assert B < MOD_ROWS and D == D_MODEL and T % TIME_CHUNK == 0 and C % TIME_CHUNK == 0

    cond = jnp.zeros((MOD_ROWS, D), F32).at[:B].set(c).at[B].set(c_ctx)
    mods = _modulation(cond, w_mod, b_mod).reshape(L, MOD_ROWS, N_MOD, 1, D)
    lat_row = lambda b: b
    ctx_row = lambda b: B

    cos_x, sin_x = _rope_tables(T)
    cos_c, sin_c = jnp.ones((C, GROUP_W), F32), jnp.zeros((C, GROUP_W), F32)
    seg32 = _segment_ones(ATTN_D)
    tri = (jnp.arange(LANES)[:, None] <= jnp.arange(LANES)[None, :]).astype(BF16)
    tile_g = lambda g: jnp.tile(g, GROUP_W // g.shape[0]).reshape(1, GROUP_W)
    vec = lambda v: v.reshape(1, -1)

    for l in range(L):
        last = l == L - 1
        lam_init = 0.8 - 0.6 * math.exp(-0.3 * l)
        sh1, sc1, gt1, sh2, sc2, gt2 = (mods[l, :, i] for i in range(N_MOD))
        w_in_bf = w_in[l].astype(BF16)
        w_out_bf = w_out[l].astype(BF16)
        wr = jnp.zeros((D, LANES), F32).at[:, :N_EXPERTS].set(w_router[l])
        wr_hi = wr.astype(BF16)
        wr_cat = jnp.concatenate([wr_hi, (wr - wr_hi.astype(F32)).astype(BF16)], axis=0)
        lams = [vec(p[l]) for p in (lam_q1, lam_k1, lam_q2, lam_k2)]
        gq_t, gk_t, go_t = tile_g(g_q[l]), tile_g(g_k[l]), tile_g(g_attn_out[l])
        mix_w = (_block_diag(w_pool[l]).astype(BF16), vec(b_pool[l]), vec(pool_scale[l]), conv_w[l], vec(conv_b[l]),
                 vec(conv_ln_g[l]), vec(conv_ln_b[l]), w_pw2[l].astype(BF16), vec(sgu_ln_g[l]), vec(sgu_ln_b[l]),
                 w_spatial[l].astype(BF16), jnp.repeat(b_spatial[l].T, SGU_GW, axis=1))
        g1, g2 = vec(g_norm1[l]), vec(g_norm2[l])

        qx, kx, vx, rx = _inproj(x, lat_row, g1, sc1, sh1, w_in_bf, gq_t, gk_t, cos_x, sin_x, seg32, tm=1024, sub=512)
        qc, kc, vc, rc = _inproj(ctx, ctx_row, g1, sc1, sh1, w_in_bf, gq_t, gk_t, cos_c, sin_c, seg32, tm=256)

        attn_x = _attention(qx, [(kc, vc), (kx, vx)], lams, go_t, gq_t, gk_t, lam_init, tq=256)
        x, h2x, aff_x = _mixout(x, attn_x, rx, lat_row, mix_w, w_out_bf, gt1, g2, sc2, sh2, wr_cat)
        pos_x, gate_x, ptok_x = _route(aff_x, tri, EC_CAPACITY * T // N_EXPERTS)
        groups = [_gather(pos_x, gate_x, h2x, n_exp=4)]

        if not last:
            attn_c = _attention(qc, [(kc, vc)], lams, go_t, gq_t, gk_t, lam_init, tq=256)
            ctx, h2c, aff_c = _mixout(ctx, attn_c, rc, ctx_row, mix_w, w_out_bf, gt1, g2, sc2, sh2, wr_cat)
            pos_c, gate_c, ptok_c = _route(aff_c, tri, EC_CAPACITY * C // N_EXPERTS)
            groups.append(_gather(pos_c, gate_c, h2c, n_exp=N_EXPERTS))

        ys = _moe(groups, l, w_gate, w_up, w_down)
        x = _combine(x, gt2, lat_row, ptok_x, ys[0])
        if not last:
            ctx = _combine(ctx, gt2, ctx_row, ptok_c, ys[1])
    return x
```

```python
import functools
import math

import jax
import jax.numpy as jnp
from jax import lax
from jax.experimental import pallas as pl
from jax.experimental.pallas import tpu as pltpu

F32 = jnp.float32
BF16 = jnp.bfloat16

D_MODEL = 1024
GRID_W = 64
GROUP_W = 256
ATTN_HEADS = 4
ATTN_D = 32
ATTN_VD = 64
ROPE_THETA = 10000.0
POOL_WINDOWS = (2, 4, 8, 16)
POOL_GW = 64
CONV_K = 31
SGU_CHUNK = 128
SGU_GROUPS = 4
SGU_GW = 64
N_EXPERTS = 16
EC_CAPACITY = 2
D_EXPERT = 2048
N_MOD = 6
IN_W = 2048
EPS = 1e-6
LOG2E = 1.4426950408889634
MAX_UNSTABILISED_SCORE = 60.0
SCORE_BOUND_SLACK = 1.02
SUB_ULP_STEPS = 12

MOD_ROWS = 16
LANES = 128
SUBLANES = 8
HALO = 16
TIME_CHUNK = 256
VMEM_LIMIT = 56 * 1024 * 1024


def _params(sem):
    return pltpu.CompilerParams(dimension_semantics=sem, vmem_limit_bytes=VMEM_LIMIT)


def _dot(a, b):
    return jnp.dot(a, b, preferred_element_type=F32)


def _dot_t(a, b):
    return lax.dot_general(a, b, (((1,), (1,)), ((), ())), preferred_element_type=F32)


def _split_dot(a, w):
    hi = a.astype(BF16)
    lo = (a - hi.astype(F32)).astype(BF16)
    return _dot(hi, w) + _dot(lo, w)


def _silu(x):
    return x * jax.nn.sigmoid(x)


def _layer_norm(x, g, b):
    mu = jnp.mean(x, axis=-1, keepdims=True)
    xc = x - mu
    var = jnp.mean(xc * xc, axis=-1, keepdims=True)
    return xc * lax.rsqrt(var + EPS) * g + b


def _mod_kernel(c_ref, w_ref, b_ref, o_ref):
    s = _silu(c_ref[...])
    o_ref[...] = _dot(s.astype(BF16), w_ref[...].astype(BF16)) + b_ref[...]


def _modulation(cond, w_mod, b_mod):
    L, D, N = w_mod.shape
    tn = 1024
    return pl.pallas_call(
        _mod_kernel,
        grid=(L, N // tn),
        in_specs=[
            pl.BlockSpec((MOD_ROWS, D), lambda l, j: (0, 0)),
            pl.BlockSpec((None, D, tn), lambda l, j: (l, 0, j)),
            pl.BlockSpec((None, 1, tn), lambda l, j: (l, 0, j)),
        ],
        out_specs=pl.BlockSpec((None, MOD_ROWS, tn), lambda l, j: (l, 0, j)),
        out_shape=jax.ShapeDtypeStruct((L, MOD_ROWS, N), F32),
        compiler_params=_params(("arbitrary", "arbitrary")),
        name="modulation",
    )(cond, w_mod, b_mod.reshape(L, 1, N))


def _inproj_kernel(x_ref, g_ref, sc_ref, sh_ref, w_ref, gq_ref, gk_ref, cos_ref, sin_ref, seg_ref,
                   q_ref, k_ref, v_ref, r_ref, *, sub):
    gain = g_ref[...] * (1.0 + sc_ref[...])
    seg = seg_ref[...]
    first = (lax.broadcasted_iota(jnp.int32, (sub, GROUP_W), 1) & 15) < 8

    for r in range(x_ref.shape[0] // sub):
        rows = slice(r * sub, (r + 1) * sub)
        x = x_ref[rows, :]
        h = x * lax.rsqrt(jnp.mean(x * x, axis=-1, keepdims=True) + EPS) * gain + sh_ref[...]
        p = _dot(h.astype(BF16), w_ref[...])
        cos = cos_ref[rows, :]
        sin = sin_ref[rows, :]

        def prep(a, g):
            ss = _split_dot(a * a, seg)
            n = a * lax.rsqrt(ss * (1.0 / ATTN_D) + EPS) * g
            partner = jnp.where(first, pltpu.roll(n, GROUP_W - 8, 1), pltpu.roll(n, 8, 1))
            return n * cos + partner * sin

        q_ref[rows, :] = (prep(p[:, 0:GROUP_W], gq_ref[...]) * (ATTN_D ** -0.5 * LOG2E)).astype(BF16)
        k_ref[:, rows] = jnp.transpose(prep(p[:, GROUP_W:2 * GROUP_W], gk_ref[...])).astype(BF16)
        v_ref[rows, :] = p[:, 2 * GROUP_W:3 * GROUP_W].astype(BF16)
        r_ref[rows, :] = p[:, 3 * GROUP_W:]


def _inproj(x, mod_row, g1, sc, sh, w_bf, gq_t, gk_t, cos_t, sin_t, seg32, tm, sub=256):
    B, T, D = x.shape
    tm = min(tm, T)
    rest_w = IN_W - 3 * GROUP_W
    row = lambda b, i: (mod_row(b), 0, 0)
    const2 = lambda b, i: (0, 0)
    tok = lambda b, i: (b, i, 0)
    return pl.pallas_call(
        functools.partial(_inproj_kernel, sub=min(sub, tm)),
        grid=(B, T // tm),
        in_specs=[
            pl.BlockSpec((None, tm, D), tok),
            pl.BlockSpec((1, D), const2),
            pl.BlockSpec((None, 1, D), row),
            pl.BlockSpec((None, 1, D), row),
            pl.BlockSpec((D, IN_W), const2),
            pl.BlockSpec((1, GROUP_W), const2),
            pl.BlockSpec((1, GROUP_W), const2),
            pl.BlockSpec((tm, GROUP_W), lambda b, i: (i, 0)),
            pl.BlockSpec((tm, GROUP_W), lambda b, i: (i, 0)),
            pl.BlockSpec((GROUP_W, GROUP_W), const2),
        ],
        out_specs=[
            pl.BlockSpec((None, tm, GROUP_W), tok),
            pl.BlockSpec((None, GROUP_W, tm), lambda b, i: (b, 0, i)),
            pl.BlockSpec((None, tm, GROUP_W), tok),
            pl.BlockSpec((None, tm, rest_w), tok),
        ],
        out_shape=[
            jax.ShapeDtypeStruct((B, T, GROUP_W), BF16),
            jax.ShapeDtypeStruct((B, GROUP_W, T), BF16),
            jax.ShapeDtypeStruct((B, T, GROUP_W), BF16),
            jax.ShapeDtypeStruct((B, T, rest_w), F32),
        ],
        compiler_params=_params(("arbitrary", "arbitrary")),
        name="inproj",
    )(x, g1, sc, sh, w_bf, gq_t, gk_t, cos_t, sin_t, seg32)


def _attn_kernel(*refs, n_seg, lam_init):
    q_ref = refs[0]
    kv_refs = refs[1:1 + 2 * n_seg]
    lq1_ref, lk1_ref, lq2_ref, lk2_ref, go_ref, gq_ref, gk_ref, o_ref = refs[1 + 2 * n_seg:]
    lam = (jnp.exp(jnp.sum(lq1_ref[...] * lk1_ref[...], axis=-1, keepdims=True))
           - jnp.exp(jnp.sum(lq2_ref[...] * lk2_ref[...], axis=-1, keepdims=True)) + lam_init)
    q = q_ref[...]

    def scores(off):
        qs = q[:, off:off + ATTN_D]
        return [_dot(qs, kv_refs[2 * i][off:off + ATTN_D, :]) for i in range(n_seg)]

    def run(stabilise):
        def softmax_parts(s):
            if stabilise:
                mx = functools.reduce(jnp.maximum, [jnp.max(si, axis=-1, keepdims=True) for si in s])
                s = [si - mx for si in s]
            p = [jnp.exp2(si) for si in s]
            den = functools.reduce(jnp.add, [jnp.sum(pi, axis=-1, keepdims=True) for pi in p])
            return [pi.astype(BF16) for pi in p], den

        heads = []
        nxt = (scores(0), scores(ATTN_D))
        for h in range(ATTN_HEADS):
            cur = nxt
            if h + 1 < ATTN_HEADS:
                nxt = (scores((h + 1) * ATTN_VD), scores((h + 1) * ATTN_VD + ATTN_D))
            p1, l1 = softmax_parts(cur[0])
            p2, l2 = softmax_parts(cur[1])
            c = (lam * l1 / l2).astype(BF16)
            o = None
            for i in range(n_seg):
                oi = _dot(p1[i] - c * p2[i], kv_refs[2 * i + 1][:, h * ATTN_VD:(h + 1) * ATTN_VD])
                o = oi if o is None else o + oi
            o = o * (1.0 / l1)
            ms = jnp.mean(o * o, axis=-1, keepdims=True)
            heads.append(o * lax.rsqrt(ms + EPS))
        o_ref[...] = (jnp.concatenate(heads, axis=1) * go_ref[...] * (1.0 - lam_init)).astype(BF16)

    bound = (jnp.max(jnp.abs(gq_ref[...])) * jnp.max(jnp.abs(gk_ref[...]))
             * (ATTN_D * ATTN_D ** -0.5 * LOG2E * SCORE_BOUND_SLACK))
    small = bound < MAX_UNSTABILISED_SCORE

    @pl.when(small)
    def _():
        run(False)

    @pl.when(jnp.logical_not(small))
    def _():
        run(True)


def _attention(q, kvs, lams, go_t, gq_t, gk_t, lam_init, tq):
    B, T, _ = q.shape
    tok = lambda b, i: (b, i, 0)
    const2 = lambda b, i: (0, 0)
    in_specs = [pl.BlockSpec((None, tq, GROUP_W), tok)]
    args = [q]
    for kt, v in kvs:
        n = v.shape[1]
        in_specs += [pl.BlockSpec((None, GROUP_W, n), lambda b, i: (b, 0, 0)),
                     pl.BlockSpec((None, n, GROUP_W), lambda b, i: (b, 0, 0))]
        args += [kt, v]
    in_specs += [pl.BlockSpec((1, ATTN_D), const2)] * 4
    in_specs += [pl.BlockSpec((1, GROUP_W), const2)] * 3
    args += list(lams) + [go_t, gq_t, gk_t]
    return pl.pallas_call(
        functools.partial(_attn_kernel, n_seg=len(kvs), lam_init=lam_init),
        grid=(B, T // tq),
        in_specs=in_specs,
        out_specs=pl.BlockSpec((None, tq, GROUP_W), tok),
        out_shape=jax.ShapeDtypeStruct((B, T, GROUP_W), BF16),
        compiler_params=_params(("arbitrary", "arbitrary")),
        name="diff_attention",
    )(*args)


def _gelu_tanh(x):
    return 0.5 * x * (1.0 + jnp.tanh(0.7978845608028654 * (x + 0.044715 * (x * x * x))))


def _mixout_kernel(cur_ref, prev_ref, next_ref, wpool_ref, bpool_ref, pscale_ref, convw_ref, convb_ref, clng_ref,
                   clnb_ref, wpw2_ref, slng_ref, slnb_ref, wsp_ref, bsp_ref,
                   x_ref, a_ref, w_ref, gt_ref, g2_ref, sc_ref, sh_ref, wr_ref,
                   xo_ref, h_ref, aff_ref, zwin, ywin, *, T):
    i = pl.program_id(1)
    has_prev = jnp.where(i > 0, 1.0, 0.0)
    has_next = jnp.where(i < pl.num_programs(1) - 1, 1.0, 0.0)

    def glu(ref):
        return ref[:, GROUP_W:2 * GROUP_W] * jax.nn.sigmoid(ref[:, 2 * GROUP_W:3 * GROUP_W])

    step = cur_ref.shape[0]
    win_rows = step + 2 * HALO

    def fill(win, head, body, tail):
        win[0, 0:HALO, :] = head * has_prev
        win[0, HALO:HALO + step, :] = body
        win[0, HALO + step:, :] = tail * has_next
        whole = win[0]
        for r in range(1, SUBLANES):
            win[r] = pltpu.roll(whole, win_rows - r, 0)

    def shifted(win, row):
        base = row - row % SUBLANES
        return win[row % SUBLANES, base:base + TIME_CHUNK, :]

    fill(zwin, prev_ref[:, 0:GROUP_W], cur_ref[:, 0:GROUP_W], next_ref[:, 0:GROUP_W])
    fill(ywin, glu(prev_ref), glu(cur_ref), glu(next_ref))

    pool_group = lax.broadcasted_iota(jnp.int32, (TIME_CHUNK, GROUP_W), 1) >> 6
    sgu_masks = [((lax.broadcasted_iota(jnp.int32, (1, GROUP_W), 1) >> 6) == g).astype(F32) for g in range(SGU_GROUPS)]
    gain = g2_ref[...] * (1.0 + sc_ref[...])

    for c in range(step // TIME_CHUNK):
        first = c * TIME_CHUNK
        rows = slice(first, first + TIME_CHUNK)

        def zs(off):
            return shifted(zwin, first + HALO + off)

        t = i * step + first + lax.broadcasted_iota(jnp.int32, (TIME_CHUNK, 1), 0)
        z0 = zs(0)
        sums = []
        acc = None
        for w in POOL_WINDOWS:
            for off in range(-(w // 2), w // 2):
                if acc is None or not (-(w // 4) <= off < w // 4):
                    acc = zs(off) if acc is None else acc + zs(off)
            sums.append(acc)
        mean = None
        for g, w in enumerate(POOL_WINDOWS):
            cnt = (jnp.minimum(t + w // 2, T) - jnp.maximum(t - w // 2, 0)).astype(F32)
            mg = sums[g] / cnt
            mean = mg if mean is None else jnp.where(pool_group == g, mg, mean)
        pool = (_dot((mean - z0).astype(BF16), wpool_ref[...]) + bpool_ref[...]) * pscale_ref[...]
        mix = _dot(a_ref[rows, :], w_ref[0:GROUP_W, :]) + _dot(pool.astype(BF16), w_ref[GROUP_W:2 * GROUP_W, :])

        acc = None
        for k in range(CONV_K):
            term = shifted(ywin, first + HALO + k - CONV_K // 2) * convw_ref[k:k + 1, :]
            acc = term if acc is None else acc + term
        cn = _silu(_layer_norm(acc + convb_ref[...], clng_ref[...], clnb_ref[...]))
        conv = _dot(cn.astype(BF16), wpw2_ref[...])
        mix = mix + _dot(conv.astype(BF16), w_ref[2 * GROUP_W:3 * GROUP_W, :])

        gl = _gelu_tanh(cur_ref[rows, 3 * GROUP_W:5 * GROUP_W])
        u = gl[:, 0:GROUP_W]
        vn = _layer_norm(gl[:, GROUP_W:2 * GROUP_W], slng_ref[...], slnb_ref[...])
        gated = []
        for j in range(TIME_CHUNK // SGU_CHUNK):
            sub = slice(j * SGU_CHUNK, (j + 1) * SGU_CHUNK)
            s = bsp_ref[...]
            for g in range(SGU_GROUPS):
                s = s + _dot(wsp_ref[g], (vn[sub] * sgu_masks[g]).astype(BF16))
            gated.append((u[sub] * s).astype(BF16))
        mix = mix + _dot(jnp.concatenate(gated, axis=0), w_ref[3 * GROUP_W:, :])

        xn = x_ref[rows, :] + gt_ref[...] * mix
        xo_ref[rows, :] = xn
        h = xn * lax.rsqrt(jnp.mean(xn * xn, axis=-1, keepdims=True) + EPS) * gain + sh_ref[...]
        hi = h.astype(BF16)
        h_ref[rows, :] = hi
        lo = (h - hi.astype(F32)).astype(BF16)
        d = hi.shape[1]
        logits = _dot(hi, wr_ref[0:d, :]) + _dot(hi, wr_ref[d:2 * d, :]) + _dot(lo, wr_ref[0:d, :])
        lt = jnp.transpose(logits)[0:N_EXPERTS, :]
        e = jnp.exp(lt - jnp.max(lt, axis=0, keepdims=True))
        aff_ref[:, rows] = e / jnp.sum(e, axis=0, keepdims=True)


def _mixout(x, attn, rest, mod_row, mix_w, w_bf, gt, g2, sc, sh, wr_cat, step=4 * TIME_CHUNK):
    B, T, D = x.shape
    RW = rest.shape[-1]
    step = min(step, T)
    n_step = T // step
    per_step = step // HALO
    const2 = lambda b, i: (0, 0)
    row = lambda b, i: (mod_row(b), 0, 0)
    tok = lambda b, i: (b, i, 0)
    vec = pl.BlockSpec((1, GROUP_W), const2)
    mat = pl.BlockSpec((GROUP_W, GROUP_W), const2)
    return pl.pallas_call(
        functools.partial(_mixout_kernel, T=T),
        grid=(B, n_step),
        in_specs=[
            pl.BlockSpec((None, step, RW), tok),
            pl.BlockSpec((None, HALO, RW), lambda b, i: (b, jnp.maximum(i * per_step - 1, 0), 0)),
            pl.BlockSpec((None, HALO, RW), lambda b, i: (b, jnp.minimum((i + 1) * per_step, n_step * per_step - 1), 0)),
            mat, vec, vec,
            pl.BlockSpec((CONV_K, GROUP_W), const2), vec, vec, vec, mat,
            vec, vec,
            pl.BlockSpec((SGU_GROUPS, SGU_CHUNK, SGU_CHUNK), lambda b, i: (0, 0, 0)),
            pl.BlockSpec((SGU_CHUNK, GROUP_W), const2),
            pl.BlockSpec((None, step, D), tok),
            pl.BlockSpec((None, step, GROUP_W), tok),
            pl.BlockSpec((D, D), const2),
            pl.BlockSpec((None, 1, D), row),
            pl.BlockSpec((1, D), const2),
            pl.BlockSpec((None, 1, D), row),
            pl.BlockSpec((None, 1, D), row),
            pl.BlockSpec((2 * D, LANES), const2),
        ],
        out_specs=[
            pl.BlockSpec((None, step, D), tok),
            pl.BlockSpec((None, step, D), tok),
            pl.BlockSpec((None, N_EXPERTS, step), lambda b, i: (b, 0, i)),
        ],
        out_shape=[
            jax.ShapeDtypeStruct((B, T, D), F32),
            jax.ShapeDtypeStruct((B, T, D), BF16),
            jax.ShapeDtypeStruct((B, N_EXPERTS, T), F32),
        ],
        scratch_shapes=[pltpu.VMEM((SUBLANES, step + 2 * HALO, GROUP_W), F32)] * 2,
        compiler_params=_params(("arbitrary", "arbitrary")),
        name="mixers_outproj",
    )(rest, rest, rest, *mix_w, x, attn, w_bf, gt, g2, sc, sh, wr_cat)


def _lane_prefix(x, tri):
    outs = []
    off = jnp.zeros((x.shape[0], 1), F32)
    for j in range(x.shape[1] // LANES):
        xb = x[:, j * LANES:(j + 1) * LANES]
        inc = _dot(xb.astype(BF16), tri)
        outs.append(inc - xb + off)
        off = off + inc[:, LANES - 1:LANES]
    return jnp.concatenate(outs, axis=1)


def _route_kernel(a_ref, tri_ref, pos_ref, gate_ref, ptok_ref, *, cap, n_expert):
    a = a_ref[...]
    R, T = a.shape

    def enough(c):
        return jnp.sum(jnp.where(a >= c, 1.0, 0.0), axis=-1, keepdims=True) >= cap

    def bit_search(i, lo):
        cand = lo | lax.shift_left(jnp.int32(1), 30 - i)
        return jnp.where(enough(pltpu.bitcast(cand, F32)), cand, lo)

    lo_bits = lax.fori_loop(0, 31, bit_search, jnp.zeros((R, 1), jnp.int32))

    def refine(i, c):
        lo, hi = c
        mid = lo + 0.5 * (hi - lo)
        ok = enough(mid)
        return jnp.where(ok, mid, lo), jnp.where(ok, hi, mid)

    thr, _ = lax.fori_loop(0, SUB_ULP_STEPS, refine,
                           (pltpu.bitcast(lo_bits, F32), pltpu.bitcast(lo_bits + 1, F32)))
    gt = a > thr
    eq = a == thr
    need = cap - jnp.sum(jnp.where(gt, 1.0, 0.0), axis=-1, keepdims=True)
    tri = tri_ref[...]
    tie_rank = _lane_prefix(jnp.where(eq, 1.0, 0.0), tri)
    sel = jnp.logical_or(gt, jnp.logical_and(eq, tie_rank < need))
    pos = jnp.where(sel, _lane_prefix(jnp.where(sel, 1.0, 0.0), tri), -1.0)
    pos_ref[...] = pos
    gate_ref[...] = jnp.where(sel, a, 0.0)
    fill = jnp.full((LANES - n_expert, T), -1.0, F32)
    for b in range(R // n_expert):
        padded = jnp.concatenate([pos[b * n_expert:(b + 1) * n_expert], fill], axis=0)
        ptok_ref[b] = jnp.transpose(padded)


def _route(aff_t, tri, cap):
    B, E, T = aff_t.shape
    blk = pl.BlockSpec((B * E, T), lambda i: (0, 0))
    pos, gate, ptok = pl.pallas_call(
        functools.partial(_route_kernel, cap=cap, n_expert=E),
        grid=(1,),
        in_specs=[blk, pl.BlockSpec((LANES, LANES), lambda i: (0, 0))],
        out_specs=[blk, blk, pl.BlockSpec((B, T, LANES), lambda i: (0, 0, 0))],
        out_shape=[
            jax.ShapeDtypeStruct((B * E, T), F32),
            jax.ShapeDtypeStruct((B * E, T), F32),
            jax.ShapeDtypeStruct((B, T, LANES), F32),
        ],
        compiler_params=_params(("arbitrary",)),
        name="route",
    )(aff_t.reshape(B * E, T), tri)
    return pos.reshape(B, E, T), gate.reshape(B, E, T), ptok


def _gather_kernel(pos_ref, gate_ref, h_ref, xs_ref, gs_ref, *, cap):
    n_exp = xs_ref.shape[0]
    first = pl.program_id(1) * n_exp
    T = pos_ref.shape[1]
    slot = lax.broadcasted_iota(jnp.int32, (cap, T), 0).astype(F32)
    for j in range(n_exp):
        prow = pos_ref[pl.ds(first + j, 1), :]
        grow = gate_ref[pl.ds(first + j, 1), :]
        hit = slot == prow
        onehot = jnp.where(hit, 1.0, 0.0).astype(BF16)
        xs_ref[j] = _dot(onehot, h_ref[...]).astype(BF16)
        gs_ref[j] = jnp.sum(jnp.where(hit, grow, 0.0), axis=-1, keepdims=True)


def _gather(pos_t, gate_t, h, n_exp):
    B, E, T = pos_t.shape
    D = h.shape[-1]
    cap = EC_CAPACITY * T // E
    et = pl.BlockSpec((None, E, T), lambda b, e: (b, 0, 0))
    return pl.pallas_call(
        functools.partial(_gather_kernel, cap=cap),
        grid=(B, E // n_exp),
        in_specs=[et, et, pl.BlockSpec((None, T, D), lambda b, e: (b, 0, 0))],
        out_specs=[
            pl.BlockSpec((n_exp, cap, D), lambda b, e: (e, b, 0)),
            pl.BlockSpec((n_exp, cap, 1), lambda b, e: (e, b, 0)),
        ],
        out_shape=[
            jax.ShapeDtypeStruct((E, B * cap, D), BF16),
            jax.ShapeDtypeStruct((E, B * cap, 1), F32),
        ],
        compiler_params=_params(("arbitrary", "arbitrary")),
        name="gather",
    )(pos_t, gate_t, h)


def _moe_kernel(*refs, n_grp, tm):
    xs = refs[0:2 * n_grp:2]
    gs = refs[1:2 * n_grp:2]
    wg_ref, wu_ref, wd_ref = refs[2 * n_grp:2 * n_grp + 3]
    ys = refs[2 * n_grp + 3:3 * n_grp + 3]
    accs = refs[3 * n_grp + 3:4 * n_grp + 3]
    wgb, wub, wdb = refs[4 * n_grp + 3:]
    f = pl.program_id(1)

    @pl.when(jnp.logical_and(pl.program_id(0) == 0, f == 0))
    def _():
        for acc in accs:
            acc[...] = jnp.zeros_like(acc)

    wgb[...] = wg_ref[...].astype(BF16)
    wub[...] = wu_ref[...].astype(BF16)
    wdb[...] = wd_ref[...].astype(BF16)
    carry_on = f > 0

    for x_ref, g_ref, y_ref, acc in zip(xs, gs, ys, accs):
        rows = x_ref.shape[0]
        t = min(tm, rows)
        for r in range(rows // t):
            sl = slice(r * t, (r + 1) * t)
            xt = x_ref[sl, :]
            a = _dot(xt, wgb[...])
            u = _dot(xt, wub[...])
            total = jnp.where(carry_on, acc[sl, :], 0.0) + _dot((_silu(a) * u).astype(BF16), wdb[...])
            acc[sl, :] = total
            y_ref[sl, :] = (total * g_ref[sl, :]).astype(BF16)


def _moe(groups, layer, w_gate, w_up, w_down, tf=512, tm=512):
    _, E, D, F = w_gate.shape
    in_specs, args, out_specs, out_shape, scratch = [], [], [], [], []
    for xs, gs in groups:
        R = xs.shape[1]
        in_specs += [pl.BlockSpec((None, R, D), lambda e, f: (e, 0, 0)), pl.BlockSpec((None, R, 1), lambda e, f: (e, 0, 0))]
        args += [xs, gs]
        out_specs.append(pl.BlockSpec((None, R, D), lambda e, f: (e, 0, 0)))
        out_shape.append(jax.ShapeDtypeStruct((E, R, D), BF16))
        scratch.append(pltpu.VMEM((R, D), F32))
    in_specs += [
        pl.BlockSpec((None, None, D, tf), lambda e, f: (layer, e, 0, f)),
        pl.BlockSpec((None, None, D, tf), lambda e, f: (layer, e, 0, f)),
        pl.BlockSpec((None, None, tf, D), lambda e, f: (layer, e, f, 0)),
    ]
    args += [w_gate, w_up, w_down]
    scratch += [pltpu.VMEM((D, tf), BF16), pltpu.VMEM((D, tf), BF16), pltpu.VMEM((tf, D), BF16)]
    return pl.pallas_call(
        functools.partial(_moe_kernel, n_grp=len(groups), tm=tm),
        grid=(E, F // tf),
        in_specs=in_specs,
        out_specs=out_specs,
        out_shape=out_shape,
        scratch_shapes=scratch,
        compiler_params=_params(("arbitrary", "arbitrary")),
        name="expert_ffn",
    )(*args)


def _combine_kernel(x_ref, gt_ref, ptok_ref, ys_ref, o_ref, *, tm):
    T, D = x_ref.shape
    E, cap, _ = ys_ref.shape
    t = min(tm, T)
    lane = lax.broadcasted_iota(jnp.int32, (t, cap), 1).astype(F32)

    def tile(r, carry):
        sl = pl.ds(pl.multiple_of(r * t, t), t)
        pt = ptok_ref[sl, :]
        acc = jnp.zeros((t, D), F32)
        for e in range(E):
            onehot = jnp.where(pt[:, e:e + 1] == lane, 1.0, 0.0).astype(BF16)
            acc = acc + _dot(onehot, ys_ref[e])
        o_ref[sl, :] = x_ref[sl, :] + gt_ref[...] * acc
        return carry

    lax.fori_loop(0, T // t, tile, 0)


def _combine(x, gt, mod_row, ptok, ys, tm=256):
    B, T, D = x.shape
    E = ys.shape[0]
    cap = ys.shape[1] // B
    ys4 = ys.reshape(E, B, cap, D)
    return pl.pallas_call(
        functools.partial(_combine_kernel, tm=tm),
        grid=(B,),
        in_specs=[
            pl.BlockSpec((None, T, D), lambda b: (b, 0, 0)),
            pl.BlockSpec((None, 1, D), lambda b: (mod_row(b), 0, 0)),
            pl.BlockSpec((None, T, LANES), lambda b: (b, 0, 0)),
            pl.BlockSpec((E, None, cap, D), lambda b: (0, b, 0, 0)),
        ],
        out_specs=pl.BlockSpec((None, T, D), lambda b: (b, 0, 0)),
        out_shape=jax.ShapeDtypeStruct((B, T, D), F32),
        compiler_params=_params(("arbitrary",)),
        name="combine",
    )(x, gt, ptok, ys4)


def _rope_tables(T):
    rows = (jnp.arange(T) // GRID_W).astype(F32)
    cols = (jnp.arange(T) % GRID_W).astype(F32)
    n_freq = ATTN_D // 4
    inv = ROPE_THETA ** (-jnp.arange(n_freq, dtype=F32) / n_freq)
    ang_r, ang_c = rows[:, None] * inv, cols[:, None] * inv
    cos = jnp.concatenate([jnp.cos(ang_r)] * 2 + [jnp.cos(ang_c)] * 2, axis=1)
    sin = jnp.concatenate([-jnp.sin(ang_r), jnp.sin(ang_r), -jnp.sin(ang_c), jnp.sin(ang_c)], axis=1)
    reps = GROUP_W // ATTN_D
    return jnp.tile(cos, (1, reps)).astype(F32), jnp.tile(sin, (1, reps)).astype(F32)


def _segment_ones(width):
    i = jnp.arange(GROUP_W) // width
    return (i[:, None] == i[None, :]).astype(BF16)


def _block_diag(w):
    G, n, _ = w.shape
    eye = jnp.eye(G, dtype=w.dtype)
    return (eye[:, None, :, None] * w[:, :, None, :]).reshape(G * n, G * n)


def kernel(x, c, ctx, c_ctx, w_mod, b_mod, g_norm1, g_norm2, w_in, w_out, g_q, g_k, lam_q1, lam_k1, lam_q2, lam_k2,
           g_attn_out, w_pool, b_pool, pool_scale, conv_w, conv_b, conv_ln_g, conv_ln_b, w_pw2, sgu_ln_g, sgu_ln_b,
           w_spatial, b_spatial, w_router, w_gate, w_up, w_down):
    B, T, D = x.shape
    C = ctx.shape[1]
    L = w_mod.shape[0]
    assert B < MOD_ROWS and D == D_MODEL and T % TIME_CHUNK == 0 and C % TIME_CHUNK == 0

    cond = jnp.zeros((MOD_ROWS, D), F32).at[:B].set(c).at[B].set(c_ctx)
    mods = _modulation(cond, w_mod, b_mod).reshape(L, MOD_ROWS, N_MOD, 1, D)
    lat_row = lambda b: b
    ctx_row = lambda b: B

    cos_x, sin_x = _rope_tables(T)
    cos_c, sin_c = jnp.ones((C, GROUP_W), F32), jnp.zeros((C, GROUP_W), F32)
    seg32 = _segment_ones(ATTN_D)
    tri = (jnp.arange(LANES)[:, None] <= jnp.arange(LANES)[None, :]).astype(BF16)
    tile_g = lambda g: jnp.tile(g, GROUP_W // g.shape[0]).reshape(1, GROUP_W)
    vec = lambda v: v.reshape(1, -1)

    for l in range(L):
        last = l == L - 1
        lam_init = 0.8 - 0.6 * math.exp(-0.3 * l)
        sh1, sc1, gt1, sh2, sc2, gt2 = (mods[l, :, i] for i in range(N_MOD))
        w_in_bf = w_in[l].astype(BF16)
        w_out_bf = w_out[l].astype(BF16)
        wr = jnp.zeros((D, LANES), F32).at[:, :N_EXPERTS].set(w_router[l])
        wr_hi = wr.astype(BF16)
        wr_cat = jnp.concatenate([wr_hi, (wr - wr_hi.astype(F32)).astype(BF16)], axis=0)
        lams = [vec(p[l]) for p in (lam_q1, lam_k1, lam_q2, lam_k2)]
        gq_t, gk_t, go_t = tile_g(g_q[l]), tile_g(g_k[l]), tile_g(g_attn_out[l])
        mix_w = (_block_diag(w_pool[l]).astype(BF16), vec(b_pool[l]), vec(pool_scale[l]), conv_w[l], vec(conv_b[l]),
                 vec(conv_ln_g[l]), vec(conv_ln_b[l]), w_pw2[l].astype(BF16), vec(sgu_ln_g[l]), vec(sgu_ln_b[l]),
                 w_spatial[l].astype(BF16), jnp.repeat(b_spatial[l].T, SGU_GW, axis=1))
        g1, g2 = vec(g_norm1[l]), vec(g_norm2[l])

        qx, kx, vx, rx = _inproj(x, lat_row, g1, sc1, sh1, w_in_bf, gq_t, gk_t, cos_x, sin_x, seg32, tm=1024, sub=512)
        qc, kc, vc, rc = _inproj(ctx, ctx_row, g1, sc1, sh1, w_in_bf, gq_t, gk_t, cos_c, sin_c, seg32, tm=256)

        attn_x = _attention(qx, [(kc, vc), (kx, vx)], lams, go_t, gq_t, gk_t, lam_init, tq=512)
        x, h2x, aff_x = _mixout(x, attn_x, rx, lat_row, mix_w, w_out_bf, gt1, g2, sc2, sh2, wr_cat)
        pos_x, gate_x, ptok_x = _route(aff_x, tri, EC_CAPACITY * T // N_EXPERTS)
        groups = [_gather(pos_x, gate_x, h2x, n_exp=8)]

        if not last:
            attn_c = _attention(qc, [(kc, vc)], lams, go_t, gq_t, gk_t, lam_init, tq=256)
            ctx, h2c, aff_c = _mixout(ctx, attn_c, rc, ctx_row, mix_w, w_out_bf, gt1, g2, sc2, sh2, wr_cat)
            pos_c, gate_c, ptok_c = _route(aff_c, tri, EC_CAPACITY * C // N_EXPERTS)
            groups.append(_gather(pos_c, gate_c, h2c, n_exp=N_EXPERTS))

        ys = _moe(groups, l, w_gate, w_up, w_down)
        x = _combine(x, gt2, lat_row, ptok_x, ys[0])
        if not last:
            ctx = _combine(ctx, gt2, ctx_row, ptok_c, ys[1])
    return x
```

```python
import functools
import math

import jax
import jax.numpy as jnp
from jax import lax
from jax.experimental import pallas as pl
from jax.experimental.pallas import tpu as pltpu

F32 = jnp.float32
BF16 = jnp.bfloat16

D_MODEL = 1024
GRID_W = 64
GROUP_W = 256
ATTN_HEADS = 4
ATTN_D = 32
ATTN_VD = 64
ROPE_THETA = 10000.0
POOL_WINDOWS = (2, 4, 8, 16)
CONV_K = 31
SGU_CHUNK = 128
SGU_GROUPS = 4
SGU_GW = 64
N_EXPERTS = 16
EC_CAPACITY = 2
N_MOD = 6
IN_W = 2048
EPS = 1e-6
LOG2E = 1.4426950408889634
MAX_UNSTABILISED_SCORE = 60.0
SCORE_BOUND_SLACK = 1.02
SUB_ULP_STEPS = 12

MOD_ROWS = 16
LANES = 128
SUBLANES = 8
HALO = 16
TIME_CHUNK = 256
VMEM_LIMIT = 56 * 1024 * 1024

MOD_COLS = 1024
INPROJ_ROWS = 1024
INPROJ_SUB = 512
ATTN_Q_ROWS = 512
MIX_ROWS = 4 * TIME_CHUNK
FFN_COLS = 512
FFN_ROWS = 512
COMBINE_ROWS = 512


def _params(sem):
    return pltpu.CompilerParams(dimension_semantics=sem, vmem_limit_bytes=VMEM_LIMIT)


def _dot(a, b):
    return jnp.dot(a, b, preferred_element_type=F32)


def _split_dot(a, w):
    hi = a.astype(BF16)
    lo = (a - hi.astype(F32)).astype(BF16)
    return _dot(hi, w) + _dot(lo, w)


def _silu(x):
    return x * jax.nn.sigmoid(x)


def _layer_norm(x, g, b):
    mu = jnp.mean(x, axis=-1, keepdims=True)
    xc = x - mu
    var = jnp.mean(xc * xc, axis=-1, keepdims=True)
    return xc * lax.rsqrt(var + EPS) * g + b


def _mod_kernel(c_ref, w_ref, b_ref, o_ref):
    s = _silu(c_ref[...])
    o_ref[...] = _dot(s.astype(BF16), w_ref[...].astype(BF16)) + b_ref[...]


def _modulation(cond, w_mod, b_mod):
    L, D, N = w_mod.shape
    tn = MOD_COLS
    return pl.pallas_call(
        _mod_kernel,
        grid=(L, N // tn),
        in_specs=[
            pl.BlockSpec((MOD_ROWS, D), lambda l, j: (0, 0)),
            pl.BlockSpec((None, D, tn), lambda l, j: (l, 0, j)),
            pl.BlockSpec((None, 1, tn), lambda l, j: (l, 0, j)),
        ],
        out_specs=pl.BlockSpec((None, MOD_ROWS, tn), lambda l, j: (l, 0, j)),
        out_shape=jax.ShapeDtypeStruct((L, MOD_ROWS, N), F32),
        compiler_params=_params(("arbitrary", "arbitrary")),
        name="modulation",
    )(cond, w_mod, b_mod.reshape(L, 1, N))


def _inproj_kernel(x_ref, g_ref, sc_ref, sh_ref, w_ref, gq_ref, gk_ref, cos_ref, sin_ref, seg_ref,
                   q_ref, k_ref, v_ref, r_ref, *, sub):
    gain = g_ref[...] * (1.0 + sc_ref[...])
    seg = seg_ref[...]
    n_freq = ATTN_D // 4
    first = (lax.broadcasted_iota(jnp.int32, (sub, GROUP_W), 1) % (2 * n_freq)) < n_freq

    for r in range(x_ref.shape[0] // sub):
        rows = slice(r * sub, (r + 1) * sub)
        x = x_ref[rows, :]
        h = x * lax.rsqrt(jnp.mean(x * x, axis=-1, keepdims=True) + EPS) * gain + sh_ref[...]
        p = _dot(h.astype(BF16), w_ref[...])
        cos = cos_ref[rows, :]
        sin = sin_ref[rows, :]

        def prep(a, g):
            ss = _split_dot(a * a, seg)
            n = a * lax.rsqrt(ss * (1.0 / ATTN_D) + EPS) * g
            partner = jnp.where(first, pltpu.roll(n, GROUP_W - n_freq, 1), pltpu.roll(n, n_freq, 1))
            return n * cos + partner * sin

        q_ref[rows, :] = (prep(p[:, 0:GROUP_W], gq_ref[...]) * (ATTN_D ** -0.5 * LOG2E)).astype(BF16)
        k_ref[:, rows] = jnp.transpose(prep(p[:, GROUP_W:2 * GROUP_W], gk_ref[...])).astype(BF16)
        v_ref[rows, :] = p[:, 2 * GROUP_W:3 * GROUP_W].astype(BF16)
        r_ref[rows, :] = p[:, 3 * GROUP_W:]


def _inproj(x, mod_row, g1, sc, sh, w_bf, gq_t, gk_t, cos_t, sin_t, seg32):
    B, T, D = x.shape
    tm = min(INPROJ_ROWS, T)
    rest_w = IN_W - 3 * GROUP_W
    row = lambda b, i: (mod_row(b), 0, 0)
    const2 = lambda b, i: (0, 0)
    tok = lambda b, i: (b, i, 0)
    return pl.pallas_call(
        functools.partial(_inproj_kernel, sub=min(INPROJ_SUB, tm)),
        grid=(B, T // tm),
        in_specs=[
            pl.BlockSpec((None, tm, D), tok),
            pl.BlockSpec((1, D), const2),
            pl.BlockSpec((None, 1, D), row),
            pl.BlockSpec((None, 1, D), row),
            pl.BlockSpec((D, IN_W), const2),
            pl.BlockSpec((1, GROUP_W), const2),
            pl.BlockSpec((1, GROUP_W), const2),
            pl.BlockSpec((tm, GROUP_W), lambda b, i: (i, 0)),
            pl.BlockSpec((tm, GROUP_W), lambda b, i: (i, 0)),
            pl.BlockSpec((GROUP_W, GROUP_W), const2),
        ],
        out_specs=[
            pl.BlockSpec((None, tm, GROUP_W), tok),
            pl.BlockSpec((None, GROUP_W, tm), lambda b, i: (b, 0, i)),
            pl.BlockSpec((None, tm, GROUP_W), tok),
            pl.BlockSpec((None, tm, rest_w), tok),
        ],
        out_shape=[
            jax.ShapeDtypeStruct((B, T, GROUP_W), BF16),
            jax.ShapeDtypeStruct((B, GROUP_W, T), BF16),
            jax.ShapeDtypeStruct((B, T, GROUP_W), BF16),
            jax.ShapeDtypeStruct((B, T, rest_w), F32),
        ],
        compiler_params=_params(("arbitrary", "arbitrary")),
        name="inproj",
    )(x, g1, sc, sh, w_bf, gq_t, gk_t, cos_t, sin_t, seg32)


def _attn_kernel(*refs, n_seg, lam_init):
    q_ref = refs[0]
    kv_refs = refs[1:1 + 2 * n_seg]
    lq1_ref, lk1_ref, lq2_ref, lk2_ref, go_ref, gq_ref, gk_ref, o_ref = refs[1 + 2 * n_seg:]
    lam = (jnp.exp(jnp.sum(lq1_ref[...] * lk1_ref[...], axis=-1, keepdims=True))
           - jnp.exp(jnp.sum(lq2_ref[...] * lk2_ref[...], axis=-1, keepdims=True)) + lam_init)
    q = q_ref[...]

    def scores(off):
        qs = q[:, off:off + ATTN_D]
        return [_dot(qs, kv_refs[2 * i][off:off + ATTN_D, :]) for i in range(n_seg)]

    def run(stabilise):
        def softmax_parts(s):
            if stabilise:
                mx = functools.reduce(jnp.maximum, [jnp.max(si, axis=-1, keepdims=True) for si in s])
                s = [si - mx for si in s]
            p = [jnp.exp2(si) for si in s]
            den = functools.reduce(jnp.add, [jnp.sum(pi, axis=-1, keepdims=True) for pi in p])
            return [pi.astype(BF16) for pi in p], den

        heads = []
        nxt = (scores(0), scores(ATTN_D))
        for h in range(ATTN_HEADS):
            cur = nxt
            if h + 1 < ATTN_HEADS:
                nxt = (scores((h + 1) * ATTN_VD), scores((h + 1) * ATTN_VD + ATTN_D))
            p1, l1 = softmax_parts(cur[0])
            p2, l2 = softmax_parts(cur[1])
            c = (lam * l1 / l2).astype(BF16)
            o = None
            for i in range(n_seg):
                oi = _dot(p1[i] - c * p2[i], kv_refs[2 * i + 1][:, h * ATTN_VD:(h + 1) * ATTN_VD])
                o = oi if o is None else o + oi
            o = o * (1.0 / l1)
            ms = jnp.mean(o * o, axis=-1, keepdims=True)
            heads.append(o * lax.rsqrt(ms + EPS))
        o_ref[...] = (jnp.concatenate(heads, axis=1) * go_ref[...] * (1.0 - lam_init)).astype(BF16)

    bound = (jnp.max(jnp.abs(gq_ref[...])) * jnp.max(jnp.abs(gk_ref[...]))
             * (ATTN_D * ATTN_D ** -0.5 * LOG2E * SCORE_BOUND_SLACK))
    small = bound < MAX_UNSTABILISED_SCORE

    @pl.when(small)
    def _():
        run(False)

    @pl.when(jnp.logical_not(small))
    def _():
        run(True)


def _attention(q, kvs, lams, go_t, gq_t, gk_t, lam_init):
    B, T, _ = q.shape
    tq = min(ATTN_Q_ROWS, T)
    tok = lambda b, i: (b, i, 0)
    const2 = lambda b, i: (0, 0)
    in_specs = [pl.BlockSpec((None, tq, GROUP_W), tok)]
    args = [q]
    for kt, v in kvs:
        n = v.shape[1]
        in_specs += [pl.BlockSpec((None, GROUP_W, n), lambda b, i: (b, 0, 0)),
                     pl.BlockSpec((None, n, GROUP_W), lambda b, i: (b, 0, 0))]
        args += [kt, v]
    in_specs += [pl.BlockSpec((1, ATTN_D), const2)] * 4
    in_specs += [pl.BlockSpec((1, GROUP_W), const2)] * 3
    args += list(lams) + [go_t, gq_t, gk_t]
    return pl.pallas_call(
        functools.partial(_attn_kernel, n_seg=len(kvs), lam_init=lam_init),
        grid=(B, T // tq),
        in_specs=in_specs,
        out_specs=pl.BlockSpec((None, tq, GROUP_W), tok),
        out_shape=jax.ShapeDtypeStruct((B, T, GROUP_W), BF16),
        compiler_params=_params(("arbitrary", "arbitrary")),
        name="diff_attention",
    )(*args)


def _gelu_tanh(x):
    return 0.5 * x * (1.0 + jnp.tanh(0.7978845608028654 * (x + 0.044715 * (x * x * x))))


def _mixout_kernel(cur_ref, prev_ref, next_ref, wpool_ref, bpool_ref, pscale_ref, convw_ref, convb_ref, clng_ref,
                   clnb_ref, wpw2_ref, slng_ref, slnb_ref, wsp_ref, bsp_ref,
                   x_ref, a_ref, w_ref, gt_ref, g2_ref, sc_ref, sh_ref, wr_ref,
                   xo_ref, h_ref, aff_ref, zwin, ywin, *, T):
    i = pl.program_id(1)
    has_prev = jnp.where(i > 0, 1.0, 0.0)
    has_next = jnp.where(i < pl.num_programs(1) - 1, 1.0, 0.0)

    def glu(ref):
        return ref[:, GROUP_W:2 * GROUP_W] * jax.nn.sigmoid(ref[:, 2 * GROUP_W:3 * GROUP_W])

    step = cur_ref.shape[0]
    win_rows = step + 2 * HALO

    def fill(win, head, body, tail):
        win[0, 0:HALO, :] = head * has_prev
        win[0, HALO:HALO + step, :] = body
        win[0, HALO + step:, :] = tail * has_next
        whole = win[0]
        for r in range(1, SUBLANES):
            win[r] = pltpu.roll(whole, win_rows - r, 0)

    def shifted(win, row):
        base = row - row % SUBLANES
        return win[row % SUBLANES, base:base + TIME_CHUNK, :]

    fill(zwin, prev_ref[:, 0:GROUP_W], cur_ref[:, 0:GROUP_W], next_ref[:, 0:GROUP_W])
    fill(ywin, glu(prev_ref), glu(cur_ref), glu(next_ref))

    pool_group = lax.broadcasted_iota(jnp.int32, (TIME_CHUNK, GROUP_W), 1) // SGU_GW
    sgu_masks = [((lax.broadcasted_iota(jnp.int32, (1, GROUP_W), 1) // SGU_GW) == g).astype(F32)
                 for g in range(SGU_GROUPS)]
    gain = g2_ref[...] * (1.0 + sc_ref[...])

    for c in range(step // TIME_CHUNK):
        first = c * TIME_CHUNK
        rows = slice(first, first + TIME_CHUNK)

        def zs(off):
            return shifted(zwin, first + HALO + off)

        t = i * step + first + lax.broadcasted_iota(jnp.int32, (TIME_CHUNK, 1), 0)
        z0 = zs(0)
        sums = []
        acc = None
        for w in POOL_WINDOWS:
            for off in range(-(w // 2), w // 2):
                if acc is None or not (-(w // 4) <= off < w // 4):
                    acc = zs(off) if acc is None else acc + zs(off)
            sums.append(acc)
        mean = None
        for g, w in enumerate(POOL_WINDOWS):
            cnt = (jnp.minimum(t + w // 2, T) - jnp.maximum(t - w // 2, 0)).astype(F32)
            mg = sums[g] / cnt
            mean = mg if mean is None else jnp.where(pool_group == g, mg, mean)
        pool = (_dot((mean - z0).astype(BF16), wpool_ref[...]) + bpool_ref[...]) * pscale_ref[...]
        mix = _dot(a_ref[rows, :], w_ref[0:GROUP_W, :]) + _dot(pool.astype(BF16), w_ref[GROUP_W:2 * GROUP_W, :])

        acc = None
        for k in range(CONV_K):
            term = shifted(ywin, first + HALO + k - CONV_K // 2) * convw_ref[k:k + 1, :]
            acc = term if acc is None else acc + term
        cn = _silu(_layer_norm(acc + convb_ref[...], clng_ref[...], clnb_ref[...]))
        conv = _dot(cn.astype(BF16), wpw2_ref[...])
        mix = mix + _dot(conv.astype(BF16), w_ref[2 * GROUP_W:3 * GROUP_W, :])

        gl = _gelu_tanh(cur_ref[rows, 3 * GROUP_W:5 * GROUP_W])
        u = gl[:, 0:GROUP_W]
        vn = _layer_norm(gl[:, GROUP_W:2 * GROUP_W], slng_ref[...], slnb_ref[...])
        gated = []
        for j in range(TIME_CHUNK // SGU_CHUNK):
            sub = slice(j * SGU_CHUNK, (j + 1) * SGU_CHUNK)
            s = bsp_ref[...]
            for g in range(SGU_GROUPS):
                s = s + _dot(wsp_ref[g], (vn[sub] * sgu_masks[g]).astype(BF16))
            gated.append((u[sub] * s).astype(BF16))
        mix = mix + _dot(jnp.concatenate(gated, axis=0), w_ref[3 * GROUP_W:, :])

        xn = x_ref[rows, :] + gt_ref[...] * mix
        xo_ref[rows, :] = xn
        h = xn * lax.rsqrt(jnp.mean(xn * xn, axis=-1, keepdims=True) + EPS) * gain + sh_ref[...]
        hi = h.astype(BF16)
        h_ref[rows, :] = hi
        lo = (h - hi.astype(F32)).astype(BF16)
        d = hi.shape[1]
        logits = _dot(hi, wr_ref[0:d, :]) + _dot(hi, wr_ref[d:2 * d, :]) + _dot(lo, wr_ref[0:d, :])
        lt = jnp.transpose(logits)[0:N_EXPERTS, :]
        e = jnp.exp(lt - jnp.max(lt, axis=0, keepdims=True))
        aff_ref[:, rows] = e / jnp.sum(e, axis=0, keepdims=True)


def _mixout(x, attn, rest, mod_row, mix_w, w_bf, gt, g2, sc, sh, wr_cat):
    B, T, D = x.shape
    RW = rest.shape[-1]
    step = min(MIX_ROWS, T)
    n_step = T // step
    per_step = step // HALO
    const2 = lambda b, i: (0, 0)
    row = lambda b, i: (mod_row(b), 0, 0)
    tok = lambda b, i: (b, i, 0)
    vec = pl.BlockSpec((1, GROUP_W), const2)
    mat = pl.BlockSpec((GROUP_W, GROUP_W), const2)
    return pl.pallas_call(
        functools.partial(_mixout_kernel, T=T),
        grid=(B, n_step),
        in_specs=[
            pl.BlockSpec((None, step, RW), tok),
            pl.BlockSpec((None, HALO, RW), lambda b, i: (b, jnp.maximum(i * per_step - 1, 0), 0)),
            pl.BlockSpec((None, HALO, RW), lambda b, i: (b, jnp.minimum((i + 1) * per_step, n_step * per_step - 1), 0)),
            mat, vec, vec,
            pl.BlockSpec((CONV_K, GROUP_W), const2), vec, vec, vec, mat,
            vec, vec,
            pl.BlockSpec((SGU_GROUPS, SGU_CHUNK, SGU_CHUNK), lambda b, i: (0, 0, 0)),
            pl.BlockSpec((SGU_CHUNK, GROUP_W), const2),
            pl.BlockSpec((None, step, D), tok),
            pl.BlockSpec((None, step, GROUP_W), tok),
            pl.BlockSpec((D, D), const2),
            pl.BlockSpec((None, 1, D), row),
            pl.BlockSpec((1, D), const2),
            pl.BlockSpec((None, 1, D), row),
            pl.BlockSpec((None, 1, D), row),
            pl.BlockSpec((2 * D, LANES), const2),
        ],
        out_specs=[
            pl.BlockSpec((None, step, D), tok),
            pl.BlockSpec((None, step, D), tok),
            pl.BlockSpec((None, N_EXPERTS, step), lambda b, i: (b, 0, i)),
        ],
        out_shape=[
            jax.ShapeDtypeStruct((B, T, D), F32),
            jax.ShapeDtypeStruct((B, T, D), BF16),
            jax.ShapeDtypeStruct((B, N_EXPERTS, T), F32),
        ],
        scratch_shapes=[pltpu.VMEM((SUBLANES, step + 2 * HALO, GROUP_W), F32)] * 2,
        compiler_params=_params(("arbitrary", "arbitrary")),
        name="mixers_outproj",
    )(rest, rest, rest, *mix_w, x, attn, w_bf, gt, g2, sc, sh, wr_cat)


def _lane_prefix(x, tri):
    outs = []
    off = jnp.zeros((x.shape[0], 1), F32)
    for j in range(x.shape[1] // LANES):
        xb = x[:, j * LANES:(j + 1) * LANES]
        inc = _dot(xb.astype(BF16), tri)
        outs.append(inc - xb + off)
        off = off + inc[:, LANES - 1:LANES]
    return jnp.concatenate(outs, axis=1)


def _route_kernel(a_ref, tri_ref, pos_ref, gate_ref, ptok_ref, *, cap, n_expert):
    a = a_ref[...]
    R, T = a.shape

    def enough(c):
        return jnp.sum(jnp.where(a >= c, 1.0, 0.0), axis=-1, keepdims=True) >= cap

    def bit_search(i, lo):
        cand = lo | lax.shift_left(jnp.int32(1), 30 - i)
        return jnp.where(enough(pltpu.bitcast(cand, F32)), cand, lo)

    lo_bits = lax.fori_loop(0, 31, bit_search, jnp.zeros((R, 1), jnp.int32))

    def refine(i, c):
        lo, hi = c
        mid = lo + 0.5 * (hi - lo)
        ok = enough(mid)
        return jnp.where(ok, mid, lo), jnp.where(ok, hi, mid)

    thr, _ = lax.fori_loop(0, SUB_ULP_STEPS, refine,
                           (pltpu.bitcast(lo_bits, F32), pltpu.bitcast(lo_bits + 1, F32)))
    gt = a > thr
    eq = a == thr
    need = cap - jnp.sum(jnp.where(gt, 1.0, 0.0), axis=-1, keepdims=True)
    tri = tri_ref[...]
    tie_rank = _lane_prefix(jnp.where(eq, 1.0, 0.0), tri)
    sel = jnp.logical_or(gt, jnp.logical_and(eq, tie_rank < need))
    pos = jnp.where(sel, _lane_prefix(jnp.where(sel, 1.0, 0.0), tri), -1.0)
    pos_ref[...] = pos
    gate_ref[...] = jnp.where(sel, a, 0.0)
    fill = jnp.full((LANES - n_expert, T), -1.0, F32)
    for b in range(R // n_expert):
        padded = jnp.concatenate([pos[b * n_expert:(b + 1) * n_expert], fill], axis=0)
        ptok_ref[b] = jnp.transpose(padded)


def _route(aff_t, tri, cap):
    B, E, T = aff_t.shape
    blk = pl.BlockSpec((B * E, T), lambda i: (0, 0))
    pos, gate, ptok = pl.pallas_call(
        functools.partial(_route_kernel, cap=cap, n_expert=E),
        grid=(1,),
        in_specs=[blk, pl.BlockSpec((LANES, LANES), lambda i: (0, 0))],
        out_specs=[blk, blk, pl.BlockSpec((B, T, LANES), lambda i: (0, 0, 0))],
        out_shape=[
            jax.ShapeDtypeStruct((B * E, T), F32),
            jax.ShapeDtypeStruct((B * E, T), F32),
            jax.ShapeDtypeStruct((B, T, LANES), F32),
        ],
        compiler_params=_params(("arbitrary",)),
        name="route",
    )(aff_t.reshape(B * E, T), tri)
    return pos.reshape(B, E, T), gate.reshape(B, E, T), ptok


def _gather_kernel(pos_ref, gate_ref, h_ref, xs_ref, gs_ref):
    E, cap, _ = xs_ref.shape
    T = pos_ref.shape[1]
    slot = lax.broadcasted_iota(jnp.int32, (cap, T), 0).astype(F32)
    for e in range(E):
        hit = slot == pos_ref[e:e + 1, :]
        onehot = jnp.where(hit, 1.0, 0.0).astype(BF16)
        xs_ref[e] = _dot(onehot, h_ref[...]).astype(BF16)
        gs_ref[e] = jnp.sum(jnp.where(hit, gate_ref[e:e + 1, :], 0.0), axis=-1, keepdims=True)


def _gather(pos_t, gate_t, h):
    B, E, T = pos_t.shape
    D = h.shape[-1]
    cap = EC_CAPACITY * T // E
    et = pl.BlockSpec((None, E, T), lambda b: (b, 0, 0))
    return pl.pallas_call(
        _gather_kernel,
        grid=(B,),
        in_specs=[et, et, pl.BlockSpec((None, T, D), lambda b: (b, 0, 0))],
        out_specs=[
            pl.BlockSpec((E, cap, D), lambda b: (0, b, 0)),
            pl.BlockSpec((E, cap, 1), lambda b: (0, b, 0)),
        ],
        out_shape=[
            jax.ShapeDtypeStruct((E, B * cap, D), BF16),
            jax.ShapeDtypeStruct((E, B * cap, 1), F32),
        ],
        compiler_params=_params(("arbitrary",)),
        name="gather",
    )(pos_t, gate_t, h)


def _moe_kernel(*refs, n_grp, tm):
    xs = refs[0:2 * n_grp:2]
    gs = refs[1:2 * n_grp:2]
    wg_ref, wu_ref, wd_ref = refs[2 * n_grp:2 * n_grp + 3]
    ys = refs[2 * n_grp + 3:3 * n_grp + 3]
    accs = refs[3 * n_grp + 3:4 * n_grp + 3]
    wgb, wub, wdb = refs[4 * n_grp + 3:]
    f = pl.program_id(1)

    @pl.when(jnp.logical_and(pl.program_id(0) == 0, f == 0))
    def _():
        for acc in accs:
            acc[...] = jnp.zeros_like(acc)

    wgb[...] = wg_ref[...].astype(BF16)
    wub[...] = wu_ref[...].astype(BF16)
    wdb[...] = wd_ref[...].astype(BF16)
    carry_on = f > 0

    for x_ref, g_ref, y_ref, acc in zip(xs, gs, ys, accs):
        rows = x_ref.shape[0]
        t = min(tm, rows)
        for r in range(rows // t):
            sl = slice(r * t, (r + 1) * t)
            xt = x_ref[sl, :]
            a = _dot(xt, wgb[...])
            u = _dot(xt, wub[...])
            total = jnp.where(carry_on, acc[sl, :], 0.0) + _dot((_silu(a) * u).astype(BF16), wdb[...])
            acc[sl, :] = total
            y_ref[sl, :] = (total * g_ref[sl, :]).astype(BF16)


def _moe(groups, layer, w_gate, w_up, w_down):
    _, E, D, F = w_gate.shape
    tf = FFN_COLS
    in_specs, args, out_specs, out_shape, scratch = [], [], [], [], []
    for xs, gs in groups:
        R = xs.shape[1]
        in_specs += [pl.BlockSpec((None, R, D), lambda e, f: (e, 0, 0)), pl.BlockSpec((None, R, 1), lambda e, f: (e, 0, 0))]
        args += [xs, gs]
        out_specs.append(pl.BlockSpec((None, R, D), lambda e, f: (e, 0, 0)))
        out_shape.append(jax.ShapeDtypeStruct((E, R, D), BF16))
        scratch.append(pltpu.VMEM((R, D), F32))
    in_specs += [
        pl.BlockSpec((None, None, D, tf), lambda e, f: (layer, e, 0, f)),
        pl.BlockSpec((None, None, D, tf), lambda e, f: (layer, e, 0, f)),
        pl.BlockSpec((None, None, tf, D), lambda e, f: (layer, e, f, 0)),
    ]
    args += [w_gate, w_up, w_down]
    scratch += [pltpu.VMEM((D, tf), BF16), pltpu.VMEM((D, tf), BF16), pltpu.VMEM((tf, D), BF16)]
    return pl.pallas_call(
        functools.partial(_moe_kernel, n_grp=len(groups), tm=FFN_ROWS),
        grid=(E, F // tf),
        in_specs=in_specs,
        out_specs=out_specs,
        out_shape=out_shape,
        scratch_shapes=scratch,
        compiler_params=_params(("arbitrary", "arbitrary")),
        name="expert_ffn",
    )(*args)


def _combine_kernel(x_ref, gt_ref, ptok_ref, ys_ref, o_ref, *, tm):
    T, D = x_ref.shape
    E, cap, _ = ys_ref.shape
    t = min(tm, T)
    lane = lax.broadcasted_iota(jnp.int32, (t, cap), 1).astype(F32)

    def tile(r, carry):
        sl = pl.ds(pl.multiple_of(r * t, t), t)
        pt = ptok_ref[sl, :]
        acc = jnp.zeros((t, D), F32)
        for e in range(E):
            onehot = jnp.where(pt[:, e:e + 1] == lane, 1.0, 0.0).astype(BF16)
            acc = acc + _dot(onehot, ys_ref[e])
        o_ref[sl, :] = x_ref[sl, :] + gt_ref[...] * acc
        return carry

    lax.fori_loop(0, T // t, tile, 0)


def _combine(x, gt, mod_row, ptok, ys):
    B, T, D = x.shape
    E = ys.shape[0]
    cap = ys.shape[1] // B
    ys4 = ys.reshape(E, B, cap, D)
    return pl.pallas_call(
        functools.partial(_combine_kernel, tm=COMBINE_ROWS),
        grid=(B,),
        in_specs=[
            pl.BlockSpec((None, T, D), lambda b: (b, 0, 0)),
            pl.BlockSpec((None, 1, D), lambda b: (mod_row(b), 0, 0)),
            pl.BlockSpec((None, T, LANES), lambda b: (b, 0, 0)),
            pl.BlockSpec((E, None, cap, D), lambda b: (0, b, 0, 0)),
        ],
        out_specs=pl.BlockSpec((None, T, D), lambda b: (b, 0, 0)),
        out_shape=jax.ShapeDtypeStruct((B, T, D), F32),
        compiler_params=_params(("arbitrary",)),
        name="combine",
    )(x, gt, ptok, ys4)


def _rope_tables(T):
    rows = (jnp.arange(T) // GRID_W).astype(F32)
    cols = (jnp.arange(T) % GRID_W).astype(F32)
    n_freq = ATTN_D // 4
    inv = ROPE_THETA ** (-jnp.arange(n_freq, dtype=F32) / n_freq)
    ang_r, ang_c = rows[:, None] * inv, cols[:, None] * inv
    cos = jnp.concatenate([jnp.cos(ang_r)] * 2 + [jnp.cos(ang_c)] * 2, axis=1)
    sin = jnp.concatenate([-jnp.sin(ang_r), jnp.sin(ang_r), -jnp.sin(ang_c), jnp.sin(ang_c)], axis=1)
    reps = GROUP_W // ATTN_D
    return jnp.tile(cos, (1, reps)).astype(F32), jnp.tile(sin, (1, reps)).astype(F32)


def _segment_ones(width):
    i = jnp.arange(GROUP_W) // width
    return (i[:, None] == i[None, :]).astype(BF16)


def _block_diag(w):
    G, n, _ = w.shape
    eye = jnp.eye(G, dtype=w.dtype)
    return (eye[:, None, :, None] * w[:, :, None, :]).reshape(G * n, G * n)


def kernel(x, c, ctx, c_ctx, w_mod, b_mod, g_norm1, g_norm2, w_in, w_out, g_q, g_k, lam_q1, lam_k1, lam_q2, lam_k2,
           g_attn_out, w_pool, b_pool, pool_scale, conv_w, conv_b, conv_ln_g, conv_ln_b, w_pw2, sgu_ln_g, sgu_ln_b,
           w_spatial, b_spatial, w_router, w_gate, w_up, w_down):
    B, T, D = x.shape
    C = ctx.shape[1]
    L = w_mod.shape[0]
    assert B < MOD_ROWS and D == D_MODEL and T % TIME_CHUNK == 0 and C % TIME_CHUNK == 0

    cond = jnp.zeros((MOD_ROWS, D), F32).at[:B].set(c).at[B].set(c_ctx)
    mods = _modulation(cond, w_mod, b_mod).reshape(L, MOD_ROWS, N_MOD, 1, D)
    lat_row = lambda b: b
    ctx_row = lambda b: B

    cos_x, sin_x = _rope_tables(T)
    cos_c, sin_c = jnp.ones((C, GROUP_W), F32), jnp.zeros((C, GROUP_W), F32)
    seg32 = _segment_ones(ATTN_D)
    tri = (jnp.arange(LANES)[:, None] <= jnp.arange(LANES)[None, :]).astype(BF16)
    tile_g = lambda g: jnp.tile(g, GROUP_W // g.shape[0]).reshape(1, GROUP_W)
    vec = lambda v: v.reshape(1, -1)

    for l in range(L):
        last = l == L - 1
        lam_init = 0.8 - 0.6 * math.exp(-0.3 * l)
        sh1, sc1, gt1, sh2, sc2, gt2 = (mods[l, :, i] for i in range(N_MOD))
        w_in_bf = w_in[l].astype(BF16)
        w_out_bf = w_out[l].astype(BF16)
        wr = jnp.zeros((D, LANES), F32).at[:, :N_EXPERTS].set(w_router[l])
        wr_hi = wr.astype(BF16)
        wr_cat = jnp.concatenate([wr_hi, (wr - wr_hi.astype(F32)).astype(BF16)], axis=0)
        lams = [vec(p[l]) for p in (lam_q1, lam_k1, lam_q2, lam_k2)]
        gq_t, gk_t, go_t = tile_g(g_q[l]), tile_g(g_k[l]), tile_g(g_attn_out[l])
        mix_w = (_block_diag(w_pool[l]).astype(BF16), vec(b_pool[l]), vec(pool_scale[l]), conv_w[l], vec(conv_b[l]),
                 vec(conv_ln_g[l]), vec(conv_ln_b[l]), w_pw2[l].astype(BF16), vec(sgu_ln_g[l]), vec(sgu_ln_b[l]),
                 w_spatial[l].astype(BF16), jnp.repeat(b_spatial[l].T, SGU_GW, axis=1))
        g1, g2 = vec(g_norm1[l]), vec(g_norm2[l])

        qx, kx, vx, rx = _inproj(x, lat_row, g1, sc1, sh1, w_in_bf, gq_t, gk_t, cos_x, sin_x, seg32)
        qc, kc, vc, rc = _inproj(ctx, ctx_row, g1, sc1, sh1, w_in_bf, gq_t, gk_t, cos_c, sin_c, seg32)

        attn_x = _attention(qx, [(kc, vc), (kx, vx)], lams, go_t, gq_t, gk_t, lam_init)
        x, h2x, aff_x = _mixout(x, attn_x, rx, lat_row, mix_w, w_out_bf, gt1, g2, sc2, sh2, wr_cat)
        pos_x, gate_x, ptok_x = _route(aff_x, tri, EC_CAPACITY * T // N_EXPERTS)
        groups = [_gather(pos_x, gate_x, h2x)]

        if not last:
            attn_c = _attention(qc, [(kc, vc)], lams, go_t, gq_t, gk_t, lam_init)
            ctx, h2c, aff_c = _mixout(ctx, attn_c, rc, ctx_row, mix_w, w_out_bf, gt1, g2, sc2, sh2, wr_cat)
            pos_c, gate_c, ptok_c = _route(aff_c, tri, EC_CAPACITY * C // N_EXPERTS)
            groups.append(_gather(pos_c, gate_c, h2c))

        ys = _moe(groups, l, w_gate, w_up, w_down)
        x = _combine(x, gt2, lat_row, ptok_x, ys[0])
        if not last:
            ctx = _combine(ctx, gt2, ctx_row, ptok_c, ys[1])
    return x
```

```python
import functools
import math

import jax
import jax.numpy as jnp
from jax import lax
from jax.experimental import pallas as pl
from jax.experimental.pallas import tpu as pltpu

F32 = jnp.float32
BF16 = jnp.bfloat16

D_MODEL = 1024
GRID_W = 64
GROUP_W = 256
ATTN_HEADS = 4
ATTN_D = 32
ATTN_VD = 64
ROPE_THETA = 10000.0
POOL_WINDOWS = (2, 4, 8, 16)
CONV_K = 31
SGU_CHUNK = 128
SGU_GROUPS = 4
SGU_GW = 64
N_EXPERTS = 16
EC_CAPACITY = 2
N_MOD = 6
IN_W = 2048
EPS = 1e-6
LOG2E = 1.4426950408889634
MAX_UNSTABILISED_SCORE = 40.0
MAX_UNSTABILISED_LAM = 2.0 ** 30
SCORE_BOUND_SLACK = 1.02
SUB_ULP_STEPS = 12

MOD_ROWS = 16
LANES = 128
SUBLANES = 8
HALO = 16
TIME_CHUNK = 256
VMEM_LIMIT = 56 * 1024 * 1024

MOD_COLS = 1024
INPROJ_ROWS = 1024
INPROJ_SUB = 512
ATTN_Q_ROWS = 512
MIX_ROWS = 4 * TIME_CHUNK
FFN_COLS = 512
FFN_ROWS = 512
COMBINE_ROWS = 512


def _params(sem):
    return pltpu.CompilerParams(dimension_semantics=sem, vmem_limit_bytes=VMEM_LIMIT)


def _dot(a, b):
    return jnp.dot(a, b, preferred_element_type=F32)


def _split_dot(a, w):
    hi = a.astype(BF16)
    lo = (a - hi.astype(F32)).astype(BF16)
    return _dot(hi, w) + _dot(lo, w)


def _silu(x):
    return x * jax.nn.sigmoid(x)


def _layer_norm(x, g, b):
    mu = jnp.mean(x, axis=-1, keepdims=True)
    xc = x - mu
    var = jnp.mean(xc * xc, axis=-1, keepdims=True)
    return xc * lax.rsqrt(var + EPS) * g + b


def _mod_kernel(c_ref, w_ref, b_ref, o_ref):
    s = _silu(c_ref[...])
    o_ref[...] = _dot(s.astype(BF16), w_ref[...].astype(BF16)) + b_ref[...]


def _modulation(cond, w_mod, b_mod):
    L, D, N = w_mod.shape
    tn = MOD_COLS
    return pl.pallas_call(
        _mod_kernel,
        grid=(L, N // tn),
        in_specs=[
            pl.BlockSpec((MOD_ROWS, D), lambda l, j: (0, 0)),
            pl.BlockSpec((None, D, tn), lambda l, j: (l, 0, j)),
            pl.BlockSpec((None, 1, tn), lambda l, j: (l, 0, j)),
        ],
        out_specs=pl.BlockSpec((None, MOD_ROWS, tn), lambda l, j: (l, 0, j)),
        out_shape=jax.ShapeDtypeStruct((L, MOD_ROWS, N), F32),
        compiler_params=_params(("arbitrary", "arbitrary")),
        name="modulation",
    )(cond, w_mod, b_mod.reshape(L, 1, N))


def _inproj_kernel(x_ref, g_ref, sc_ref, sh_ref, w_ref, gq_ref, gk_ref, cos_ref, sin_ref, seg_ref,
                   q_ref, k_ref, v_ref, r_ref, *, sub):
    gain = g_ref[...] * (1.0 + sc_ref[...])
    seg = seg_ref[...]
    n_freq = ATTN_D // 4
    first = (lax.broadcasted_iota(jnp.int32, (sub, GROUP_W), 1) % (2 * n_freq)) < n_freq

    for r in range(x_ref.shape[0] // sub):
        rows = slice(r * sub, (r + 1) * sub)
        x = x_ref[rows, :]
        h = x * lax.rsqrt(jnp.mean(x * x, axis=-1, keepdims=True) + EPS) * gain + sh_ref[...]
        p = _dot(h.astype(BF16), w_ref[...])
        cos = cos_ref[rows, :]
        sin = sin_ref[rows, :]

        def prep(a, g):
            ss = _split_dot(a * a, seg)
            n = a * lax.rsqrt(ss * (1.0 / ATTN_D) + EPS) * g
            partner = jnp.where(first, pltpu.roll(n, GROUP_W - n_freq, 1), pltpu.roll(n, n_freq, 1))
            return n * cos + partner * sin

        q_ref[rows, :] = (prep(p[:, 0:GROUP_W], gq_ref[...]) * (ATTN_D ** -0.5 * LOG2E)).astype(BF16)
        k_ref[:, rows] = jnp.transpose(prep(p[:, GROUP_W:2 * GROUP_W], gk_ref[...])).astype(BF16)
        v_ref[rows, :] = p[:, 2 * GROUP_W:3 * GROUP_W].astype(BF16)
        r_ref[rows, :] = p[:, 3 * GROUP_W:]


def _inproj(x, mod_row, g1, sc, sh, w_bf, gq_t, gk_t, cos_t, sin_t, seg32):
    B, T, D = x.shape
    tm = min(INPROJ_ROWS, T)
    rest_w = IN_W - 3 * GROUP_W
    row = lambda b, i: (mod_row(b), 0, 0)
    const2 = lambda b, i: (0, 0)
    tok = lambda b, i: (b, i, 0)
    return pl.pallas_call(
        functools.partial(_inproj_kernel, sub=min(INPROJ_SUB, tm)),
        grid=(B, T // tm),
        in_specs=[
            pl.BlockSpec((None, tm, D), tok),
            pl.BlockSpec((1, D), const2),
            pl.BlockSpec((None, 1, D), row),
            pl.BlockSpec((None, 1, D), row),
            pl.BlockSpec((D, IN_W), const2),
            pl.BlockSpec((1, GROUP_W), const2),
            pl.BlockSpec((1, GROUP_W), const2),
            pl.BlockSpec((tm, GROUP_W), lambda b, i: (i, 0)),
            pl.BlockSpec((tm, GROUP_W), lambda b, i: (i, 0)),
            pl.BlockSpec((GROUP_W, GROUP_W), const2),
        ],
        out_specs=[
            pl.BlockSpec((None, tm, GROUP_W), tok),
            pl.BlockSpec((None, GROUP_W, tm), lambda b, i: (b, 0, i)),
            pl.BlockSpec((None, tm, GROUP_W), tok),
            pl.BlockSpec((None, tm, rest_w), tok),
        ],
        out_shape=[
            jax.ShapeDtypeStruct((B, T, GROUP_W), BF16),
            jax.ShapeDtypeStruct((B, GROUP_W, T), BF16),
            jax.ShapeDtypeStruct((B, T, GROUP_W), BF16),
            jax.ShapeDtypeStruct((B, T, rest_w), F32),
        ],
        compiler_params=_params(("arbitrary", "arbitrary")),
        name="inproj",
    )(x, g1, sc, sh, w_bf, gq_t, gk_t, cos_t, sin_t, seg32)


def _attn_kernel(*refs, n_seg, lam_init):
    q_ref = refs[0]
    kv_refs = refs[1:1 + 2 * n_seg]
    lq1_ref, lk1_ref, lq2_ref, lk2_ref, go_ref, gq_ref, gk_ref, o_ref = refs[1 + 2 * n_seg:]
    lam = (jnp.exp(jnp.sum(lq1_ref[...] * lk1_ref[...], axis=-1, keepdims=True))
           - jnp.exp(jnp.sum(lq2_ref[...] * lk2_ref[...], axis=-1, keepdims=True)) + lam_init)
    q = q_ref[...]

    def scores(off):
        qs = q[:, off:off + ATTN_D]
        return [_dot(qs, kv_refs[2 * i][off:off + ATTN_D, :]) for i in range(n_seg)]

    def run(stabilise):
        def softmax_parts(s):
            if stabilise:
                mx = functools.reduce(jnp.maximum, [jnp.max(si, axis=-1, keepdims=True) for si in s])
                s = [si - mx for si in s]
            p = [jnp.exp2(si) for si in s]
            den = functools.reduce(jnp.add, [jnp.sum(pi, axis=-1, keepdims=True) for pi in p])
            return [pi.astype(BF16) for pi in p], den

        heads = []
        nxt = (scores(0), scores(ATTN_D))
        for h in range(ATTN_HEADS):
            cur = nxt
            if h + 1 < ATTN_HEADS:
                nxt = (scores((h + 1) * ATTN_VD), scores((h + 1) * ATTN_VD + ATTN_D))
            p1, l1 = softmax_parts(cur[0])
            p2, l2 = softmax_parts(cur[1])
            c = (lam * l1 / l2).astype(BF16)
            o = None
            for i in range(n_seg):
                oi = _dot(p1[i] - c * p2[i], kv_refs[2 * i + 1][:, h * ATTN_VD:(h + 1) * ATTN_VD])
                o = oi if o is None else o + oi
            o = o * (1.0 / l1)
            ms = jnp.mean(o * o, axis=-1, keepdims=True)
            heads.append(o * lax.rsqrt(ms + EPS))
        o_ref[...] = (jnp.concatenate(heads, axis=1) * go_ref[...] * (1.0 - lam_init)).astype(BF16)

    bound = (jnp.max(jnp.abs(gq_ref[...])) * jnp.max(jnp.abs(gk_ref[...]))
             * (ATTN_D * ATTN_D ** -0.5 * LOG2E * SCORE_BOUND_SLACK))
    small = jnp.logical_and(bound < MAX_UNSTABILISED_SCORE, jnp.max(jnp.abs(lam)) < MAX_UNSTABILISED_LAM)

    @pl.when(small)
    def _():
        run(False)

    @pl.when(jnp.logical_not(small))
    def _():
        run(True)


def _attention(q, kvs, lams, go_t, gq_t, gk_t, lam_init):
    B, T, _ = q.shape
    tq = min(ATTN_Q_ROWS, T)
    tok = lambda b, i: (b, i, 0)
    const2 = lambda b, i: (0, 0)
    in_specs = [pl.BlockSpec((None, tq, GROUP_W), tok)]
    args = [q]
    for kt, v in kvs:
        n = v.shape[1]
        in_specs += [pl.BlockSpec((None, GROUP_W, n), lambda b, i: (b, 0, 0)),
                     pl.BlockSpec((None, n, GROUP_W), lambda b, i: (b, 0, 0))]
        args += [kt, v]
    in_specs += [pl.BlockSpec((1, ATTN_D), const2)] * 4
    in_specs += [pl.BlockSpec((1, GROUP_W), const2)] * 3
    args += list(lams) + [go_t, gq_t, gk_t]
    return pl.pallas_call(
        functools.partial(_attn_kernel, n_seg=len(kvs), lam_init=lam_init),
        grid=(B, T // tq),
        in_specs=in_specs,
        out_specs=pl.BlockSpec((None, tq, GROUP_W), tok),
        out_shape=jax.ShapeDtypeStruct((B, T, GROUP_W), BF16),
        compiler_params=_params(("arbitrary", "arbitrary")),
        name="diff_attention",
    )(*args)


def _gelu_tanh(x):
    return 0.5 * x * (1.0 + jnp.tanh(0.7978845608028654 * (x + 0.044715 * (x * x * x))))


def _mixout_kernel(cur_ref, prev_ref, next_ref, wpool_ref, bpool_ref, pscale_ref, convw_ref, convb_ref, clng_ref,
                   clnb_ref, wpw2_ref, slng_ref, slnb_ref, wsp_ref, bsp_ref,
                   x_ref, a_ref, w_ref, gt_ref, g2_ref, sc_ref, sh_ref, wr_ref,
                   xo_ref, h_ref, aff_ref, zwin, ywin, *, T):
    i = pl.program_id(1)
    has_prev = jnp.where(i > 0, 1.0, 0.0)
    has_next = jnp.where(i < pl.num_programs(1) - 1, 1.0, 0.0)

    def glu(ref):
        return ref[:, GROUP_W:2 * GROUP_W] * jax.nn.sigmoid(ref[:, 2 * GROUP_W:3 * GROUP_W])

    step = cur_ref.shape[0]
    win_rows = step + 2 * HALO

    def fill(win, head, body, tail):
        win[0, 0:HALO, :] = head * has_prev
        win[0, HALO:HALO + step, :] = body
        win[0, HALO + step:, :] = tail * has_next
        whole = win[0]
        for r in range(1, SUBLANES):
            win[r] = pltpu.roll(whole, win_rows - r, 0)

    def shifted(win, row):
        base = row - row % SUBLANES
        return win[row % SUBLANES, base:base + TIME_CHUNK, :]

    fill(zwin, prev_ref[:, 0:GROUP_W], cur_ref[:, 0:GROUP_W], next_ref[:, 0:GROUP_W])
    fill(ywin, glu(prev_ref), glu(cur_ref), glu(next_ref))

    pool_group = lax.broadcasted_iota(jnp.int32, (TIME_CHUNK, GROUP_W), 1) // SGU_GW
    sgu_masks = [((lax.broadcasted_iota(jnp.int32, (1, GROUP_W), 1) // SGU_GW) == g).astype(F32)
                 for g in range(SGU_GROUPS)]
    gain = g2_ref[...] * (1.0 + sc_ref[...])

    for c in range(step // TIME_CHUNK):
        first = c * TIME_CHUNK
        rows = slice(first, first + TIME_CHUNK)

        def zs(off):
            return shifted(zwin, first + HALO + off)

        t = i * step + first + lax.broadcasted_iota(jnp.int32, (TIME_CHUNK, 1), 0)
        z0 = zs(0)
        sums = []
        acc = None
        for w in POOL_WINDOWS:
            for off in range(-(w // 2), w // 2):
                if acc is None or not (-(w // 4) <= off < w // 4):
                    acc = zs(off) if acc is None else acc + zs(off)
            sums.append(acc)
        mean = None
        for g, w in enumerate(POOL_WINDOWS):
            cnt = (jnp.minimum(t + w // 2, T) - jnp.maximum(t - w // 2, 0)).astype(F32)
            mg = sums[g] / cnt
            mean = mg if mean is None else jnp.where(pool_group == g, mg, mean)
        pool = (_dot((mean - z0).astype(BF16), wpool_ref[...]) + bpool_ref[...]) * pscale_ref[...]
        mix = _dot(a_ref[rows, :], w_ref[0:GROUP_W, :]) + _dot(pool.astype(BF16), w_ref[GROUP_W:2 * GROUP_W, :])

        acc = None
        for k in range(CONV_K):
            term = shifted(ywin, first + HALO + k - CONV_K // 2) * convw_ref[k:k + 1, :]
            acc = term if acc is None else acc + term
        cn = _silu(_layer_norm(acc + convb_ref[...], clng_ref[...], clnb_ref[...]))
        conv = _dot(cn.astype(BF16), wpw2_ref[...])
        mix = mix + _dot(conv.astype(BF16), w_ref[2 * GROUP_W:3 * GROUP_W, :])

        gl = _gelu_tanh(cur_ref[rows, 3 * GROUP_W:5 * GROUP_W])
        u = gl[:, 0:GROUP_W]
        vn = _layer_norm(gl[:, GROUP_W:2 * GROUP_W], slng_ref[...], slnb_ref[...])
        gated = []
        for j in range(TIME_CHUNK // SGU_CHUNK):
            sub = slice(j * SGU_CHUNK, (j + 1) * SGU_CHUNK)
            s = bsp_ref[...]
            for g in range(SGU_GROUPS):
                s = s + _dot(wsp_ref[g], (vn[sub] * sgu_masks[g]).astype(BF16))
            gated.append((u[sub] * s).astype(BF16))
        mix = mix + _dot(jnp.concatenate(gated, axis=0), w_ref[3 * GROUP_W:, :])

        xn = x_ref[rows, :] + gt_ref[...] * mix
        xo_ref[rows, :] = xn
        h = xn * lax.rsqrt(jnp.mean(xn * xn, axis=-1, keepdims=True) + EPS) * gain + sh_ref[...]
        hi = h.astype(BF16)
        h_ref[rows, :] = hi
        lo = (h - hi.astype(F32)).astype(BF16)
        d = hi.shape[1]
        logits = _dot(hi, wr_ref[0:d, :]) + _dot(hi, wr_ref[d:2 * d, :]) + _dot(lo, wr_ref[0:d, :])
        lt = jnp.transpose(logits)[0:N_EXPERTS, :]
        e = jnp.exp(lt - jnp.max(lt, axis=0, keepdims=True))
        aff_ref[:, rows] = e / jnp.sum(e, axis=0, keepdims=True)


def _mixout(x, attn, rest, mod_row, mix_w, w_bf, gt, g2, sc, sh, wr_cat):
    B, T, D = x.shape
    RW = rest.shape[-1]
    step = min(MIX_ROWS, T)
    n_step = T // step
    per_step = step // HALO
    const2 = lambda b, i: (0, 0)
    row = lambda b, i: (mod_row(b), 0, 0)
    tok = lambda b, i: (b, i, 0)
    vec = pl.BlockSpec((1, GROUP_W), const2)
    mat = pl.BlockSpec((GROUP_W, GROUP_W), const2)
    return pl.pallas_call(
        functools.partial(_mixout_kernel, T=T),
        grid=(B, n_step),
        in_specs=[
            pl.BlockSpec((None, step, RW), tok),
            pl.BlockSpec((None, HALO, RW), lambda b, i: (b, jnp.maximum(i * per_step - 1, 0), 0)),
            pl.BlockSpec((None, HALO, RW), lambda b, i: (b, jnp.minimum((i + 1) * per_step, n_step * per_step - 1), 0)),
            mat, vec, vec,
            pl.BlockSpec((CONV_K, GROUP_W), const2), vec, vec, vec, mat,
            vec, vec,
            pl.BlockSpec((SGU_GROUPS, SGU_CHUNK, SGU_CHUNK), lambda b, i: (0, 0, 0)),
            pl.BlockSpec((SGU_CHUNK, GROUP_W), const2),
            pl.BlockSpec((None, step, D), tok),
            pl.BlockSpec((None, step, GROUP_W), tok),
            pl.BlockSpec((D, D), const2),
            pl.BlockSpec((None, 1, D), row),
            pl.BlockSpec((1, D), const2),
            pl.BlockSpec((None, 1, D), row),
            pl.BlockSpec((None, 1, D), row),
            pl.BlockSpec((2 * D, LANES), const2),
        ],
        out_specs=[
            pl.BlockSpec((None, step, D), tok),
            pl.BlockSpec((None, step, D), tok),
            pl.BlockSpec((None, N_EXPERTS, step), lambda b, i: (b, 0, i)),
        ],
        out_shape=[
            jax.ShapeDtypeStruct((B, T, D), F32),
            jax.ShapeDtypeStruct((B, T, D), BF16),
            jax.ShapeDtypeStruct((B, N_EXPERTS, T), F32),
        ],
        scratch_shapes=[pltpu.VMEM((SUBLANES, step + 2 * HALO, GROUP_W), F32)] * 2,
        compiler_params=_params(("arbitrary", "arbitrary")),
        name="mixers_outproj",
    )(rest, rest, rest, *mix_w, x, attn, w_bf, gt, g2, sc, sh, wr_cat)


def _lane_prefix(x, tri):
    outs = []
    off = jnp.zeros((x.shape[0], 1), F32)
    for j in range(x.shape[1] // LANES):
        xb = x[:, j * LANES:(j + 1) * LANES]
        inc = _dot(xb.astype(BF16), tri)
        outs.append(inc - xb + off)
        off = off + inc[:, LANES - 1:LANES]
    return jnp.concatenate(outs, axis=1)


def _route_kernel(a_ref, tri_ref, pos_ref, gate_ref, ptok_ref, *, cap, n_expert):
    a = a_ref[...]
    R, T = a.shape

    def enough(c):
        return jnp.sum(jnp.where(a >= c, 1.0, 0.0), axis=-1, keepdims=True) >= cap

    def bit_search(i, lo):
        cand = lo | lax.shift_left(jnp.int32(1), 30 - i)
        return jnp.where(enough(pltpu.bitcast(cand, F32)), cand, lo)

    lo_bits = lax.fori_loop(0, 31, bit_search, jnp.zeros((R, 1), jnp.int32))

    def refine(i, c):
        lo, hi = c
        mid = lo + 0.5 * (hi - lo)
        ok = enough(mid)
        return jnp.where(ok, mid, lo), jnp.where(ok, hi, mid)

    thr, _ = lax.fori_loop(0, SUB_ULP_STEPS, refine,
                           (pltpu.bitcast(lo_bits, F32), pltpu.bitcast(lo_bits + 1, F32)))
    gt = a > thr
    eq = a == thr
    need = cap - jnp.sum(jnp.where(gt, 1.0, 0.0), axis=-1, keepdims=True)
    tri = tri_ref[...]
    tie_rank = _lane_prefix(jnp.where(eq, 1.0, 0.0), tri)
    sel = jnp.logical_or(gt, jnp.logical_and(eq, tie_rank < need))
    pos = jnp.where(sel, _lane_prefix(jnp.where(sel, 1.0, 0.0), tri), -1.0)
    pos_ref[...] = pos
    gate_ref[...] = jnp.where(sel, a, 0.0)
    fill = jnp.full((LANES - n_expert, T), -1.0, F32)
    for b in range(R // n_expert):
        padded = jnp.concatenate([pos[b * n_expert:(b + 1) * n_expert], fill], axis=0)
        ptok_ref[b] = jnp.transpose(padded)


def _route(aff_t, tri, cap):
    B, E, T = aff_t.shape
    blk = pl.BlockSpec((B * E, T), lambda i: (0, 0))
    pos, gate, ptok = pl.pallas_call(
        functools.partial(_route_kernel, cap=cap, n_expert=E),
        grid=(1,),
        in_specs=[blk, pl.BlockSpec((LANES, LANES), lambda i: (0, 0))],
        out_specs=[blk, blk, pl.BlockSpec((B, T, LANES), lambda i: (0, 0, 0))],
        out_shape=[
            jax.ShapeDtypeStruct((B * E, T), F32),
            jax.ShapeDtypeStruct((B * E, T), F32),
            jax.ShapeDtypeStruct((B, T, LANES), F32),
        ],
        compiler_params=_params(("arbitrary",)),
        name="route",
    )(aff_t.reshape(B * E, T), tri)
    return pos.reshape(B, E, T), gate.reshape(B, E, T), ptok


def _gather_kernel(pos_ref, gate_ref, h_ref, xs_ref, gs_ref):
    E, cap, _ = xs_ref.shape
    T = pos_ref.shape[1]
    slot = lax.broadcasted_iota(jnp.int32, (cap, T), 0).astype(F32)
    for e in range(E):
        hit = slot == pos_ref[e:e + 1, :]
        onehot = jnp.where(hit, 1.0, 0.0).astype(BF16)
        xs_ref[e] = _dot(onehot, h_ref[...]).astype(BF16)
        gs_ref[e] = jnp.sum(jnp.where(hit, gate_ref[e:e + 1, :], 0.0), axis=-1, keepdims=True)


def _gather(pos_t, gate_t, h):
    B, E, T = pos_t.shape
    D = h.shape[-1]
    cap = EC_CAPACITY * T // E
    et = pl.BlockSpec((None, E, T), lambda b: (b, 0, 0))
    return pl.pallas_call(
        _gather_kernel,
        grid=(B,),
        in_specs=[et, et, pl.BlockSpec((None, T, D), lambda b: (b, 0, 0))],
        out_specs=[
            pl.BlockSpec((E, cap, D), lambda b: (0, b, 0)),
            pl.BlockSpec((E, cap, 1), lambda b: (0, b, 0)),
        ],
        out_shape=[
            jax.ShapeDtypeStruct((E, B * cap, D), BF16),
            jax.ShapeDtypeStruct((E, B * cap, 1), F32),
        ],
        compiler_params=_params(("arbitrary",)),
        name="gather",
    )(pos_t, gate_t, h)


def _moe_kernel(*refs, n_grp, tm):
    xs = refs[0:2 * n_grp:2]
    gs = refs[1:2 * n_grp:2]
    wg_ref, wu_ref, wd_ref = refs[2 * n_grp:2 * n_grp + 3]
    ys = refs[2 * n_grp + 3:3 * n_grp + 3]
    accs = refs[3 * n_grp + 3:4 * n_grp + 3]
    wgb, wub, wdb = refs[4 * n_grp + 3:]
    f = pl.program_id(1)

    @pl.when(jnp.logical_and(pl.program_id(0) == 0, f == 0))
    def _():
        for acc in accs:
            acc[...] = jnp.zeros_like(acc)

    wgb[...] = wg_ref[...].astype(BF16)
    wub[...] = wu_ref[...].astype(BF16)
    wdb[...] = wd_ref[...].astype(BF16)
    carry_on = f > 0

    for x_ref, g_ref, y_ref, acc in zip(xs, gs, ys, accs):
        rows = x_ref.shape[0]
        t = min(tm, rows)
        for r in range(rows // t):
            sl = slice(r * t, (r + 1) * t)
            xt = x_ref[sl, :]
            a = _dot(xt, wgb[...])
            u = _dot(xt, wub[...])
            total = jnp.where(carry_on, acc[sl, :], 0.0) + _dot((_silu(a) * u).astype(BF16), wdb[...])
            acc[sl, :] = total
            y_ref[sl, :] = (total * g_ref[sl, :]).astype(BF16)


def _moe(groups, layer, w_gate, w_up, w_down):
    _, E, D, F = w_gate.shape
    tf = FFN_COLS
    in_specs, args, out_specs, out_shape, scratch = [], [], [], [], []
    for xs, gs in groups:
        R = xs.shape[1]
        in_specs += [pl.BlockSpec((None, R, D), lambda e, f: (e, 0, 0)), pl.BlockSpec((None, R, 1), lambda e, f: (e, 0, 0))]
        args += [xs, gs]
        out_specs.append(pl.BlockSpec((None, R, D), lambda e, f: (e, 0, 0)))
        out_shape.append(jax.ShapeDtypeStruct((E, R, D), BF16))
        scratch.append(pltpu.VMEM((R, D), F32))
    in_specs += [
        pl.BlockSpec((None, None, D, tf), lambda e, f: (layer, e, 0, f)),
        pl.BlockSpec((None, None, D, tf), lambda e, f: (layer, e, 0, f)),
        pl.BlockSpec((None, None, tf, D), lambda e, f: (layer, e, f, 0)),
    ]
    args += [w_gate, w_up, w_down]
    scratch += [pltpu.VMEM((D, tf), BF16), pltpu.VMEM((D, tf), BF16), pltpu.VMEM((tf, D), BF16)]
    return pl.pallas_call(
        functools.partial(_moe_kernel, n_grp=len(groups), tm=FFN_ROWS),
        grid=(E, F // tf),
        in_specs=in_specs,
        out_specs=out_specs,
        out_shape=out_shape,
        scratch_shapes=scratch,
        compiler_params=_params(("arbitrary", "arbitrary")),
        name="expert_ffn",
    )(*args)


def _combine_kernel(x_ref, gt_ref, ptok_ref, ys_ref, o_ref, *, tm):
    T, D = x_ref.shape
    E, cap, _ = ys_ref.shape
    t = min(tm, T)
    lane = lax.broadcasted_iota(jnp.int32, (t, cap), 1).astype(F32)

    def tile(r, carry):
        sl = pl.ds(pl.multiple_of(r * t, t), t)
        pt = ptok_ref[sl, :]
        acc = jnp.zeros((t, D), F32)
        for e in range(E):
            onehot = jnp.where(pt[:, e:e + 1] == lane, 1.0, 0.0).astype(BF16)
            acc = acc + _dot(onehot, ys_ref[e])
        o_ref[sl, :] = x_ref[sl, :] + gt_ref[...] * acc
        return carry

    lax.fori_loop(0, T // t, tile, 0)


def _combine(x, gt, mod_row, ptok, ys):
    B, T, D = x.shape
    E = ys.shape[0]
    cap = ys.shape[1] // B
    ys4 = ys.reshape(E, B, cap, D)
    return pl.pallas_call(
        functools.partial(_combine_kernel, tm=COMBINE_ROWS),
        grid=(B,),
        in_specs=[
            pl.BlockSpec((None, T, D), lambda b: (b, 0, 0)),
            pl.BlockSpec((None, 1, D), lambda b: (mod_row(b), 0, 0)),
            pl.BlockSpec((None, T, LANES), lambda b: (b, 0, 0)),
            pl.BlockSpec((E, None, cap, D), lambda b: (0, b, 0, 0)),
        ],
        out_specs=pl.BlockSpec((None, T, D), lambda b: (b, 0, 0)),
        out_shape=jax.ShapeDtypeStruct((B, T, D), F32),
        compiler_params=_params(("arbitrary",)),
        name="combine",
    )(x, gt, ptok, ys4)


def _rope_tables(T):
    rows = (jnp.arange(T) // GRID_W).astype(F32)
    cols = (jnp.arange(T) % GRID_W).astype(F32)
    n_freq = ATTN_D // 4
    inv = ROPE_THETA ** (-jnp.arange(n_freq, dtype=F32) / n_freq)
    ang_r, ang_c = rows[:, None] * inv, cols[:, None] * inv
    cos = jnp.concatenate([jnp.cos(ang_r)] * 2 + [jnp.cos(ang_c)] * 2, axis=1)
    sin = jnp.concatenate([-jnp.sin(ang_r), jnp.sin(ang_r), -jnp.sin(ang_c), jnp.sin(ang_c)], axis=1)
    reps = GROUP_W // ATTN_D
    return jnp.tile(cos, (1, reps)).astype(F32), jnp.tile(sin, (1, reps)).astype(F32)


def _segment_ones(width):
    i = jnp.arange(GROUP_W) // width
    return (i[:, None] == i[None, :]).astype(BF16)


def _block_diag(w):
    G, n, _ = w.shape
    eye = jnp.eye(G, dtype=w.dtype)
    return (eye[:, None, :, None] * w[:, :, None, :]).reshape(G * n, G * n)


def kernel(x, c, ctx, c_ctx, w_mod, b_mod, g_norm1, g_norm2, w_in, w_out, g_q, g_k, lam_q1, lam_k1, lam_q2, lam_k2,
           g_attn_out, w_pool, b_pool, pool_scale, conv_w, conv_b, conv_ln_g, conv_ln_b, w_pw2, sgu_ln_g, sgu_ln_b,
           w_spatial, b_spatial, w_router, w_gate, w_up, w_down):
    B, T, D = x.shape
    C = ctx.shape[1]
    L = w_mod.shape[0]
    assert B < MOD_ROWS and D == D_MODEL and T % TIME_CHUNK == 0 and C % TIME_CHUNK == 0

    cond = jnp.zeros((MOD_ROWS, D), F32).at[:B].set(c).at[B].set(c_ctx)
    mods = _modulation(cond, w_mod, b_mod).reshape(L, MOD_ROWS, N_MOD, 1, D)
    lat_row = lambda b: b
    ctx_row = lambda b: B

    cos_x, sin_x = _rope_tables(T)
    cos_c, sin_c = jnp.ones((C, GROUP_W), F32), jnp.zeros((C, GROUP_W), F32)
    seg32 = _segment_ones(ATTN_D)
    tri = (jnp.arange(LANES)[:, None] <= jnp.arange(LANES)[None, :]).astype(BF16)
    tile_g = lambda g: jnp.tile(g, GROUP_W // g.shape[0]).reshape(1, GROUP_W)
    vec = lambda v: v.reshape(1, -1)

    for l in range(L):
        last = l == L - 1
        lam_init = 0.8 - 0.6 * math.exp(-0.3 * l)
        sh1, sc1, gt1, sh2, sc2, gt2 = (mods[l, :, i] for i in range(N_MOD))
        w_in_bf = w_in[l].astype(BF16)
        w_out_bf = w_out[l].astype(BF16)
        wr = jnp.zeros((D, LANES), F32).at[:, :N_EXPERTS].set(w_router[l])
        wr_hi = wr.astype(BF16)
        wr_cat = jnp.concatenate([wr_hi, (wr - wr_hi.astype(F32)).astype(BF16)], axis=0)
        lams = [vec(p[l]) for p in (lam_q1, lam_k1, lam_q2, lam_k2)]
        gq_t, gk_t, go_t = tile_g(g_q[l]), tile_g(g_k[l]), tile_g(g_attn_out[l])
        mix_w = (_block_diag(w_pool[l]).astype(BF16), vec(b_pool[l]), vec(pool_scale[l]), conv_w[l], vec(conv_b[l]),
                 vec(conv_ln_g[l]), vec(conv_ln_b[l]), w_pw2[l].astype(BF16), vec(sgu_ln_g[l]), vec(sgu_ln_b[l]),
                 w_spatial[l].astype(BF16), jnp.repeat(b_spatial[l].T, SGU_GW, axis=1))
        g1, g2 = vec(g_norm1[l]), vec(g_norm2[l])

        qx, kx, vx, rx = _inproj(x, lat_row, g1, sc1, sh1, w_in_bf, gq_t, gk_t, cos_x, sin_x, seg32)
        qc, kc, vc, rc = _inproj(ctx, ctx_row, g1, sc1, sh1, w_in_bf, gq_t, gk_t, cos_c, sin_c, seg32)

        attn_x = _attention(qx, [(kc, vc), (kx, vx)], lams, go_t, gq_t, gk_t, lam_init)
        x, h2x, aff_x = _mixout(x, attn_x, rx, lat_row, mix_w, w_out_bf, gt1, g2, sc2, sh2, wr_cat)
        pos_x, gate_x, ptok_x = _route(aff_x, tri, EC_CAPACITY * T // N_EXPERTS)
        groups = [_gather(pos_x, gate_x, h2x)]

        if not last:
            attn_c = _attention(qc, [(kc, vc)], lams, go_t, gq_t, gk_t, lam_init)
            ctx, h2c, aff_c = _mixout(ctx, attn_c, rc, ctx_row, mix_w, w_out_bf, gt1, g2, sc2, sh2, wr_cat)
            pos_c, gate_c, ptok_c = _route(aff_c, tri, EC_CAPACITY * C // N_EXPERTS)
            groups.append(_gather(pos_c, gate_c, h2c))

        ys = _moe(groups, l, w_gate, w_up, w_down)
        x = _combine(x, gt2, lat_row, ptok_x, ys[0])
        if not last:
            ctx = _combine(ctx, gt2, ctx_row, ptok_c, ys[1])
    return x
```

```python
import functools
import math

import jax
import jax.numpy as jnp
from jax import lax
from jax.experimental import pallas as pl
from jax.experimental.pallas import tpu as pltpu

F32 = jnp.float32
BF16 = jnp.bfloat16

D_MODEL = 1024
GRID_W = 64
GROUP_W = 256
ATTN_HEADS = 4
ATTN_D = 32
ATTN_VD = 64
ROPE_THETA = 10000.0
POOL_WINDOWS = (2, 4, 8, 16)
CONV_K = 31
SGU_CHUNK = 128
SGU_GROUPS = 4
SGU_GW = 64
N_EXPERTS = 16
EC_CAPACITY = 2
N_MOD = 6
IN_W = 2048
EPS = 1e-6
LOG2E = 1.4426950408889634
MAX_UNSTABILISED_SCORE = 40.0
MAX_UNSTABILISED_LAM = 2.0 ** 30
SCORE_BOUND_SLACK = 1.02
SUB_ULP_STEPS = 12

MOD_ROWS = 16
LANES = 128
SUBLANES = 8
HALO = 16
TIME_CHUNK = 256
VMEM_LIMIT = 56 * 1024 * 1024

MOD_COLS = 1024
INPROJ_ROWS = 1024
INPROJ_SUB = 512
ATTN_Q_ROWS = 512
MIX_ROWS = 4 * TIME_CHUNK
FFN_COLS = 512
FFN_ROWS = 512
COMBINE_ROWS = 512


def _params(sem):
    return pltpu.CompilerParams(dimension_semantics=sem, vmem_limit_bytes=VMEM_LIMIT)


def _dot(a, b):
    return jnp.dot(a, b, preferred_element_type=F32)


def _split_dot(a, w):
    hi = a.astype(BF16)
    lo = (a - hi.astype(F32)).astype(BF16)
    return _dot(hi, w) + _dot(lo, w)


def _silu(x):
    return x * jax.nn.sigmoid(x)


def _layer_norm(x, g, b):
    mu = jnp.mean(x, axis=-1, keepdims=True)
    xc = x - mu
    var = jnp.mean(xc * xc, axis=-1, keepdims=True)
    return xc * lax.rsqrt(var + EPS) * g + b


def _mod_kernel(c_ref, w_ref, b_ref, o_ref):
    s = _silu(c_ref[...])
    o_ref[...] = _dot(s.astype(BF16), w_ref[...].astype(BF16)) + b_ref[...]


def _modulation(cond, w_mod, b_mod):
    L, D, N = w_mod.shape
    tn = MOD_COLS
    return pl.pallas_call(
        _mod_kernel,
        grid=(L, N // tn),
        in_specs=[
            pl.BlockSpec((MOD_ROWS, D), lambda l, j: (0, 0)),
            pl.BlockSpec((None, D, tn), lambda l, j: (l, 0, j)),
            pl.BlockSpec((None, 1, tn), lambda l, j: (l, 0, j)),
        ],
        out_specs=pl.BlockSpec((None, MOD_ROWS, tn), lambda l, j: (l, 0, j)),
        out_shape=jax.ShapeDtypeStruct((L, MOD_ROWS, N), F32),
        compiler_params=_params(("arbitrary", "arbitrary")),
        name="modulation",
    )(cond, w_mod, b_mod.reshape(L, 1, N))


def _inproj_kernel(x_ref, g_ref, sc_ref, sh_ref, w_ref, gq_ref, gk_ref, cos_ref, sin_ref, seg_ref,
                   q_ref, k_ref, v_ref, r_ref, *, sub):
    gain = g_ref[...] * (1.0 + sc_ref[...])
    seg = seg_ref[...]
    n_freq = ATTN_D // 4
    first = (lax.broadcasted_iota(jnp.int32, (sub, GROUP_W), 1) % (2 * n_freq)) < n_freq

    for r in range(x_ref.shape[0] // sub):
        rows = slice(r * sub, (r + 1) * sub)
        x = x_ref[rows, :]
        h = x * lax.rsqrt(jnp.mean(x * x, axis=-1, keepdims=True) + EPS) * gain + sh_ref[...]
        p = _dot(h.astype(BF16), w_ref[...])
        cos = cos_ref[rows, :]
        sin = sin_ref[rows, :]

        def prep(a, g):
            ss = _split_dot(a * a, seg)
            n = a * lax.rsqrt(ss * (1.0 / ATTN_D) + EPS) * g
            partner = jnp.where(first, pltpu.roll(n, GROUP_W - n_freq, 1), pltpu.roll(n, n_freq, 1))
            return n * cos + partner * sin

        q_ref[rows, :] = (prep(p[:, 0:GROUP_W], gq_ref[...]) * (ATTN_D ** -0.5 * LOG2E)).astype(BF16)
        k_ref[:, rows] = jnp.transpose(prep(p[:, GROUP_W:2 * GROUP_W], gk_ref[...])).astype(BF16)
        v_ref[rows, :] = p[:, 2 * GROUP_W:3 * GROUP_W].astype(BF16)
        r_ref[rows, :] = p[:, 3 * GROUP_W:]


def _inproj(x, mod_row, g1, sc, sh, w_bf, gq_t, gk_t, cos_t, sin_t, seg32):
    B, T, D = x.shape
    tm = min(INPROJ_ROWS, T)
    rest_w = IN_W - 3 * GROUP_W
    row = lambda b, i: (mod_row(b), 0, 0)
    const2 = lambda b, i: (0, 0)
    tok = lambda b, i: (b, i, 0)
    return pl.pallas_call(
        functools.partial(_inproj_kernel, sub=min(INPROJ_SUB, tm)),
        grid=(B, T // tm),
        in_specs=[
            pl.BlockSpec((None, tm, D), tok),
            pl.BlockSpec((1, D), const2),
            pl.BlockSpec((None, 1, D), row),
            pl.BlockSpec((None, 1, D), row),
            pl.BlockSpec((D, IN_W), const2),
            pl.BlockSpec((1, GROUP_W), const2),
            pl.BlockSpec((1, GROUP_W), const2),
            pl.BlockSpec((tm, GROUP_W), lambda b, i: (i, 0)),
            pl.BlockSpec((tm, GROUP_W), lambda b, i: (i, 0)),
            pl.BlockSpec((GROUP_W, GROUP_W), const2),
        ],
        out_specs=[
            pl.BlockSpec((None, tm, GROUP_W), tok),
            pl.BlockSpec((None, GROUP_W, tm), lambda b, i: (b, 0, i)),
            pl.BlockSpec((None, tm, GROUP_W), tok),
            pl.BlockSpec((None, tm, rest_w), tok),
        ],
        out_shape=[
            jax.ShapeDtypeStruct((B, T, GROUP_W), BF16),
            jax.ShapeDtypeStruct((B, GROUP_W, T), BF16),
            jax.ShapeDtypeStruct((B, T, GROUP_W), BF16),
            jax.ShapeDtypeStruct((B, T, rest_w), F32),
        ],
        compiler_params=_params(("arbitrary", "arbitrary")),
        name="inproj",
    )(x, g1, sc, sh, w_bf, gq_t, gk_t, cos_t, sin_t, seg32)


def _attn_kernel(*refs, n_seg, lam_init):
    q_ref = refs[0]
    kv_refs = refs[1:1 + 2 * n_seg]
    lq1_ref, lk1_ref, lq2_ref, lk2_ref, go_ref, gq_ref, gk_ref, o_ref = refs[1 + 2 * n_seg:]
    lam = (jnp.exp(jnp.sum(lq1_ref[...] * lk1_ref[...], axis=-1, keepdims=True))
           - jnp.exp(jnp.sum(lq2_ref[...] * lk2_ref[...], axis=-1, keepdims=True)) + lam_init)
    q = q_ref[...]

    def scores(off):
        qs = q[:, off:off + ATTN_D]
        return [_dot(qs, kv_refs[2 * i][off:off + ATTN_D, :]) for i in range(n_seg)]

    def run(stabilise):
        def softmax_parts(s):
            if stabilise:
                mx = functools.reduce(jnp.maximum, [jnp.max(si, axis=-1, keepdims=True) for si in s])
                s = [si - mx for si in s]
            p = [jnp.exp2(si) for si in s]
            den = functools.reduce(jnp.add, [jnp.sum(pi, axis=-1, keepdims=True) for pi in p])
            return [pi.astype(BF16) for pi in p], den

        heads = []
        nxt = (scores(0), scores(ATTN_D))
        for h in range(ATTN_HEADS):
            cur = nxt
            if h + 1 < ATTN_HEADS:
                nxt = (scores((h + 1) * ATTN_VD), scores((h + 1) * ATTN_VD + ATTN_D))
            p1, l1 = softmax_parts(cur[0])
            p2, l2 = softmax_parts(cur[1])
            c = (lam * l1 / l2).astype(BF16)
            o = None
            for i in range(n_seg):
                oi = _dot(p1[i] - c * p2[i], kv_refs[2 * i + 1][:, h * ATTN_VD:(h + 1) * ATTN_VD])
                o = oi if o is None else o + oi
            o = o * (1.0 / l1)
            ms = jnp.mean(o * o, axis=-1, keepdims=True)
            heads.append(o * lax.rsqrt(ms + EPS))
        o_ref[...] = (jnp.concatenate(heads, axis=1) * go_ref[...] * (1.0 - lam_init)).astype(BF16)

    bound = (jnp.max(jnp.abs(gq_ref[...])) * jnp.max(jnp.abs(gk_ref[...]))
             * (ATTN_D * ATTN_D ** -0.5 * LOG2E * SCORE_BOUND_SLACK))
    small = jnp.logical_and(bound < MAX_UNSTABILISED_SCORE, jnp.max(jnp.abs(lam)) < MAX_UNSTABILISED_LAM)

    @pl.when(small)
    def _():
        run(False)

    @pl.when(jnp.logical_not(small))
    def _():
        run(True)


def _attention(q, kvs, lams, go_t, gq_t, gk_t, lam_init):
    B, T, _ = q.shape
    tq = min(ATTN_Q_ROWS, T)
    tok = lambda b, i: (b, i, 0)
    const2 = lambda b, i: (0, 0)
    in_specs = [pl.BlockSpec((None, tq, GROUP_W), tok)]
    args = [q]
    for kt, v in kvs:
        n = v.shape[1]
        in_specs += [pl.BlockSpec((None, GROUP_W, n), lambda b, i: (b, 0, 0)),
                     pl.BlockSpec((None, n, GROUP_W), lambda b, i: (b, 0, 0))]
        args += [kt, v]
    in_specs += [pl.BlockSpec((1, ATTN_D), const2)] * 4
    in_specs += [pl.BlockSpec((1, GROUP_W), const2)] * 3
    args += list(lams) + [go_t, gq_t, gk_t]
    return pl.pallas_call(
        functools.partial(_attn_kernel, n_seg=len(kvs), lam_init=lam_init),
        grid=(B, T // tq),
        in_specs=in_specs,
        out_specs=pl.BlockSpec((None, tq, GROUP_W), tok),
        out_shape=jax.ShapeDtypeStruct((B, T, GROUP_W), BF16),
        compiler_params=_params(("arbitrary", "arbitrary")),
        name="diff_attention",
    )(*args)


def _gelu_tanh(x):
    return 0.5 * x * (1.0 + jnp.tanh(0.7978845608028654 * (x + 0.044715 * (x * x * x))))


def _mixout_kernel(cur_ref, prev_ref, next_ref, wpool_ref, bpool_ref, pscale_ref, convw_ref, convb_ref, clng_ref,
                   clnb_ref, wpw2_ref, slng_ref, slnb_ref, wsp_ref, bsp_ref,
                   x_ref, a_ref, w_ref, gt_ref, g2_ref, sc_ref, sh_ref, wr_ref,
                   xo_ref, h_ref, aff_ref, zwin, ywin, *, T):
    i = pl.program_id(1)
    has_prev = jnp.where(i > 0, 1.0, 0.0)
    has_next = jnp.where(i < pl.num_programs(1) - 1, 1.0, 0.0)

    def glu(ref):
        return ref[:, GROUP_W:2 * GROUP_W] * jax.nn.sigmoid(ref[:, 2 * GROUP_W:3 * GROUP_W])

    step = cur_ref.shape[0]
    win_rows = step + 2 * HALO

    def fill(win, head, body, tail):
        win[0, 0:HALO, :] = head * has_prev
        win[0, HALO:HALO + step, :] = body
        win[0, HALO + step:, :] = tail * has_next
        whole = win[0]
        for r in range(1, SUBLANES):
            win[r] = pltpu.roll(whole, win_rows - r, 0)

    def shifted(win, row):
        base = row - row % SUBLANES
        return win[row % SUBLANES, base:base + TIME_CHUNK, :]

    fill(zwin, prev_ref[:, 0:GROUP_W], cur_ref[:, 0:GROUP_W], next_ref[:, 0:GROUP_W])
    fill(ywin, glu(prev_ref), glu(cur_ref), glu(next_ref))

    pool_group = lax.broadcasted_iota(jnp.int32, (TIME_CHUNK, GROUP_W), 1) // SGU_GW
    sgu_masks = [((lax.broadcasted_iota(jnp.int32, (1, GROUP_W), 1) // SGU_GW) == g).astype(F32)
                 for g in range(SGU_GROUPS)]
    gain = g2_ref[...] * (1.0 + sc_ref[...])

    for c in range(step // TIME_CHUNK):
        first = c * TIME_CHUNK
        rows = slice(first, first + TIME_CHUNK)

        def zs(off):
            return shifted(zwin, first + HALO + off)

        t = i * step + first + lax.broadcasted_iota(jnp.int32, (TIME_CHUNK, 1), 0)
        z0 = zs(0)
        sums = []
        acc = None
        for w in POOL_WINDOWS:
            for off in range(-(w // 2), w // 2):
                if acc is None or not (-(w // 4) <= off < w // 4):
                    acc = zs(off) if acc is None else acc + zs(off)
            sums.append(acc)
        mean = None
        for g, w in enumerate(POOL_WINDOWS):
            cnt = (jnp.minimum(t + w // 2, T) - jnp.maximum(t - w // 2, 0)).astype(F32)
            mg = sums[g] / cnt
            mean = mg if mean is None else jnp.where(pool_group == g, mg, mean)
        pool = (_dot((mean - z0).astype(BF16), wpool_ref[...]) + bpool_ref[...]) * pscale_ref[...]
        mix = _dot(a_ref[rows, :], w_ref[0:GROUP_W, :]) + _dot(pool.astype(BF16), w_ref[GROUP_W:2 * GROUP_W, :])

        acc = None
        for k in range(CONV_K):
            term = shifted(ywin, first + HALO + k - CONV_K // 2) * convw_ref[k:k + 1, :]
            acc = term if acc is None else acc + term
        cn = _silu(_layer_norm(acc + convb_ref[...], clng_ref[...], clnb_ref[...]))
        conv = _dot(cn.astype(BF16), wpw2_ref[...])
        mix = mix + _dot(conv.astype(BF16), w_ref[2 * GROUP_W:3 * GROUP_W, :])

        gl = _gelu_tanh(cur_ref[rows, 3 * GROUP_W:5 * GROUP_W])
        u = gl[:, 0:GROUP_W]
        vn = _layer_norm(gl[:, GROUP_W:2 * GROUP_W], slng_ref[...], slnb_ref[...])
        gated = []
        for j in range(TIME_CHUNK // SGU_CHUNK):
            sub = slice(j * SGU_CHUNK, (j + 1) * SGU_CHUNK)
            s = bsp_ref[...]
            for g in range(SGU_GROUPS):
                s = s + _dot(wsp_ref[g], (vn[sub] * sgu_masks[g]).astype(BF16))
            gated.append((u[sub] * s).astype(BF16))
        mix = mix + _dot(jnp.concatenate(gated, axis=0), w_ref[3 * GROUP_W:, :])

        xn = x_ref[rows, :] + gt_ref[...] * mix
        xo_ref[rows, :] = xn
        h = xn * lax.rsqrt(jnp.mean(xn * xn, axis=-1, keepdims=True) + EPS) * gain + sh_ref[...]
        hi = h.astype(BF16)
        h_ref[rows, :] = hi
        logits = _dot(hi, wr_ref[...])
        valid = lax.broadcasted_iota(jnp.int32, logits.shape, 1) < N_EXPERTS
        z = jnp.where(valid, logits, -jnp.inf)
        e = jnp.exp(z - jnp.max(z, axis=-1, keepdims=True))
        aff = e / jnp.sum(e, axis=-1, keepdims=True)
        aff_ref[:, rows] = jnp.transpose(aff)[0:N_EXPERTS, :]


def _mixout(x, attn, rest, mod_row, mix_w, w_bf, gt, g2, sc, sh, wr_bf):
    B, T, D = x.shape
    RW = rest.shape[-1]
    step = min(MIX_ROWS, T)
    n_step = T // step
    per_step = step // HALO
    const2 = lambda b, i: (0, 0)
    row = lambda b, i: (mod_row(b), 0, 0)
    tok = lambda b, i: (b, i, 0)
    vec = pl.BlockSpec((1, GROUP_W), const2)
    mat = pl.BlockSpec((GROUP_W, GROUP_W), const2)
    return pl.pallas_call(
        functools.partial(_mixout_kernel, T=T),
        grid=(B, n_step),
        in_specs=[
            pl.BlockSpec((None, step, RW), tok),
            pl.BlockSpec((None, HALO, RW), lambda b, i: (b, jnp.maximum(i * per_step - 1, 0), 0)),
            pl.BlockSpec((None, HALO, RW), lambda b, i: (b, jnp.minimum((i + 1) * per_step, n_step * per_step - 1), 0)),
            mat, vec, vec,
            pl.BlockSpec((CONV_K, GROUP_W), const2), vec, vec, vec, mat,
            vec, vec,
            pl.BlockSpec((SGU_GROUPS, SGU_CHUNK, SGU_CHUNK), lambda b, i: (0, 0, 0)),
            pl.BlockSpec((SGU_CHUNK, GROUP_W), const2),
            pl.BlockSpec((None, step, D), tok),
            pl.BlockSpec((None, step, GROUP_W), tok),
            pl.BlockSpec((D, D), const2),
            pl.BlockSpec((None, 1, D), row),
            pl.BlockSpec((1, D), const2),
            pl.BlockSpec((None, 1, D), row),
            pl.BlockSpec((None, 1, D), row),
            pl.BlockSpec((D, LANES), const2),
        ],
        out_specs=[
            pl.BlockSpec((None, step, D), tok),
            pl.BlockSpec((None, step, D), tok),
            pl.BlockSpec((None, N_EXPERTS, step), lambda b, i: (b, 0, i)),
        ],
        out_shape=[
            jax.ShapeDtypeStruct((B, T, D), F32),
            jax.ShapeDtypeStruct((B, T, D), BF16),
            jax.ShapeDtypeStruct((B, N_EXPERTS, T), F32),
        ],
        scratch_shapes=[pltpu.VMEM((SUBLANES, step + 2 * HALO, GROUP_W), F32)] * 2,
        compiler_params=_params(("arbitrary", "arbitrary")),
        name="mixers_outproj",
    )(rest, rest, rest, *mix_w, x, attn, w_bf, gt, g2, sc, sh, wr_bf)


def _lane_prefix(x, tri):
    outs = []
    off = jnp.zeros((x.shape[0], 1), F32)
    for j in range(x.shape[1] // LANES):
        xb = x[:, j * LANES:(j + 1) * LANES]
        inc = _dot(xb.astype(BF16), tri)
        outs.append(inc - xb + off)
        off = off + inc[:, LANES - 1:LANES]
    return jnp.concatenate(outs, axis=1)


def _route_kernel(a_ref, tri_ref, pos_ref, gate_ref, ptok_ref, *, cap, n_expert):
    a = a_ref[...]
    R, T = a.shape

    def enough(c):
        return jnp.sum(jnp.where(a >= c, 1.0, 0.0), axis=-1, keepdims=True) >= cap

    def bit_search(i, lo):
        cand = lo | lax.shift_left(jnp.int32(1), 30 - i)
        return jnp.where(enough(pltpu.bitcast(cand, F32)), cand, lo)

    lo_bits = lax.fori_loop(0, 31, bit_search, jnp.zeros((R, 1), jnp.int32))

    def refine(i, c):
        lo, hi = c
        mid = lo + 0.5 * (hi - lo)
        ok = enough(mid)
        return jnp.where(ok, mid, lo), jnp.where(ok, hi, mid)

    thr, _ = lax.fori_loop(0, SUB_ULP_STEPS, refine,
                           (pltpu.bitcast(lo_bits, F32), pltpu.bitcast(lo_bits + 1, F32)))
    gt = a > thr
    eq = a == thr
    need = cap - jnp.sum(jnp.where(gt, 1.0, 0.0), axis=-1, keepdims=True)
    tri = tri_ref[...]
    tie_rank = _lane_prefix(jnp.where(eq, 1.0, 0.0), tri)
    sel = jnp.logical_or(gt, jnp.logical_and(eq, tie_rank < need))
    pos = jnp.where(sel, _lane_prefix(jnp.where(sel, 1.0, 0.0), tri), -1.0)
    pos_ref[...] = pos
    gate_ref[...] = jnp.where(sel, a, 0.0)
    fill = jnp.full((LANES - n_expert, T), -1.0, F32)
    for b in range(R // n_expert):
        padded = jnp.concatenate([pos[b * n_expert:(b + 1) * n_expert], fill], axis=0)
        ptok_ref[b] = jnp.transpose(padded)


def _route(aff_t, tri, cap):
    B, E, T = aff_t.shape
    blk = pl.BlockSpec((B * E, T), lambda i: (0, 0))
    pos, gate, ptok = pl.pallas_call(
        functools.partial(_route_kernel, cap=cap, n_expert=E),
        grid=(1,),
        in_specs=[blk, pl.BlockSpec((LANES, LANES), lambda i: (0, 0))],
        out_specs=[blk, blk, pl.BlockSpec((B, T, LANES), lambda i: (0, 0, 0))],
        out_shape=[
            jax.ShapeDtypeStruct((B * E, T), F32),
            jax.ShapeDtypeStruct((B * E, T), F32),
            jax.ShapeDtypeStruct((B, T, LANES), F32),
        ],
        compiler_params=_params(("arbitrary",)),
        name="route",
    )(aff_t.reshape(B * E, T), tri)
    return pos.reshape(B, E, T), gate.reshape(B, E, T), ptok


def _gather_kernel(pos_ref, gate_ref, h_ref, xs_ref, gs_ref):
    E, cap, _ = xs_ref.shape
    T = pos_ref.shape[1]
    slot = lax.broadcasted_iota(jnp.int32, (cap, T), 0).astype(F32)
    for e in range(E):
        hit = slot == pos_ref[e:e + 1, :]
        onehot = jnp.where(hit, 1.0, 0.0).astype(BF16)
        xs_ref[e] = _dot(onehot, h_ref[...]).astype(BF16)
        gs_ref[e] = jnp.sum(jnp.where(hit, gate_ref[e:e + 1, :], 0.0), axis=-1, keepdims=True)


def _gather(pos_t, gate_t, h):
    B, E, T = pos_t.shape
    D = h.shape[-1]
    cap = EC_CAPACITY * T // E
    et = pl.BlockSpec((None, E, T), lambda b: (b, 0, 0))
    return pl.pallas_call(
        _gather_kernel,
        grid=(B,),
        in_specs=[et, et, pl.BlockSpec((None, T, D), lambda b: (b, 0, 0))],
        out_specs=[
            pl.BlockSpec((E, cap, D), lambda b: (0, b, 0)),
            pl.BlockSpec((E, cap, 1), lambda b: (0, b, 0)),
        ],
        out_shape=[
            jax.ShapeDtypeStruct((E, B * cap, D), BF16),
            jax.ShapeDtypeStruct((E, B * cap, 1), F32),
        ],
        compiler_params=_params(("arbitrary",)),
        name="gather",
    )(pos_t, gate_t, h)


def _moe_kernel(*refs, n_grp, tm):
    xs = refs[0:2 * n_grp:2]
    gs = refs[1:2 * n_grp:2]
    wg_ref, wu_ref, wd_ref = refs[2 * n_grp:2 * n_grp + 3]
    ys = refs[2 * n_grp + 3:3 * n_grp + 3]
    accs = refs[3 * n_grp + 3:4 * n_grp + 3]
    wgb, wub, wdb = refs[4 * n_grp + 3:]
    f = pl.program_id(1)

    @pl.when(jnp.logical_and(pl.program_id(0) == 0, f == 0))
    def _():
        for acc in accs:
            acc[...] = jnp.zeros_like(acc)

    wgb[...] = wg_ref[...].astype(BF16)
    wub[...] = wu_ref[...].astype(BF16)
    wdb[...] = wd_ref[...].astype(BF16)
    carry_on = f > 0

    for x_ref, g_ref, y_ref, acc in zip(xs, gs, ys, accs):
        rows = x_ref.shape[0]
        t = min(tm, rows)
        for r in range(rows // t):
            sl = slice(r * t, (r + 1) * t)
            xt = x_ref[sl, :]
            a = _dot(xt, wgb[...])
            u = _dot(xt, wub[...])
            total = jnp.where(carry_on, acc[sl, :], 0.0) + _dot((_silu(a) * u).astype(BF16), wdb[...])
            acc[sl, :] = total
            y_ref[sl, :] = (total * g_ref[sl, :]).astype(BF16)


def _moe(groups, layer, w_gate, w_up, w_down):
    _, E, D, F = w_gate.shape
    tf = FFN_COLS
    in_specs, args, out_specs, out_shape, scratch = [], [], [], [], []
    for xs, gs in groups:
        R = xs.shape[1]
        in_specs += [pl.BlockSpec((None, R, D), lambda e, f: (e, 0, 0)), pl.BlockSpec((None, R, 1), lambda e, f: (e, 0, 0))]
        args += [xs, gs]
        out_specs.append(pl.BlockSpec((None, R, D), lambda e, f: (e, 0, 0)))
        out_shape.append(jax.ShapeDtypeStruct((E, R, D), BF16))
        scratch.append(pltpu.VMEM((R, D), F32))
    in_specs += [
        pl.BlockSpec((None, None, D, tf), lambda e, f: (layer, e, 0, f)),
        pl.BlockSpec((None, None, D, tf), lambda e, f: (layer, e, 0, f)),
        pl.BlockSpec((None, None, tf, D), lambda e, f: (layer, e, f, 0)),
    ]
    args += [w_gate, w_up, w_down]
    scratch += [pltpu.VMEM((D, tf), BF16), pltpu.VMEM((D, tf), BF16), pltpu.VMEM((tf, D), BF16)]
    return pl.pallas_call(
        functools.partial(_moe_kernel, n_grp=len(groups), tm=FFN_ROWS),
        grid=(E, F // tf),
        in_specs=in_specs,
        out_specs=out_specs,
        out_shape=out_shape,
        scratch_shapes=scratch,
        compiler_params=_params(("arbitrary", "arbitrary")),
        name="expert_ffn",
    )(*args)


def _combine_kernel(x_ref, gt_ref, ptok_ref, ys_ref, o_ref, *, tm):
    T, D = x_ref.shape
    E, cap, _ = ys_ref.shape
    t = min(tm, T)
    lane = lax.broadcasted_iota(jnp.int32, (t, cap), 1).astype(F32)

    def tile(r, carry):
        sl = pl.ds(pl.multiple_of(r * t, t), t)
        pt = ptok_ref[sl, :]
        acc = jnp.zeros((t, D), F32)
        for e in range(E):
            onehot = jnp.where(pt[:, e:e + 1] == lane, 1.0, 0.0).astype(BF16)
            acc = acc + _dot(onehot, ys_ref[e])
        o_ref[sl, :] = x_ref[sl, :] + gt_ref[...] * acc
        return carry

    lax.fori_loop(0, T // t, tile, 0)


def _combine(x, gt, mod_row, ptok, ys):
    B, T, D = x.shape
    E = ys.shape[0]
    cap = ys.shape[1] // B
    ys4 = ys.reshape(E, B, cap, D)
    return pl.pallas_call(
        functools.partial(_combine_kernel, tm=COMBINE_ROWS),
        grid=(B,),
        in_specs=[
            pl.BlockSpec((None, T, D), lambda b: (b, 0, 0)),
            pl.BlockSpec((None, 1, D), lambda b: (mod_row(b), 0, 0)),
            pl.BlockSpec((None, T, LANES), lambda b: (b, 0, 0)),
            pl.BlockSpec((E, None, cap, D), lambda b: (0, b, 0, 0)),
        ],
        out_specs=pl.BlockSpec((None, T, D), lambda b: (b, 0, 0)),
        out_shape=jax.ShapeDtypeStruct((B, T, D), F32),
        compiler_params=_params(("arbitrary",)),
        name="combine",
    )(x, gt, ptok, ys4)


def _rope_tables(T):
    rows = (jnp.arange(T) // GRID_W).astype(F32)
    cols = (jnp.arange(T) % GRID_W).astype(F32)
    n_freq = ATTN_D // 4
    inv = ROPE_THETA ** (-jnp.arange(n_freq, dtype=F32) / n_freq)
    ang_r, ang_c = rows[:, None] * inv, cols[:, None] * inv
    cos = jnp.concatenate([jnp.cos(ang_r)] * 2 + [jnp.cos(ang_c)] * 2, axis=1)
    sin = jnp.concatenate([-jnp.sin(ang_r), jnp.sin(ang_r), -jnp.sin(ang_c), jnp.sin(ang_c)], axis=1)
    reps = GROUP_W // ATTN_D
    return jnp.tile(cos, (1, reps)).astype(F32), jnp.tile(sin, (1, reps)).astype(F32)


def _segment_ones(width):
    i = jnp.arange(GROUP_W) // width
    return (i[:, None] == i[None, :]).astype(BF16)


def _block_diag(w):
    G, n, _ = w.shape
    eye = jnp.eye(G, dtype=w.dtype)
    return (eye[:, None, :, None] * w[:, :, None, :]).reshape(G * n, G * n)


def kernel(x, c, ctx, c_ctx, w_mod, b_mod, g_norm1, g_norm2, w_in, w_out, g_q, g_k, lam_q1, lam_k1, lam_q2, lam_k2,
           g_attn_out, w_pool, b_pool, pool_scale, conv_w, conv_b, conv_ln_g, conv_ln_b, w_pw2, sgu_ln_g, sgu_ln_b,
           w_spatial, b_spatial, w_router, w_gate, w_up, w_down):
    B, T, D = x.shape
    C = ctx.shape[1]
    L = w_mod.shape[0]
    assert B < MOD_ROWS and D == D_MODEL and T % TIME_CHUNK == 0 and C % TIME_CHUNK == 0

    cond = jnp.zeros((MOD_ROWS, D), F32).at[:B].set(c).at[B].set(c_ctx)
    mods = _modulation(cond, w_mod, b_mod).reshape(L, MOD_ROWS, N_MOD, 1, D)
    lat_row = lambda b: b
    ctx_row = lambda b: B

    cos_x, sin_x = _rope_tables(T)
    cos_c, sin_c = jnp.ones((C, GROUP_W), F32), jnp.zeros((C, GROUP_W), F32)
    seg32 = _segment_ones(ATTN_D)
    tri = (jnp.arange(LANES)[:, None] <= jnp.arange(LANES)[None, :]).astype(BF16)
    tile_g = lambda g: jnp.tile(g, GROUP_W // g.shape[0]).reshape(1, GROUP_W)
    vec = lambda v: v.reshape(1, -1)

    for l in range(L):
        last = l == L - 1
        lam_init = 0.8 - 0.6 * math.exp(-0.3 * l)
        sh1, sc1, gt1, sh2, sc2, gt2 = (mods[l, :, i] for i in range(N_MOD))
        w_in_bf = w_in[l].astype(BF16)
        w_out_bf = w_out[l].astype(BF16)
        wr = jnp.zeros((D, LANES), F32).at[:, :N_EXPERTS].set(w_router[l])
        wr_bf = wr.astype(BF16)
        lams = [vec(p[l]) for p in (lam_q1, lam_k1, lam_q2, lam_k2)]
        gq_t, gk_t, go_t = tile_g(g_q[l]), tile_g(g_k[l]), tile_g(g_attn_out[l])
        mix_w = (_block_diag(w_pool[l]).astype(BF16), vec(b_pool[l]), vec(pool_scale[l]), conv_w[l], vec(conv_b[l]),
                 vec(conv_ln_g[l]), vec(conv_ln_b[l]), w_pw2[l].astype(BF16), vec(sgu_ln_g[l]), vec(sgu_ln_b[l]),
                 w_spatial[l].astype(BF16), jnp.repeat(b_spatial[l].T, SGU_GW, axis=1))
        g1, g2 = vec(g_norm1[l]), vec(g_norm2[l])

        qx, kx, vx, rx = _inproj(x, lat_row, g1, sc1, sh1, w_in_bf, gq_t, gk_t, cos_x, sin_x, seg32)
        qc, kc, vc, rc = _inproj(ctx, ctx_row, g1, sc1, sh1, w_in_bf, gq_t, gk_t, cos_c, sin_c, seg32)

        attn_x = _attention(qx, [(kc, vc), (kx, vx)], lams, go_t, gq_t, gk_t, lam_init)
        x, h2x, aff_x = _mixout(x, attn_x, rx, lat_row, mix_w, w_out_bf, gt1, g2, sc2, sh2, wr_bf)
        pos_x, gate_x, ptok_x = _route(aff_x, tri, EC_CAPACITY * T // N_EXPERTS)
        groups = [_gather(pos_x, gate_x, h2x)]

        if not last:
            attn_c = _attention(qc, [(kc, vc)], lams, go_t, gq_t, gk_t, lam_init)
            ctx, h2c, aff_c = _mixout(ctx, attn_c, rc, ctx_row, mix_w, w_out_bf, gt1, g2, sc2, sh2, wr_bf)
            pos_c, gate_c, ptok_c = _route(aff_c, tri, EC_CAPACITY * C // N_EXPERTS)
            groups.append(_gather(pos_c, gate_c, h2c))

        ys = _moe(groups, l, w_gate, w_up, w_down)
        x = _combine(x, gt2, lat_row, ptok_x, ys[0])
        if not last:
            ctx = _combine(ctx, gt2, ctx_row, ptok_c, ys[1])
    return x
```

```python
import functools
import math

import jax
import jax.numpy as jnp
from jax import lax
from jax.experimental import pallas as pl
from jax.experimental.pallas import tpu as pltpu

F32 = jnp.float32
BF16 = jnp.bfloat16

D_MODEL = 1024
GRID_W = 64
GROUP_W = 256
ATTN_HEADS = 4
ATTN_D = 32
ATTN_VD = 64
ROPE_THETA = 10000.0
POOL_WINDOWS = (2, 4, 8, 16)
CONV_K = 31
SGU_CHUNK = 128
SGU_GROUPS = 4
SGU_GW = 64
N_EXPERTS = 16
EC_CAPACITY = 2
N_MOD = 6
IN_W = 2048
EPS = 1e-6
LOG2E = 1.4426950408889634
MAX_UNSTABILISED_SCORE = 40.0
MAX_UNSTABILISED_LAM = 2.0 ** 30
SCORE_BOUND_SLACK = 1.02
SUB_ULP_STEPS = 12

MOD_ROWS = 16
LANES = 128
SUBLANES = 8
HALO = 16
TIME_CHUNK = 256
VMEM_LIMIT = 56 * 1024 * 1024

MOD_COLS = 1024
INPROJ_ROWS = 1024
INPROJ_SUB = 512
ATTN_Q_ROWS = 512
MIX_ROWS = 4 * TIME_CHUNK
FFN_COLS = 512
FFN_ROWS = 512
COMBINE_ROWS = 512
COMBINE_FLAT_PAIRS = 512


def _params(sem):
    return pltpu.CompilerParams(dimension_semantics=sem, vmem_limit_bytes=VMEM_LIMIT)


def _dot(a, b):
    return jnp.dot(a, b, preferred_element_type=F32)


def _split_dot(a, w):
    hi = a.astype(BF16)
    lo = (a - hi.astype(F32)).astype(BF16)
    return _dot(hi, w) + _dot(lo, w)


def _silu(x):
    return x * jax.nn.sigmoid(x)


def _layer_norm(x, g, b):
    mu = jnp.mean(x, axis=-1, keepdims=True)
    xc = x - mu
    var = jnp.mean(xc * xc, axis=-1, keepdims=True)
    return xc * lax.rsqrt(var + EPS) * g + b


def _mod_kernel(c_ref, w_ref, b_ref, o_ref):
    s = _silu(c_ref[...])
    o_ref[...] = _dot(s.astype(BF16), w_ref[...].astype(BF16)) + b_ref[...]


def _modulation(cond, w_mod, b_mod):
    L, D, N = w_mod.shape
    tn = MOD_COLS
    return pl.pallas_call(
        _mod_kernel,
        grid=(L, N // tn),
        in_specs=[
            pl.BlockSpec((MOD_ROWS, D), lambda l, j: (0, 0)),
            pl.BlockSpec((None, D, tn), lambda l, j: (l, 0, j)),
            pl.BlockSpec((None, 1, tn), lambda l, j: (l, 0, j)),
        ],
        out_specs=pl.BlockSpec((None, MOD_ROWS, tn), lambda l, j: (l, 0, j)),
        out_shape=jax.ShapeDtypeStruct((L, MOD_ROWS, N), F32),
        compiler_params=_params(("arbitrary", "arbitrary")),
        name="modulation",
    )(cond, w_mod, b_mod.reshape(L, 1, N))


def _inproj_kernel(x_ref, g_ref, sc_ref, sh_ref, w_ref, gq_ref, gk_ref, cos_ref, sin_ref, seg_ref,
                   *out_refs, sub, kv_only):
    if kv_only:
        k_ref, v_ref = out_refs
    else:
        q_ref, k_ref, v_ref, r_ref = out_refs
    gain = g_ref[...] * (1.0 + sc_ref[...])
    seg = seg_ref[...]
    n_freq = ATTN_D // 4
    first = (lax.broadcasted_iota(jnp.int32, (sub, GROUP_W), 1) % (2 * n_freq)) < n_freq

    for r in range(x_ref.shape[0] // sub):
        rows = slice(r * sub, (r + 1) * sub)
        x = x_ref[rows, :]
        h = x * lax.rsqrt(jnp.mean(x * x, axis=-1, keepdims=True) + EPS) * gain + sh_ref[...]
        p = _dot(h.astype(BF16), w_ref[...])
        cos = cos_ref[rows, :]
        sin = sin_ref[rows, :]

        def prep(a, g):
            ss = _split_dot(a * a, seg)
            n = a * lax.rsqrt(ss * (1.0 / ATTN_D) + EPS) * g
            partner = jnp.where(first, pltpu.roll(n, GROUP_W - n_freq, 1), pltpu.roll(n, n_freq, 1))
            return n * cos + partner * sin

        if kv_only:
            k_ref[:, rows] = jnp.transpose(prep(p[:, 0:GROUP_W], gk_ref[...])).astype(BF16)
            v_ref[rows, :] = p[:, GROUP_W:2 * GROUP_W].astype(BF16)
            continue
        q_ref[rows, :] = (prep(p[:, 0:GROUP_W], gq_ref[...]) * (ATTN_D ** -0.5 * LOG2E)).astype(BF16)
        k_ref[:, rows] = jnp.transpose(prep(p[:, GROUP_W:2 * GROUP_W], gk_ref[...])).astype(BF16)
        v_ref[rows, :] = p[:, 2 * GROUP_W:3 * GROUP_W].astype(BF16)
        r_ref[rows, :] = p[:, 3 * GROUP_W:]


def _inproj(x, mod_row, g1, sc, sh, w_bf, gq_t, gk_t, cos_t, sin_t, seg32, kv_only=False):
    B, T, D = x.shape
    tm = min(INPROJ_ROWS, T)
    rest_w = IN_W - 3 * GROUP_W
    row = lambda b, i: (mod_row(b), 0, 0)
    const2 = lambda b, i: (0, 0)
    tok = lambda b, i: (b, i, 0)
    outs = [
        (pl.BlockSpec((None, tm, GROUP_W), tok), jax.ShapeDtypeStruct((B, T, GROUP_W), BF16)),
        (pl.BlockSpec((None, GROUP_W, tm), lambda b, i: (b, 0, i)), jax.ShapeDtypeStruct((B, GROUP_W, T), BF16)),
        (pl.BlockSpec((None, tm, GROUP_W), tok), jax.ShapeDtypeStruct((B, T, GROUP_W), BF16)),
        (pl.BlockSpec((None, tm, rest_w), tok), jax.ShapeDtypeStruct((B, T, rest_w), F32)),
    ]
    if kv_only:
        w_bf = w_bf[:, GROUP_W:3 * GROUP_W]
        outs = outs[1:3]
    res = pl.pallas_call(
        functools.partial(_inproj_kernel, sub=min(INPROJ_SUB, tm), kv_only=kv_only),
        grid=(B, T // tm),
        in_specs=[
            pl.BlockSpec((None, tm, D), tok),
            pl.BlockSpec((1, D), const2),
            pl.BlockSpec((None, 1, D), row),
            pl.BlockSpec((None, 1, D), row),
            pl.BlockSpec(w_bf.shape, const2),
            pl.BlockSpec((1, GROUP_W), const2),
            pl.BlockSpec((1, GROUP_W), const2),
            pl.BlockSpec((tm, GROUP_W), lambda b, i: (i, 0)),
            pl.BlockSpec((tm, GROUP_W), lambda b, i: (i, 0)),
            pl.BlockSpec((GROUP_W, GROUP_W), const2),
        ],
        out_specs=[spec for spec, _ in outs],
        out_shape=[shape for _, shape in outs],
        compiler_params=_params(("arbitrary", "arbitrary")),
        name="inproj",
    )(x, g1, sc, sh, w_bf, gq_t, gk_t, cos_t, sin_t, seg32)
    return (None, *res, None) if kv_only else res


def _attn_kernel(*refs, n_seg, lam_init):
    q_ref = refs[0]
    kv_refs = refs[1:1 + 2 * n_seg]
    lq1_ref, lk1_ref, lq2_ref, lk2_ref, go_ref, gq_ref, gk_ref, o_ref = refs[1 + 2 * n_seg:]
    lam = (jnp.exp(jnp.sum(lq1_ref[...] * lk1_ref[...], axis=-1, keepdims=True))
           - jnp.exp(jnp.sum(lq2_ref[...] * lk2_ref[...], axis=-1, keepdims=True)) + lam_init)
    q = q_ref[...]

    def scores(off):
        qs = q[:, off:off + ATTN_D]
        return [_dot(qs, kv_refs[2 * i][off:off + ATTN_D, :]) for i in range(n_seg)]

    def run(stabilise):
        def softmax_parts(s):
            if stabilise:
                mx = functools.reduce(jnp.maximum, [jnp.max(si, axis=-1, keepdims=True) for si in s])
                s = [si - mx for si in s]
            p = [jnp.exp2(si) for si in s]
            den = functools.reduce(jnp.add, [jnp.sum(pi, axis=-1, keepdims=True) for pi in p])
            return [pi.astype(BF16) for pi in p], den

        heads = []
        nxt = (scores(0), scores(ATTN_D))
        for h in range(ATTN_HEADS):
            cur = nxt
            if h + 1 < ATTN_HEADS:
                nxt = (scores((h + 1) * ATTN_VD), scores((h + 1) * ATTN_VD + ATTN_D))
            p1, l1 = softmax_parts(cur[0])
            p2, l2 = softmax_parts(cur[1])
            c = (lam * l1 / l2).astype(BF16)
            o = None
            for i in range(n_seg):
                oi = _dot(p1[i] - c * p2[i], kv_refs[2 * i + 1][:, h * ATTN_VD:(h + 1) * ATTN_VD])
                o = oi if o is None else o + oi
            o = o * (1.0 / l1)
            ms = jnp.mean(o * o, axis=-1, keepdims=True)
            heads.append(o * lax.rsqrt(ms + EPS))
        o_ref[...] = (jnp.concatenate(heads, axis=1) * go_ref[...] * (1.0 - lam_init)).astype(BF16)

    bound = (jnp.max(jnp.abs(gq_ref[...])) * jnp.max(jnp.abs(gk_ref[...]))
             * (ATTN_D * ATTN_D ** -0.5 * LOG2E * SCORE_BOUND_SLACK))
    small = jnp.logical_and(bound < MAX_UNSTABILISED_SCORE, jnp.max(jnp.abs(lam)) < MAX_UNSTABILISED_LAM)

    @pl.when(small)
    def _():
        run(False)

    @pl.when(jnp.logical_not(small))
    def _():
        run(True)


def _attention(q, kvs, lams, go_t, gq_t, gk_t, lam_init):
    B, T, _ = q.shape
    tq = min(ATTN_Q_ROWS, T)
    tok = lambda b, i: (b, i, 0)
    const2 = lambda b, i: (0, 0)
    in_specs = [pl.BlockSpec((None, tq, GROUP_W), tok)]
    args = [q]
    for kt, v in kvs:
        n = v.shape[1]
        in_specs += [pl.BlockSpec((None, GROUP_W, n), lambda b, i: (b, 0, 0)),
                     pl.BlockSpec((None, n, GROUP_W), lambda b, i: (b, 0, 0))]
        args += [kt, v]
    in_specs += [pl.BlockSpec((1, ATTN_D), const2)] * 4
    in_specs += [pl.BlockSpec((1, GROUP_W), const2)] * 3
    args += list(lams) + [go_t, gq_t, gk_t]
    return pl.pallas_call(
        functools.partial(_attn_kernel, n_seg=len(kvs), lam_init=lam_init),
        grid=(B, T // tq),
        in_specs=in_specs,
        out_specs=pl.BlockSpec((None, tq, GROUP_W), tok),
        out_shape=jax.ShapeDtypeStruct((B, T, GROUP_W), BF16),
        compiler_params=_params(("arbitrary", "arbitrary")),
        name="diff_attention",
    )(*args)


def _gelu_tanh(x):
    return 0.5 * x * (1.0 + jnp.tanh(0.7978845608028654 * (x + 0.044715 * (x * x * x))))


def _mixout_kernel(cur_ref, prev_ref, next_ref, wpool_ref, bpool_ref, pscale_ref, convw_ref, convb_ref, clng_ref,
                   clnb_ref, wpw2_ref, slng_ref, slnb_ref, wsp_ref, bsp_ref,
                   x_ref, a_ref, w_ref, gt_ref, g2_ref, sc_ref, sh_ref, wr_ref,
                   xo_ref, h_ref, aff_ref, zwin, ywin, *, T):
    i = pl.program_id(1)
    has_prev = jnp.where(i > 0, 1.0, 0.0)
    has_next = jnp.where(i < pl.num_programs(1) - 1, 1.0, 0.0)

    def glu(ref):
        return ref[:, GROUP_W:2 * GROUP_W] * jax.nn.sigmoid(ref[:, 2 * GROUP_W:3 * GROUP_W])

    step = cur_ref.shape[0]
    win_rows = step + 2 * HALO

    def fill(win, head, body, tail):
        win[0, 0:HALO, :] = head * has_prev
        win[0, HALO:HALO + step, :] = body
        win[0, HALO + step:, :] = tail * has_next
        whole = win[0]
        for r in range(1, SUBLANES):
            win[r] = pltpu.roll(whole, win_rows - r, 0)

    def shifted(win, row):
        base = row - row % SUBLANES
        return win[row % SUBLANES, base:base + TIME_CHUNK, :]

    fill(zwin, prev_ref[:, 0:GROUP_W], cur_ref[:, 0:GROUP_W], next_ref[:, 0:GROUP_W])
    fill(ywin, glu(prev_ref), glu(cur_ref), glu(next_ref))

    pool_group = lax.broadcasted_iota(jnp.int32, (TIME_CHUNK, GROUP_W), 1) // SGU_GW
    sgu_masks = [((lax.broadcasted_iota(jnp.int32, (1, GROUP_W), 1) // SGU_GW) == g).astype(F32)
                 for g in range(SGU_GROUPS)]
    gain = g2_ref[...] * (1.0 + sc_ref[...])

    for c in range(step // TIME_CHUNK):
        first = c * TIME_CHUNK
        rows = slice(first, first + TIME_CHUNK)

        def zs(off):
            return shifted(zwin, first + HALO + off)

        t = i * step + first + lax.broadcasted_iota(jnp.int32, (TIME_CHUNK, 1), 0)
        z0 = zs(0)
        sums = []
        acc = None
        for w in POOL_WINDOWS:
            for off in range(-(w // 2), w // 2):
                if acc is None or not (-(w // 4) <= off < w // 4):
                    acc = zs(off) if acc is None else acc + zs(off)
            sums.append(acc)
        mean = None
        for g, w in enumerate(POOL_WINDOWS):
            cnt = (jnp.minimum(t + w // 2, T) - jnp.maximum(t - w // 2, 0)).astype(F32)
            mg = sums[g] / cnt
            mean = mg if mean is None else jnp.where(pool_group == g, mg, mean)
        pool = (_dot((mean - z0).astype(BF16), wpool_ref[...]) + bpool_ref[...]) * pscale_ref[...]
        mix = _dot(a_ref[rows, :], w_ref[0:GROUP_W, :]) + _dot(pool.astype(BF16), w_ref[GROUP_W:2 * GROUP_W, :])

        acc = None
        for k in range(CONV_K):
            term = shifted(ywin, first + HALO + k - CONV_K // 2) * convw_ref[k:k + 1, :]
            acc = term if acc is None else acc + term
        cn = _silu(_layer_norm(acc + convb_ref[...], clng_ref[...], clnb_ref[...]))
        conv = _dot(cn.astype(BF16), wpw2_ref[...])
        mix = mix + _dot(conv.astype(BF16), w_ref[2 * GROUP_W:3 * GROUP_W, :])

        gl = _gelu_tanh(cur_ref[rows, 3 * GROUP_W:5 * GROUP_W])
        u = gl[:, 0:GROUP_W]
        vn = _layer_norm(gl[:, GROUP_W:2 * GROUP_W], slng_ref[...], slnb_ref[...])
        gated = []
        for j in range(TIME_CHUNK // SGU_CHUNK):
            sub = slice(j * SGU_CHUNK, (j + 1) * SGU_CHUNK)
            s = bsp_ref[...]
            for g in range(SGU_GROUPS):
                s = s + _dot(wsp_ref[g], (vn[sub] * sgu_masks[g]).astype(BF16))
            gated.append((u[sub] * s).astype(BF16))
        mix = mix + _dot(jnp.concatenate(gated, axis=0), w_ref[3 * GROUP_W:, :])

        xn = x_ref[rows, :] + gt_ref[...] * mix
        xo_ref[rows, :] = xn
        h = xn * lax.rsqrt(jnp.mean(xn * xn, axis=-1, keepdims=True) + EPS) * gain + sh_ref[...]
        hi = h.astype(BF16)
        h_ref[rows, :] = hi
        logits = _dot(hi, wr_ref[...])
        valid = lax.broadcasted_iota(jnp.int32, logits.shape, 1) < N_EXPERTS
        z = jnp.where(valid, logits, -jnp.inf)
        e = jnp.exp(z - jnp.max(z, axis=-1, keepdims=True))
        aff = e / jnp.sum(e, axis=-1, keepdims=True)
        aff_ref[:, rows] = jnp.transpose(aff)[0:N_EXPERTS, :]


def _mixout(x, attn, rest, mod_row, mix_w, w_bf, gt, g2, sc, sh, wr_bf):
    B, T, D = x.shape
    RW = rest.shape[-1]
    step = min(MIX_ROWS, T)
    n_step = T // step
    per_step = step // HALO
    const2 = lambda b, i: (0, 0)
    row = lambda b, i: (mod_row(b), 0, 0)
    tok = lambda b, i: (b, i, 0)
    vec = pl.BlockSpec((1, GROUP_W), const2)
    mat = pl.BlockSpec((GROUP_W, GROUP_W), const2)
    return pl.pallas_call(
        functools.partial(_mixout_kernel, T=T),
        grid=(B, n_step),
        in_specs=[
            pl.BlockSpec((None, step, RW), tok),
            pl.BlockSpec((None, HALO, RW), lambda b, i: (b, jnp.maximum(i * per_step - 1, 0), 0)),
            pl.BlockSpec((None, HALO, RW), lambda b, i: (b, jnp.minimum((i + 1) * per_step, n_step * per_step - 1), 0)),
            mat, vec, vec,
            pl.BlockSpec((CONV_K, GROUP_W), const2), vec, vec, vec, mat,
            vec, vec,
            pl.BlockSpec((SGU_GROUPS, SGU_CHUNK, SGU_CHUNK), lambda b, i: (0, 0, 0)),
            pl.BlockSpec((SGU_CHUNK, GROUP_W), const2),
            pl.BlockSpec((None, step, D), tok),
            pl.BlockSpec((None, step, GROUP_W), tok),
            pl.BlockSpec((D, D), const2),
            pl.BlockSpec((None, 1, D), row),
            pl.BlockSpec((1, D), const2),
            pl.BlockSpec((None, 1, D), row),
            pl.BlockSpec((None, 1, D), row),
            pl.BlockSpec((D, LANES), const2),
        ],
        out_specs=[
            pl.BlockSpec((None, step, D), tok),
            pl.BlockSpec((None, step, D), tok),
            pl.BlockSpec((None, N_EXPERTS, step), lambda b, i: (b, 0, i)),
        ],
        out_shape=[
            jax.ShapeDtypeStruct((B, T, D), F32),
            jax.ShapeDtypeStruct((B, T, D), BF16),
            jax.ShapeDtypeStruct((B, N_EXPERTS, T), F32),
        ],
        scratch_shapes=[pltpu.VMEM((SUBLANES, step + 2 * HALO, GROUP_W), F32)] * 2,
        compiler_params=_params(("arbitrary", "arbitrary")),
        name="mixers_outproj",
    )(rest, rest, rest, *mix_w, x, attn, w_bf, gt, g2, sc, sh, wr_bf)


def _lane_prefix(x, tri):
    outs = []
    off = jnp.zeros((x.shape[0], 1), F32)
    for j in range(x.shape[1] // LANES):
        xb = x[:, j * LANES:(j + 1) * LANES]
        inc = _dot(xb.astype(BF16), tri)
        outs.append(inc - xb + off)
        off = off + inc[:, LANES - 1:LANES]
    return jnp.concatenate(outs, axis=1)


def _route_kernel(a_ref, tri_ref, pos_ref, gate_ref, ptok_ref, *, cap, n_expert):
    a = a_ref[...]
    R, T = a.shape

    def enough(c):
        return jnp.sum(jnp.where(a >= c, 1.0, 0.0), axis=-1, keepdims=True) >= cap

    def bit_search(i, lo):
        cand = lo | lax.shift_left(jnp.int32(1), 30 - i)
        return jnp.where(enough(pltpu.bitcast(cand, F32)), cand, lo)

    lo_bits = lax.fori_loop(0, 31, bit_search, jnp.zeros((R, 1), jnp.int32))

    def refine(i, c):
        lo, hi = c
        mid = lo + 0.5 * (hi - lo)
        ok = enough(mid)
        return jnp.where(ok, mid, lo), jnp.where(ok, hi, mid)

    thr, _ = lax.fori_loop(0, SUB_ULP_STEPS, refine,
                           (pltpu.bitcast(lo_bits, F32), pltpu.bitcast(lo_bits + 1, F32)))
    gt = a > thr
    eq = a == thr
    need = cap - jnp.sum(jnp.where(gt, 1.0, 0.0), axis=-1, keepdims=True)
    tri = tri_ref[...]
    tie_rank = _lane_prefix(jnp.where(eq, 1.0, 0.0), tri)
    sel = jnp.logical_or(gt, jnp.logical_and(eq, tie_rank < need))
    pos = jnp.where(sel, _lane_prefix(jnp.where(sel, 1.0, 0.0), tri), -1.0)
    pos_ref[...] = pos
    gate_ref[...] = jnp.where(sel, a, 0.0)
    fill = jnp.full((LANES - n_expert, T), -1.0, F32)
    for b in range(R // n_expert):
        padded = jnp.concatenate([pos[b * n_expert:(b + 1) * n_expert], fill], axis=0)
        ptok_ref[b] = jnp.transpose(padded)


def _route(aff_t, tri, cap):
    B, E, T = aff_t.shape
    blk = pl.BlockSpec((B * E, T), lambda i: (0, 0))
    pos, gate, ptok = pl.pallas_call(
        functools.partial(_route_kernel, cap=cap, n_expert=E),
        grid=(1,),
        in_specs=[blk, pl.BlockSpec((LANES, LANES), lambda i: (0, 0))],
        out_specs=[blk, blk, pl.BlockSpec((B, T, LANES), lambda i: (0, 0, 0))],
        out_shape=[
            jax.ShapeDtypeStruct((B * E, T), F32),
            jax.ShapeDtypeStruct((B * E, T), F32),
            jax.ShapeDtypeStruct((B, T, LANES), F32),
        ],
        compiler_params=_params(("arbitrary",)),
        name="route",
    )(aff_t.reshape(B * E, T), tri)
    return pos.reshape(B, E, T), gate.reshape(B, E, T), ptok


def _gather_kernel(pos_ref, gate_ref, h_ref, xs_ref, gs_ref):
    E, cap, _ = xs_ref.shape
    T = pos_ref.shape[1]
    slot = lax.broadcasted_iota(jnp.int32, (cap, T), 0).astype(F32)
    for e in range(E):
        hit = slot == pos_ref[e:e + 1, :]
        onehot = jnp.where(hit, 1.0, 0.0).astype(BF16)
        xs_ref[e] = _dot(onehot, h_ref[...]).astype(BF16)
        gs_ref[e] = jnp.sum(jnp.where(hit, gate_ref[e:e + 1, :], 0.0), axis=-1, keepdims=True)


def _gather(pos_t, gate_t, h):
    B, E, T = pos_t.shape
    D = h.shape[-1]
    cap = EC_CAPACITY * T // E
    et = pl.BlockSpec((None, E, T), lambda b: (b, 0, 0))
    return pl.pallas_call(
        _gather_kernel,
        grid=(B,),
        in_specs=[et, et, pl.BlockSpec((None, T, D), lambda b: (b, 0, 0))],
        out_specs=[
            pl.BlockSpec((E, cap, D), lambda b: (0, b, 0)),
            pl.BlockSpec((E, cap, 1), lambda b: (0, b, 0)),
        ],
        out_shape=[
            jax.ShapeDtypeStruct((E, B * cap, D), BF16),
            jax.ShapeDtypeStruct((E, B * cap, 1), F32),
        ],
        compiler_params=_params(("arbitrary",)),
        name="gather",
    )(pos_t, gate_t, h)


def _moe_kernel(*refs, n_grp, tm):
    xs = refs[0:2 * n_grp:2]
    gs = refs[1:2 * n_grp:2]
    wg_ref, wu_ref, wd_ref = refs[2 * n_grp:2 * n_grp + 3]
    ys = refs[2 * n_grp + 3:3 * n_grp + 3]
    accs = refs[3 * n_grp + 3:4 * n_grp + 3]
    wgb, wub, wdb = refs[4 * n_grp + 3:]
    f = pl.program_id(1)

    @pl.when(jnp.logical_and(pl.program_id(0) == 0, f == 0))
    def _():
        for acc in accs:
            acc[...] = jnp.zeros_like(acc)

    wgb[...] = wg_ref[...].astype(BF16)
    wub[...] = wu_ref[...].astype(BF16)
    wdb[...] = wd_ref[...].astype(BF16)
    carry_on = f > 0

    for x_ref, g_ref, y_ref, acc in zip(xs, gs, ys, accs):
        rows = x_ref.shape[0]
        t = min(tm, rows)
        for r in range(rows // t):
            sl = slice(r * t, (r + 1) * t)
            xt = x_ref[sl, :]
            a = _dot(xt, wgb[...])
            u = _dot(xt, wub[...])
            total = jnp.where(carry_on, acc[sl, :], 0.0) + _dot((_silu(a) * u).astype(BF16), wdb[...])
            acc[sl, :] = total
            y_ref[sl, :] = (total * g_ref[sl, :]).astype(BF16)


def _moe(groups, layer, w_gate, w_up, w_down):
    _, E, D, F = w_gate.shape
    tf = FFN_COLS
    in_specs, args, out_specs, out_shape, scratch = [], [], [], [], []
    for xs, gs in groups:
        R = xs.shape[1]
        in_specs += [pl.BlockSpec((None, R, D), lambda e, f: (e, 0, 0)), pl.BlockSpec((None, R, 1), lambda e, f: (e, 0, 0))]
        args += [xs, gs]
        out_specs.append(pl.BlockSpec((None, R, D), lambda e, f: (e, 0, 0)))
        out_shape.append(jax.ShapeDtypeStruct((E, R, D), BF16))
        scratch.append(pltpu.VMEM((R, D), F32))
    in_specs += [
        pl.BlockSpec((None, None, D, tf), lambda e, f: (layer, e, 0, f)),
        pl.BlockSpec((None, None, D, tf), lambda e, f: (layer, e, 0, f)),
        pl.BlockSpec((None, None, tf, D), lambda e, f: (layer, e, f, 0)),
    ]
    args += [w_gate, w_up, w_down]
    scratch += [pltpu.VMEM((D, tf), BF16), pltpu.VMEM((D, tf), BF16), pltpu.VMEM((tf, D), BF16)]
    return pl.pallas_call(
        functools.partial(_moe_kernel, n_grp=len(groups), tm=FFN_ROWS),
        grid=(E, F // tf),
        in_specs=in_specs,
        out_specs=out_specs,
        out_shape=out_shape,
        scratch_shapes=scratch,
        compiler_params=_params(("arbitrary", "arbitrary")),
        name="expert_ffn",
    )(*args)


def _combine_kernel(x_ref, gt_ref, ptok_ref, ys_ref, o_ref, *, tm):
    T, D = x_ref.shape
    E, cap, _ = ys_ref.shape
    t = min(tm, T)
    flat = E * cap <= COMBINE_FLAT_PAIRS
    if flat:
        pair = lax.broadcasted_iota(jnp.int32, (LANES, E * cap), 1)
        spread = jnp.where(lax.broadcasted_iota(jnp.int32, (LANES, E * cap), 0) == pair // cap, 1.0, 0.0).astype(BF16)
        want = (lax.broadcasted_iota(jnp.int32, (1, E * cap), 1) % cap).astype(F32)
        ys_flat = ys_ref[...].reshape(E * cap, D)
    else:
        lane = lax.broadcasted_iota(jnp.int32, (t, cap), 1).astype(F32)

    def tile(r, carry):
        sl = pl.ds(pl.multiple_of(r * t, t), t)
        pt = ptok_ref[sl, :]
        if flat:
            onehot = jnp.where(_dot(pt.astype(BF16), spread) == want, 1.0, 0.0).astype(BF16)
            acc = _dot(onehot, ys_flat)
        else:
            acc = jnp.zeros((t, D), F32)
            for e in range(E):
                onehot = jnp.where(pt[:, e:e + 1] == lane, 1.0, 0.0).astype(BF16)
                acc = acc + _dot(onehot, ys_ref[e])
        o_ref[sl, :] = x_ref[sl, :] + gt_ref[...] * acc
        return carry

    lax.fori_loop(0, T // t, tile, 0)


def _combine(x, gt, mod_row, ptok, ys):
    B, T, D = x.shape
    E = ys.shape[0]
    cap = ys.shape[1] // B
    ys4 = ys.reshape(E, B, cap, D)
    return pl.pallas_call(
        functools.partial(_combine_kernel, tm=COMBINE_ROWS),
        grid=(B,),
        in_specs=[
            pl.BlockSpec((None, T, D), lambda b: (b, 0, 0)),
            pl.BlockSpec((None, 1, D), lambda b: (mod_row(b), 0, 0)),
            pl.BlockSpec((None, T, LANES), lambda b: (b, 0, 0)),
            pl.BlockSpec((E, None, cap, D), lambda b: (0, b, 0, 0)),
        ],
        out_specs=pl.BlockSpec((None, T, D), lambda b: (b, 0, 0)),
        out_shape=jax.ShapeDtypeStruct((B, T, D), F32),
        compiler_params=_params(("arbitrary",)),
        name="combine",
    )(x, gt, ptok, ys4)


def _rope_tables(T):
    rows = (jnp.arange(T) // GRID_W).astype(F32)
    cols = (jnp.arange(T) % GRID_W).astype(F32)
    n_freq = ATTN_D // 4
    inv = ROPE_THETA ** (-jnp.arange(n_freq, dtype=F32) / n_freq)
    ang_r, ang_c = rows[:, None] * inv, cols[:, None] * inv
    cos = jnp.concatenate([jnp.cos(ang_r)] * 2 + [jnp.cos(ang_c)] * 2, axis=1)
    sin = jnp.concatenate([-jnp.sin(ang_r), jnp.sin(ang_r), -jnp.sin(ang_c), jnp.sin(ang_c)], axis=1)
    reps = GROUP_W // ATTN_D
    return jnp.tile(cos, (1, reps)).astype(F32), jnp.tile(sin, (1, reps)).astype(F32)


def _segment_ones(width):
    i = jnp.arange(GROUP_W) // width
    return (i[:, None] == i[None, :]).astype(BF16)


def _block_diag(w):
    G, n, _ = w.shape
    eye = jnp.eye(G, dtype=w.dtype)
    return (eye[:, None, :, None] * w[:, :, None, :]).reshape(G * n, G * n)


def kernel(x, c, ctx, c_ctx, w_mod, b_mod, g_norm1, g_norm2, w_in, w_out, g_q, g_k, lam_q1, lam_k1, lam_q2, lam_k2,
           g_attn_out, w_pool, b_pool, pool_scale, conv_w, conv_b, conv_ln_g, conv_ln_b, w_pw2, sgu_ln_g, sgu_ln_b,
           w_spatial, b_spatial, w_router, w_gate, w_up, w_down):
    B, T, D = x.shape
    C = ctx.shape[1]
    L = w_mod.shape[0]
    assert B < MOD_ROWS and D == D_MODEL and T % TIME_CHUNK == 0 and C % TIME_CHUNK == 0

    cond = jnp.zeros((MOD_ROWS, D), F32).at[:B].set(c).at[B].set(c_ctx)
    mods = _modulation(cond, w_mod, b_mod).reshape(L, MOD_ROWS, N_MOD, 1, D)
    lat_row = lambda b: b
    ctx_row = lambda b: B

    cos_x, sin_x = _rope_tables(T)
    cos_c, sin_c = jnp.ones((C, GROUP_W), F32), jnp.zeros((C, GROUP_W), F32)
    seg32 = _segment_ones(ATTN_D)
    tri = (jnp.arange(LANES)[:, None] <= jnp.arange(LANES)[None, :]).astype(BF16)
    tile_g = lambda g: jnp.tile(g, GROUP_W // g.shape[0]).reshape(1, GROUP_W)
    vec = lambda v: v.reshape(1, -1)

    for l in range(L):
        last = l == L - 1
        lam_init = 0.8 - 0.6 * math.exp(-0.3 * l)
        sh1, sc1, gt1, sh2, sc2, gt2 = (mods[l, :, i] for i in range(N_MOD))
        w_in_bf = w_in[l].astype(BF16)
        w_out_bf = w_out[l].astype(BF16)
        wr = jnp.zeros((D, LANES), F32).at[:, :N_EXPERTS].set(w_router[l])
        wr_bf = wr.astype(BF16)
        lams = [vec(p[l]) for p in (lam_q1, lam_k1, lam_q2, lam_k2)]
        gq_t, gk_t, go_t = tile_g(g_q[l]), tile_g(g_k[l]), tile_g(g_attn_out[l])
        mix_w = (_block_diag(w_pool[l]).astype(BF16), vec(b_pool[l]), vec(pool_scale[l]), conv_w[l], vec(conv_b[l]),
                 vec(conv_ln_g[l]), vec(conv_ln_b[l]), w_pw2[l].astype(BF16), vec(sgu_ln_g[l]), vec(sgu_ln_b[l]),
                 w_spatial[l].astype(BF16), jnp.repeat(b_spatial[l].T, SGU_GW, axis=1))
        g1, g2 = vec(g_norm1[l]), vec(g_norm2[l])

        qx, kx, vx, rx = _inproj(x, lat_row, g1, sc1, sh1, w_in_bf, gq_t, gk_t, cos_x, sin_x, seg32)
        qc, kc, vc, rc = _inproj(ctx, ctx_row, g1, sc1, sh1, w_in_bf, gq_t, gk_t, cos_c, sin_c, seg32, kv_only=last)

        attn_x = _attention(qx, [(kc, vc), (kx, vx)], lams, go_t, gq_t, gk_t, lam_init)
        x, h2x, aff_x = _mixout(x, attn_x, rx, lat_row, mix_w, w_out_bf, gt1, g2, sc2, sh2, wr_bf)
        pos_x, gate_x, ptok_x = _route(aff_x, tri, EC_CAPACITY * T // N_EXPERTS)
        groups = [_gather(pos_x, gate_x, h2x)]

        if not last:
            attn_c = _attention(qc, [(kc, vc)], lams, go_t, gq_t, gk_t, lam_init)
            ctx, h2c, aff_c = _mixout(ctx, attn_c, rc, ctx_row, mix_w, w_out_bf, gt1, g2, sc2, sh2, wr_bf)
            pos_c, gate_c, ptok_c = _route(aff_c, tri, EC_CAPACITY * C // N_EXPERTS)
            groups.append(_gather(pos_c, gate_c, h2c))

        ys = _moe(groups, l, w_gate, w_up, w_down)
        x = _combine(x, gt2, lat_row, ptok_x, ys[0])
        if not last:
            ctx = _combine(ctx, gt2, ctx_row, ptok_c, ys[1])
    return x
```

```python
import functools
import math

import jax
import jax.numpy as jnp
from jax import lax
from jax.experimental import pallas as pl
from jax.experimental.pallas import tpu as pltpu

F32 = jnp.float32
BF16 = jnp.bfloat16

D_MODEL = 1024
GRID_W = 64
GROUP_W = 256
ATTN_HEADS = 4
ATTN_D = 32
ATTN_VD = 64
ROPE_THETA = 10000.0
POOL_WINDOWS = (2, 4, 8, 16)
CONV_K = 31
SGU_CHUNK = 128
SGU_GROUPS = 4
SGU_GW = 64
N_EXPERTS = 16
EC_CAPACITY = 2
N_MOD = 6
IN_W = 2048
EPS = 1e-6
LOG2E = 1.4426950408889634
MAX_UNSTABILISED_SCORE = 40.0
MAX_UNSTABILISED_LAM = 2.0 ** 30
SCORE_BOUND_SLACK = 1.02
SUB_ULP_STEPS = 12

MOD_ROWS = 16
LANES = 128
SUBLANES = 8
HALO = 16
TIME_CHUNK = 256
VMEM_LIMIT = 56 * 1024 * 1024

MOD_COLS = 1024
INPROJ_ROWS = 1024
INPROJ_SUB = 512
ATTN_Q_ROWS = 512
MIX_ROWS = 4 * TIME_CHUNK
FFN_COLS = 512
FFN_ROWS = 512
COMBINE_ROWS = 512
COMBINE_FLAT_PAIRS = 512


def _params(sem, fuse_inputs=None):
    return pltpu.CompilerParams(dimension_semantics=sem, vmem_limit_bytes=VMEM_LIMIT, allow_input_fusion=fuse_inputs)


def _dot(a, b):
    return jnp.dot(a, b, preferred_element_type=F32)


def _split_dot(a, w):
    hi = a.astype(BF16)
    lo = (a - hi.astype(F32)).astype(BF16)
    return _dot(hi, w) + _dot(lo, w)


def _silu(x):
    return x * jax.nn.sigmoid(x)


def _layer_norm(x, g, b):
    mu = jnp.mean(x, axis=-1, keepdims=True)
    xc = x - mu
    var = jnp.mean(xc * xc, axis=-1, keepdims=True)
    return xc * lax.rsqrt(var + EPS) * g + b


def _mod_kernel(c_ref, w_ref, b_ref, o_ref):
    s = _silu(c_ref[...])
    o_ref[...] = _dot(s.astype(BF16), w_ref[...].astype(BF16)) + b_ref[...]


def _modulation(cond, w_mod, b_mod):
    L, D, N = w_mod.shape
    tn = MOD_COLS
    return pl.pallas_call(
        _mod_kernel,
        grid=(L, N // tn),
        in_specs=[
            pl.BlockSpec((MOD_ROWS, D), lambda l, j: (0, 0)),
            pl.BlockSpec((None, D, tn), lambda l, j: (l, 0, j)),
            pl.BlockSpec((None, 1, tn), lambda l, j: (l, 0, j)),
        ],
        out_specs=pl.BlockSpec((None, MOD_ROWS, tn), lambda l, j: (l, 0, j)),
        out_shape=jax.ShapeDtypeStruct((L, MOD_ROWS, N), F32),
        compiler_params=_params(("arbitrary", "arbitrary")),
        name="modulation",
    )(cond, w_mod, b_mod.reshape(L, 1, N))


def _inproj_kernel(x_ref, g_ref, sc_ref, sh_ref, w_ref, gq_ref, gk_ref, cos_ref, sin_ref, seg_ref,
                   *out_refs, sub, kv_only):
    if kv_only:
        k_ref, v_ref = out_refs
    else:
        q_ref, k_ref, v_ref, r_ref = out_refs
    gain = g_ref[...] * (1.0 + sc_ref[...])
    seg = seg_ref[...]
    n_freq = ATTN_D // 4
    first = (lax.broadcasted_iota(jnp.int32, (sub, GROUP_W), 1) % (2 * n_freq)) < n_freq

    for r in range(x_ref.shape[0] // sub):
        rows = slice(r * sub, (r + 1) * sub)
        x = x_ref[rows, :]
        h = x * lax.rsqrt(jnp.mean(x * x, axis=-1, keepdims=True) + EPS) * gain + sh_ref[...]
        p = _dot(h.astype(BF16), w_ref[...])
        cos = cos_ref[rows, :]
        sin = sin_ref[rows, :]

        def prep(a, g):
            ss = _split_dot(a * a, seg)
            n = a * lax.rsqrt(ss * (1.0 / ATTN_D) + EPS) * g
            partner = jnp.where(first, pltpu.roll(n, GROUP_W - n_freq, 1), pltpu.roll(n, n_freq, 1))
            return n * cos + partner * sin

        if kv_only:
            k_ref[:, rows] = jnp.transpose(prep(p[:, 0:GROUP_W], gk_ref[...])).astype(BF16)
            v_ref[rows, :] = p[:, GROUP_W:2 * GROUP_W].astype(BF16)
            continue
        q_ref[rows, :] = (prep(p[:, 0:GROUP_W], gq_ref[...]) * (ATTN_D ** -0.5 * LOG2E)).astype(BF16)
        k_ref[:, rows] = jnp.transpose(prep(p[:, GROUP_W:2 * GROUP_W], gk_ref[...])).astype(BF16)
        v_ref[rows, :] = p[:, 2 * GROUP_W:3 * GROUP_W].astype(BF16)
        r_ref[rows, :] = p[:, 3 * GROUP_W:]


def _inproj(x, mod_row, g1, sc, sh, w_bf, gq_t, gk_t, cos_t, sin_t, seg32, kv_only=False):
    B, T, D = x.shape
    tm = min(INPROJ_ROWS, T)
    rest_w = IN_W - 3 * GROUP_W
    row = lambda b, i: (mod_row(b), 0, 0)
    const2 = lambda b, i: (0, 0)
    tok = lambda b, i: (b, i, 0)
    outs = [
        (pl.BlockSpec((None, tm, GROUP_W), tok), jax.ShapeDtypeStruct((B, T, GROUP_W), BF16)),
        (pl.BlockSpec((None, GROUP_W, tm), lambda b, i: (b, 0, i)), jax.ShapeDtypeStruct((B, GROUP_W, T), BF16)),
        (pl.BlockSpec((None, tm, GROUP_W), tok), jax.ShapeDtypeStruct((B, T, GROUP_W), BF16)),
        (pl.BlockSpec((None, tm, rest_w), tok), jax.ShapeDtypeStruct((B, T, rest_w), F32)),
    ]
    if kv_only:
        w_bf = w_bf[:, GROUP_W:3 * GROUP_W]
        outs = outs[1:3]
    res = pl.pallas_call(
        functools.partial(_inproj_kernel, sub=min(INPROJ_SUB, tm), kv_only=kv_only),
        grid=(B, T // tm),
        in_specs=[
            pl.BlockSpec((None, tm, D), tok),
            pl.BlockSpec((1, D), const2),
            pl.BlockSpec((None, 1, D), row),
            pl.BlockSpec((None, 1, D), row),
            pl.BlockSpec(w_bf.shape, const2),
            pl.BlockSpec((1, GROUP_W), const2),
            pl.BlockSpec((1, GROUP_W), const2),
            pl.BlockSpec((tm, GROUP_W), lambda b, i: (i, 0)),
            pl.BlockSpec((tm, GROUP_W), lambda b, i: (i, 0)),
            pl.BlockSpec((GROUP_W, GROUP_W), const2),
        ],
        out_specs=[spec for spec, _ in outs],
        out_shape=[shape for _, shape in outs],
        compiler_params=_params(("arbitrary", "arbitrary"), [i == 4 for i in range(10)]),
        name="inproj",
    )(x, g1, sc, sh, w_bf, gq_t, gk_t, cos_t, sin_t, seg32)
    return (None, *res, None) if kv_only else res


def _attn_kernel(*refs, n_seg, lam_init):
    q_ref = refs[0]
    kv_refs = refs[1:1 + 2 * n_seg]
    lq1_ref, lk1_ref, lq2_ref, lk2_ref, go_ref, gq_ref, gk_ref, o_ref = refs[1 + 2 * n_seg:]
    lam = (jnp.exp(jnp.sum(lq1_ref[...] * lk1_ref[...], axis=-1, keepdims=True))
           - jnp.exp(jnp.sum(lq2_ref[...] * lk2_ref[...], axis=-1, keepdims=True)) + lam_init)
    q = q_ref[...]

    def scores(off):
        qs = q[:, off:off + ATTN_D]
        return [_dot(qs, kv_refs[2 * i][off:off + ATTN_D, :]) for i in range(n_seg)]

    def run(stabilise):
        def softmax_parts(s):
            if stabilise:
                mx = functools.reduce(jnp.maximum, [jnp.max(si, axis=-1, keepdims=True) for si in s])
                s = [si - mx for si in s]
            p = [jnp.exp2(si) for si in s]
            den = functools.reduce(jnp.add, [jnp.sum(pi, axis=-1, keepdims=True) for pi in p])
            return [pi.astype(BF16) for pi in p], den

        heads = []
        nxt = (scores(0), scores(ATTN_D))
        for h in range(ATTN_HEADS):
            cur = nxt
            if h + 1 < ATTN_HEADS:
                nxt = (scores((h + 1) * ATTN_VD), scores((h + 1) * ATTN_VD + ATTN_D))
            p1, l1 = softmax_parts(cur[0])
            p2, l2 = softmax_parts(cur[1])
            c = (lam * l1 / l2).astype(BF16)
            o = None
            for i in range(n_seg):
                oi = _dot(p1[i] - c * p2[i], kv_refs[2 * i + 1][:, h * ATTN_VD:(h + 1) * ATTN_VD])
                o = oi if o is None else o + oi
            o = o * (1.0 / l1)
            ms = jnp.mean(o * o, axis=-1, keepdims=True)
            heads.append(o * lax.rsqrt(ms + EPS))
        o_ref[...] = (jnp.concatenate(heads, axis=1) * go_ref[...] * (1.0 - lam_init)).astype(BF16)

    bound = (jnp.max(jnp.abs(gq_ref[...])) * jnp.max(jnp.abs(gk_ref[...]))
             * (ATTN_D * ATTN_D ** -0.5 * LOG2E * SCORE_BOUND_SLACK))
    small = jnp.logical_and(bound < MAX_UNSTABILISED_SCORE, jnp.max(jnp.abs(lam)) < MAX_UNSTABILISED_LAM)

    @pl.when(small)
    def _():
        run(False)

    @pl.when(jnp.logical_not(small))
    def _():
        run(True)


def _attention(q, kvs, lams, go_t, gq_t, gk_t, lam_init):
    B, T, _ = q.shape
    tq = min(ATTN_Q_ROWS, T)
    tok = lambda b, i: (b, i, 0)
    const2 = lambda b, i: (0, 0)
    in_specs = [pl.BlockSpec((None, tq, GROUP_W), tok)]
    args = [q]
    for kt, v in kvs:
        n = v.shape[1]
        in_specs += [pl.BlockSpec((None, GROUP_W, n), lambda b, i: (b, 0, 0)),
                     pl.BlockSpec((None, n, GROUP_W), lambda b, i: (b, 0, 0))]
        args += [kt, v]
    in_specs += [pl.BlockSpec((1, ATTN_D), const2)] * 4
    in_specs += [pl.BlockSpec((1, GROUP_W), const2)] * 3
    args += list(lams) + [go_t, gq_t, gk_t]
    return pl.pallas_call(
        functools.partial(_attn_kernel, n_seg=len(kvs), lam_init=lam_init),
        grid=(B, T // tq),
        in_specs=in_specs,
        out_specs=pl.BlockSpec((None, tq, GROUP_W), tok),
        out_shape=jax.ShapeDtypeStruct((B, T, GROUP_W), BF16),
        compiler_params=_params(("arbitrary", "arbitrary")),
        name="diff_attention",
    )(*args)


def _gelu_tanh(x):
    return 0.5 * x * (1.0 + jnp.tanh(0.7978845608028654 * (x + 0.044715 * (x * x * x))))


def _mixout_kernel(cur_ref, prev_ref, next_ref, wpool_ref, bpool_ref, pscale_ref, convw_ref, convb_ref, clng_ref,
                   clnb_ref, wpw2_ref, slng_ref, slnb_ref, wsp_ref, bsp_ref,
                   x_ref, a_ref, w_ref, gt_ref, g2_ref, sc_ref, sh_ref, wr_ref,
                   xo_ref, h_ref, aff_ref, zwin, ywin, *, T):
    i = pl.program_id(1)
    has_prev = jnp.where(i > 0, 1.0, 0.0)
    has_next = jnp.where(i < pl.num_programs(1) - 1, 1.0, 0.0)

    def glu(ref):
        return ref[:, GROUP_W:2 * GROUP_W] * jax.nn.sigmoid(ref[:, 2 * GROUP_W:3 * GROUP_W])

    step = cur_ref.shape[0]
    win_rows = step + 2 * HALO

    def fill(win, head, body, tail):
        win[0, 0:HALO, :] = head * has_prev
        win[0, HALO:HALO + step, :] = body
        win[0, HALO + step:, :] = tail * has_next
        whole = win[0]
        for r in range(1, SUBLANES):
            win[r] = pltpu.roll(whole, win_rows - r, 0)

    def shifted(win, row):
        base = row - row % SUBLANES
        return win[row % SUBLANES, base:base + TIME_CHUNK, :]

    fill(zwin, prev_ref[:, 0:GROUP_W], cur_ref[:, 0:GROUP_W], next_ref[:, 0:GROUP_W])
    fill(ywin, glu(prev_ref), glu(cur_ref), glu(next_ref))

    pool_group = lax.broadcasted_iota(jnp.int32, (TIME_CHUNK, GROUP_W), 1) // SGU_GW
    sgu_masks = [((lax.broadcasted_iota(jnp.int32, (1, GROUP_W), 1) // SGU_GW) == g).astype(F32)
                 for g in range(SGU_GROUPS)]
    gain = g2_ref[...] * (1.0 + sc_ref[...])

    for c in range(step // TIME_CHUNK):
        first = c * TIME_CHUNK
        rows = slice(first, first + TIME_CHUNK)

        def zs(off):
            return shifted(zwin, first + HALO + off)

        t = i * step + first + lax.broadcasted_iota(jnp.int32, (TIME_CHUNK, 1), 0)
        z0 = zs(0)
        sums = []
        acc = None
        for w in POOL_WINDOWS:
            for off in range(-(w // 2), w // 2):
                if acc is None or not (-(w // 4) <= off < w // 4):
                    acc = zs(off) if acc is None else acc + zs(off)
            sums.append(acc)
        mean = None
        for g, w in enumerate(POOL_WINDOWS):
            cnt = (jnp.minimum(t + w // 2, T) - jnp.maximum(t - w // 2, 0)).astype(F32)
            mg = sums[g] / cnt
            mean = mg if mean is None else jnp.where(pool_group == g, mg, mean)
        pool = (_dot((mean - z0).astype(BF16), wpool_ref[...]) + bpool_ref[...]) * pscale_ref[...]
        mix = _dot(a_ref[rows, :], w_ref[0:GROUP_W, :]) + _dot(pool.astype(BF16), w_ref[GROUP_W:2 * GROUP_W, :])

        acc = None
        for k in range(CONV_K):
            term = shifted(ywin, first + HALO + k - CONV_K // 2) * convw_ref[k:k + 1, :]
            acc = term if acc is None else acc + term
        cn = _silu(_layer_norm(acc + convb_ref[...], clng_ref[...], clnb_ref[...]))
        conv = _dot(cn.astype(BF16), wpw2_ref[...])
        mix = mix + _dot(conv.astype(BF16), w_ref[2 * GROUP_W:3 * GROUP_W, :])

        gl = _gelu_tanh(cur_ref[rows, 3 * GROUP_W:5 * GROUP_W])
        u = gl[:, 0:GROUP_W]
        vn = _layer_norm(gl[:, GROUP_W:2 * GROUP_W], slng_ref[...], slnb_ref[...])
        gated = []
        for j in range(TIME_CHUNK // SGU_CHUNK):
            sub = slice(j * SGU_CHUNK, (j + 1) * SGU_CHUNK)
            s = bsp_ref[...]
            for g in range(SGU_GROUPS):
                s = s + _dot(wsp_ref[g], (vn[sub] * sgu_masks[g]).astype(BF16))
            gated.append((u[sub] * s).astype(BF16))
        mix = mix + _dot(jnp.concatenate(gated, axis=0), w_ref[3 * GROUP_W:, :])

        xn = x_ref[rows, :] + gt_ref[...] * mix
        xo_ref[rows, :] = xn
        h = xn * lax.rsqrt(jnp.mean(xn * xn, axis=-1, keepdims=True) + EPS) * gain + sh_ref[...]
        hi = h.astype(BF16)
        h_ref[rows, :] = hi
        logits = _dot(hi, wr_ref[...])
        valid = lax.broadcasted_iota(jnp.int32, logits.shape, 1) < N_EXPERTS
        z = jnp.where(valid, logits, -jnp.inf)
        e = jnp.exp(z - jnp.max(z, axis=-1, keepdims=True))
        aff = e / jnp.sum(e, axis=-1, keepdims=True)
        aff_ref[:, rows] = jnp.transpose(aff)[0:N_EXPERTS, :]


def _mixout(x, attn, rest, mod_row, mix_w, w_bf, gt, g2, sc, sh, wr_bf):
    B, T, D = x.shape
    RW = rest.shape[-1]
    step = min(MIX_ROWS, T)
    n_step = T // step
    per_step = step // HALO
    const2 = lambda b, i: (0, 0)
    row = lambda b, i: (mod_row(b), 0, 0)
    tok = lambda b, i: (b, i, 0)
    vec = pl.BlockSpec((1, GROUP_W), const2)
    mat = pl.BlockSpec((GROUP_W, GROUP_W), const2)
    return pl.pallas_call(
        functools.partial(_mixout_kernel, T=T),
        grid=(B, n_step),
        in_specs=[
            pl.BlockSpec((None, step, RW), tok),
            pl.BlockSpec((None, HALO, RW), lambda b, i: (b, jnp.maximum(i * per_step - 1, 0), 0)),
            pl.BlockSpec((None, HALO, RW), lambda b, i: (b, jnp.minimum((i + 1) * per_step, n_step * per_step - 1), 0)),
            mat, vec, vec,
            pl.BlockSpec((CONV_K, GROUP_W), const2), vec, vec, vec, mat,
            vec, vec,
            pl.BlockSpec((SGU_GROUPS, SGU_CHUNK, SGU_CHUNK), lambda b, i: (0, 0, 0)),
            pl.BlockSpec((SGU_CHUNK, GROUP_W), const2),
            pl.BlockSpec((None, step, D), tok),
            pl.BlockSpec((None, step, GROUP_W), tok),
            pl.BlockSpec((D, D), const2),
            pl.BlockSpec((None, 1, D), row),
            pl.BlockSpec((1, D), const2),
            pl.BlockSpec((None, 1, D), row),
            pl.BlockSpec((None, 1, D), row),
            pl.BlockSpec((D, LANES), const2),
        ],
        out_specs=[
            pl.BlockSpec((None, step, D), tok),
            pl.BlockSpec((None, step, D), tok),
            pl.BlockSpec((None, N_EXPERTS, step), lambda b, i: (b, 0, i)),
        ],
        out_shape=[
            jax.ShapeDtypeStruct((B, T, D), F32),
            jax.ShapeDtypeStruct((B, T, D), BF16),
            jax.ShapeDtypeStruct((B, N_EXPERTS, T), F32),
        ],
        scratch_shapes=[pltpu.VMEM((SUBLANES, step + 2 * HALO, GROUP_W), F32)] * 2,
        compiler_params=_params(("arbitrary", "arbitrary"), [i == 17 for i in range(23)]),
        name="mixers_outproj",
    )(rest, rest, rest, *mix_w, x, attn, w_bf, gt, g2, sc, sh, wr_bf)


def _lane_prefix(x, tri):
    outs = []
    off = jnp.zeros((x.shape[0], 1), F32)
    for j in range(x.shape[1] // LANES):
        xb = x[:, j * LANES:(j + 1) * LANES]
        inc = _dot(xb.astype(BF16), tri)
        outs.append(inc - xb + off)
        off = off + inc[:, LANES - 1:LANES]
    return jnp.concatenate(outs, axis=1)


def _route_kernel(a_ref, tri_ref, pos_ref, gate_ref, ptok_ref, *, cap, n_expert):
    a = a_ref[...]
    R, T = a.shape

    def enough(c):
        return jnp.sum(jnp.where(a >= c, 1.0, 0.0), axis=-1, keepdims=True) >= cap

    def bit_search(i, lo):
        cand = lo | lax.shift_left(jnp.int32(1), 30 - i)
        return jnp.where(enough(pltpu.bitcast(cand, F32)), cand, lo)

    lo_bits = lax.fori_loop(0, 31, bit_search, jnp.zeros((R, 1), jnp.int32))

    def refine(i, c):
        lo, hi = c
        mid = lo + 0.5 * (hi - lo)
        ok = enough(mid)
        return jnp.where(ok, mid, lo), jnp.where(ok, hi, mid)

    thr, _ = lax.fori_loop(0, SUB_ULP_STEPS, refine,
                           (pltpu.bitcast(lo_bits, F32), pltpu.bitcast(lo_bits + 1, F32)))
    gt = a > thr
    eq = a == thr
    need = cap - jnp.sum(jnp.where(gt, 1.0, 0.0), axis=-1, keepdims=True)
    tri = tri_ref[...]
    tie_rank = _lane_prefix(jnp.where(eq, 1.0, 0.0), tri)
    sel = jnp.logical_or(gt, jnp.logical_and(eq, tie_rank < need))
    pos = jnp.where(sel, _lane_prefix(jnp.where(sel, 1.0, 0.0), tri), -1.0)
    pos_ref[...] = pos
    gate_ref[...] = jnp.where(sel, a, 0.0)
    fill = jnp.full((LANES - n_expert, T), -1.0, F32)
    for b in range(R // n_expert):
        padded = jnp.concatenate([pos[b * n_expert:(b + 1) * n_expert], fill], axis=0)
        ptok_ref[b] = jnp.transpose(padded)


def _route(aff_t, tri, cap):
    B, E, T = aff_t.shape
    blk = pl.BlockSpec((B * E, T), lambda i: (0, 0))
    pos, gate, ptok = pl.pallas_call(
        functools.partial(_route_kernel, cap=cap, n_expert=E),
        grid=(1,),
        in_specs=[blk, pl.BlockSpec((LANES, LANES), lambda i: (0, 0))],
        out_specs=[blk, blk, pl.BlockSpec((B, T, LANES), lambda i: (0, 0, 0))],
        out_shape=[
            jax.ShapeDtypeStruct((B * E, T), F32),
            jax.ShapeDtypeStruct((B * E, T), F32),
            jax.ShapeDtypeStruct((B, T, LANES), F32),
        ],
        compiler_params=_params(("arbitrary",)),
        name="route",
    )(aff_t.reshape(B * E, T), tri)
    return pos.reshape(B, E, T), gate.reshape(B, E, T), ptok


def _gather_kernel(pos_ref, gate_ref, h_ref, xs_ref, gs_ref):
    E, cap, _ = xs_ref.shape
    T = pos_ref.shape[1]
    slot = lax.broadcasted_iota(jnp.int32, (cap, T), 0).astype(F32)
    for e in range(E):
        hit = slot == pos_ref[e:e + 1, :]
        onehot = jnp.where(hit, 1.0, 0.0).astype(BF16)
        xs_ref[e] = _dot(onehot, h_ref[...]).astype(BF16)
        gs_ref[e] = jnp.sum(jnp.where(hit, gate_ref[e:e + 1, :], 0.0), axis=-1, keepdims=True)


def _gather(pos_t, gate_t, h):
    B, E, T = pos_t.shape
    D = h.shape[-1]
    cap = EC_CAPACITY * T // E
    et = pl.BlockSpec((None, E, T), lambda b: (b, 0, 0))
    return pl.pallas_call(
        _gather_kernel,
        grid=(B,),
        in_specs=[et, et, pl.BlockSpec((None, T, D), lambda b: (b, 0, 0))],
        out_specs=[
            pl.BlockSpec((E, cap, D), lambda b: (0, b, 0)),
            pl.BlockSpec((E, cap, 1), lambda b: (0, b, 0)),
        ],
        out_shape=[
            jax.ShapeDtypeStruct((E, B * cap, D), BF16),
            jax.ShapeDtypeStruct((E, B * cap, 1), F32),
        ],
        compiler_params=_params(("arbitrary",)),
        name="gather",
    )(pos_t, gate_t, h)


def _moe_kernel(*refs, n_grp, tm):
    xs = refs[0:2 * n_grp:2]
    gs = refs[1:2 * n_grp:2]
    wg_ref, wu_ref, wd_ref = refs[2 * n_grp:2 * n_grp + 3]
    ys = refs[2 * n_grp + 3:3 * n_grp + 3]
    accs = refs[3 * n_grp + 3:4 * n_grp + 3]
    wgb, wub, wdb = refs[4 * n_grp + 3:]
    f = pl.program_id(1)

    @pl.when(jnp.logical_and(pl.program_id(0) == 0, f == 0))
    def _():
        for acc in accs:
            acc[...] = jnp.zeros_like(acc)

    wgb[...] = wg_ref[...].astype(BF16)
    wub[...] = wu_ref[...].astype(BF16)
    wdb[...] = wd_ref[...].astype(BF16)
    carry_on = f > 0

    for x_ref, g_ref, y_ref, acc in zip(xs, gs, ys, accs):
        rows = x_ref.shape[0]
        t = min(tm, rows)
        for r in range(rows // t):
            sl = slice(r * t, (r + 1) * t)
            xt = x_ref[sl, :]
            a = _dot(xt, wgb[...])
            u = _dot(xt, wub[...])
            total = jnp.where(carry_on, acc[sl, :], 0.0) + _dot((_silu(a) * u).astype(BF16), wdb[...])
            acc[sl, :] = total
            y_ref[sl, :] = (total * g_ref[sl, :]).astype(BF16)


def _moe(groups, layer, w_gate, w_up, w_down):
    _, E, D, F = w_gate.shape
    tf = FFN_COLS
    in_specs, args, out_specs, out_shape, scratch = [], [], [], [], []
    for xs, gs in groups:
        R = xs.shape[1]
        in_specs += [pl.BlockSpec((None, R, D), lambda e, f: (e, 0, 0)), pl.BlockSpec((None, R, 1), lambda e, f: (e, 0, 0))]
        args += [xs, gs]
        out_specs.append(pl.BlockSpec((None, R, D), lambda e, f: (e, 0, 0)))
        out_shape.append(jax.ShapeDtypeStruct((E, R, D), BF16))
        scratch.append(pltpu.VMEM((R, D), F32))
    in_specs += [
        pl.BlockSpec((None, None, D, tf), lambda e, f: (layer, e, 0, f)),
        pl.BlockSpec((None, None, D, tf), lambda e, f: (layer, e, 0, f)),
        pl.BlockSpec((None, None, tf, D), lambda e, f: (layer, e, f, 0)),
    ]
    args += [w_gate, w_up, w_down]
    scratch += [pltpu.VMEM((D, tf), BF16), pltpu.VMEM((D, tf), BF16), pltpu.VMEM((tf, D), BF16)]
    return pl.pallas_call(
        functools.partial(_moe_kernel, n_grp=len(groups), tm=FFN_ROWS),
        grid=(E, F // tf),
        in_specs=in_specs,
        out_specs=out_specs,
        out_shape=out_shape,
        scratch_shapes=scratch,
        compiler_params=_params(("arbitrary", "arbitrary")),
        name="expert_ffn",
    )(*args)


def _combine_kernel(x_ref, gt_ref, ptok_ref, ys_ref, o_ref, *, tm):
    T, D = x_ref.shape
    E, cap, _ = ys_ref.shape
    t = min(tm, T)
    flat = E * cap <= COMBINE_FLAT_PAIRS
    if flat:
        pair = lax.broadcasted_iota(jnp.int32, (LANES, E * cap), 1)
        spread = jnp.where(lax.broadcasted_iota(jnp.int32, (LANES, E * cap), 0) == pair // cap, 1.0, 0.0).astype(BF16)
        want = (lax.broadcasted_iota(jnp.int32, (1, E * cap), 1) % cap).astype(F32)
        ys_flat = ys_ref[...].reshape(E * cap, D)
    else:
        lane = lax.broadcasted_iota(jnp.int32, (t, cap), 1).astype(F32)

    def tile(r, carry):
        sl = pl.ds(pl.multiple_of(r * t, t), t)
        pt = ptok_ref[sl, :]
        if flat:
            onehot = jnp.where(_dot(pt.astype(BF16), spread) == want, 1.0, 0.0).astype(BF16)
            acc = _dot(onehot, ys_flat)
        else:
            acc = jnp.zeros((t, D), F32)
            for e in range(E):
                onehot = jnp.where(pt[:, e:e + 1] == lane, 1.0, 0.0).astype(BF16)
                acc = acc + _dot(onehot, ys_ref[e])
        o_ref[sl, :] = x_ref[sl, :] + gt_ref[...] * acc
        return carry

    lax.fori_loop(0, T // t, tile, 0)


def _combine(x, gt, mod_row, ptok, ys):
    B, T, D = x.shape
    E = ys.shape[0]
    cap = ys.shape[1] // B
    ys4 = ys.reshape(E, B, cap, D)
    return pl.pallas_call(
        functools.partial(_combine_kernel, tm=COMBINE_ROWS),
        grid=(B,),
        in_specs=[
            pl.BlockSpec((None, T, D), lambda b: (b, 0, 0)),
            pl.BlockSpec((None, 1, D), lambda b: (mod_row(b), 0, 0)),
            pl.BlockSpec((None, T, LANES), lambda b: (b, 0, 0)),
            pl.BlockSpec((E, None, cap, D), lambda b: (0, b, 0, 0)),
        ],
        out_specs=pl.BlockSpec((None, T, D), lambda b: (b, 0, 0)),
        out_shape=jax.ShapeDtypeStruct((B, T, D), F32),
        compiler_params=_params(("arbitrary",)),
        name="combine",
    )(x, gt, ptok, ys4)


def _rope_tables(T):
    rows = (jnp.arange(T) // GRID_W).astype(F32)
    cols = (jnp.arange(T) % GRID_W).astype(F32)
    n_freq = ATTN_D // 4
    inv = ROPE_THETA ** (-jnp.arange(n_freq, dtype=F32) / n_freq)
    ang_r, ang_c = rows[:, None] * inv, cols[:, None] * inv
    cos = jnp.concatenate([jnp.cos(ang_r)] * 2 + [jnp.cos(ang_c)] * 2, axis=1)
    sin = jnp.concatenate([-jnp.sin(ang_r), jnp.sin(ang_r), -jnp.sin(ang_c), jnp.sin(ang_c)], axis=1)
    reps = GROUP_W // ATTN_D
    return jnp.tile(cos, (1, reps)).astype(F32), jnp.tile(sin, (1, reps)).astype(F32)


def _segment_ones(width):
    i = jnp.arange(GROUP_W) // width
    return (i[:, None] == i[None, :]).astype(BF16)


def _block_diag(w):
    G, n, _ = w.shape
    eye = jnp.eye(G, dtype=w.dtype)
    return (eye[:, None, :, None] * w[:, :, None, :]).reshape(G * n, G * n)


def kernel(x, c, ctx, c_ctx, w_mod, b_mod, g_norm1, g_norm2, w_in, w_out, g_q, g_k, lam_q1, lam_k1, lam_q2, lam_k2,
           g_attn_out, w_pool, b_pool, pool_scale, conv_w, conv_b, conv_ln_g, conv_ln_b, w_pw2, sgu_ln_g, sgu_ln_b,
           w_spatial, b_spatial, w_router, w_gate, w_up, w_down):
    B, T, D = x.shape
    C = ctx.shape[1]
    L = w_mod.shape[0]
    assert B < MOD_ROWS and D == D_MODEL and T % TIME_CHUNK == 0 and C % TIME_CHUNK == 0

    cond = jnp.zeros((MOD_ROWS, D), F32).at[:B].set(c).at[B].set(c_ctx)
    mods = _modulation(cond, w_mod, b_mod).reshape(L, MOD_ROWS, N_MOD, 1, D)
    lat_row = lambda b: b
    ctx_row = lambda b: B

    cos_x, sin_x = _rope_tables(T)
    cos_c, sin_c = jnp.ones((C, GROUP_W), F32), jnp.zeros((C, GROUP_W), F32)
    seg32 = _segment_ones(ATTN_D)
    tri = (jnp.arange(LANES)[:, None] <= jnp.arange(LANES)[None, :]).astype(BF16)
    tile_g = lambda g: jnp.tile(g, GROUP_W // g.shape[0]).reshape(1, GROUP_W)
    vec = lambda v: v.reshape(1, -1)

    for l in range(L):
        last = l == L - 1
        lam_init = 0.8 - 0.6 * math.exp(-0.3 * l)
        sh1, sc1, gt1, sh2, sc2, gt2 = (mods[l, :, i] for i in range(N_MOD))
        w_in_bf = w_in[l].astype(BF16)
        w_out_bf = w_out[l].astype(BF16)
        wr = jnp.zeros((D, LANES), F32).at[:, :N_EXPERTS].set(w_router[l])
        wr_bf = wr.astype(BF16)
        lams = [vec(p[l]) for p in (lam_q1, lam_k1, lam_q2, lam_k2)]
        gq_t, gk_t, go_t = tile_g(g_q[l]), tile_g(g_k[l]), tile_g(g_attn_out[l])
        mix_w = (_block_diag(w_pool[l]).astype(BF16), vec(b_pool[l]), vec(pool_scale[l]), conv_w[l], vec(conv_b[l]),
                 vec(conv_ln_g[l]), vec(conv_ln_b[l]), w_pw2[l].astype(BF16), vec(sgu_ln_g[l]), vec(sgu_ln_b[l]),
                 w_spatial[l].astype(BF16), jnp.repeat(b_spatial[l].T, SGU_GW, axis=1))
        g1, g2 = vec(g_norm1[l]), vec(g_norm2[l])

        qx, kx, vx, rx = _inproj(x, lat_row, g1, sc1, sh1, w_in_bf, gq_t, gk_t, cos_x, sin_x, seg32)
        qc, kc, vc, rc = _inproj(ctx, ctx_row, g1, sc1, sh1, w_in_bf, gq_t, gk_t, cos_c, sin_c, seg32, kv_only=last)

        attn_x = _attention(qx, [(kc, vc), (kx, vx)], lams, go_t, gq_t, gk_t, lam_init)
        x, h2x, aff_x = _mixout(x, attn_x, rx, lat_row, mix_w, w_out_bf, gt1, g2, sc2, sh2, wr_bf)
        pos_x, gate_x, ptok_x = _route(aff_x, tri, EC_CAPACITY * T // N_EXPERTS)
        groups = [_gather(pos_x, gate_x, h2x)]

        if not last:
            attn_c = _attention(qc, [(kc, vc)], lams, go_t, gq_t, gk_t, lam_init)
            ctx, h2c, aff_c = _mixout(ctx, attn_c, rc, ctx_row, mix_w, w_out_bf, gt1, g2, sc2, sh2, wr_bf)
            pos_c, gate_c, ptok_c = _route(aff_c, tri, EC_CAPACITY * C // N_EXPERTS)
            groups.append(_gather(pos_c, gate_c, h2c))

        ys = _moe(groups, l, w_gate, w_up, w_down)
        x = _combine(x, gt2, lat_row, ptok_x, ys[0])
        if not last:
            ctx = _combine(ctx, gt2, ctx_row, ptok_c, ys[1])
    return x
```
